```python
import math
import jax
import jax.numpy as jnp
from jax import lax
import numpy as np

D_MODEL = 1024
BATCH = 8
SEQ = 2048
DEPTH = 2

HEAD_DIM = 64
GRID_W = 64
N_MEM = 256
A_HEADS = 8
A_WIDTH = A_HEADS * HEAD_DIM
A_GROUPS = ((128, 1), (512, 4), (2048, 16))
BAND_BLOCK = 64
B_HEADS = 8
B_WIDTH = B_HEADS * HEAD_DIM
B_DECAY_LORA = 64
B_ICLR_LORA = 64
B_GN_EPS = 64e-5
B_MIX_COLS = 3 * B_WIDTH + 2 * B_DECAY_LORA + 2 * B_ICLR_LORA
C_HEADS = 8
C_KV_HEADS = 2
C_REP = C_HEADS // C_KV_HEADS
C_WIDTH = C_HEADS * HEAD_DIM
C_KV_WIDTH = C_KV_HEADS * HEAD_DIM
ROPE_THETA = 10000.0
D_HEADS = 4
D_VDIM = 2 * HEAD_DIM
D_QK_WIDTH = D_HEADS * 2 * HEAD_DIM
D_V_WIDTH = D_HEADS * D_VDIM
M_HEADS = 4
M_WIDTH = M_HEADS * HEAD_DIM
NUM_BUCKETS = 32
REL_MAX_DISTANCE = 1024
QBLK = 128
N_BRANCHES = 5

A_SPLIT = (A_WIDTH, A_WIDTH, A_WIDTH, A_WIDTH)
B_SPLIT = (B_WIDTH, B_WIDTH, B_WIDTH, 2 * B_DECAY_LORA, 2 * B_ICLR_LORA, B_WIDTH)
C_SPLIT = (C_WIDTH, C_KV_WIDTH, C_KV_WIDTH, C_WIDTH)
D_SPLIT = (D_QK_WIDTH, D_QK_WIDTH, D_V_WIDTH, D_V_WIDTH)
M_SPLIT = (M_WIDTH, M_WIDTH)
BRANCH_WIDTHS = (A_WIDTH, B_WIDTH, C_WIDTH, D_V_WIDTH, M_WIDTH)

F32 = jnp.float32
NEG_INF = -1e30

kernel_name = 'hybrid_gated_parallel_encoder'


def split_last(t, sizes):
    parts, start = [], 0
    for size in sizes:
        parts.append(t[..., start:start + size])
        start += size
    return parts


def to_heads(t, n_heads):
    b, s, _ = t.shape
    return t.reshape(b, s, n_heads, -1).transpose(0, 2, 1, 3)


def from_heads(t):
    b, h, s, d = t.shape
    return t.transpose(0, 2, 1, 3).reshape(b, s, h * d)


def layer_norm(t, g, b, eps=1e-5):
    tf = t.astype(F32)
    mu = jnp.mean(tf, -1, keepdims=True)
    var = jnp.mean(jnp.square(tf - mu), -1, keepdims=True)
    return ((tf - mu) * lax.rsqrt(var + eps) * g + b).astype(t.dtype)


def rms_norm(t, g, eps=1e-6):
    tf = t.astype(F32)
    return (tf * lax.rsqrt(jnp.mean(jnp.square(tf), -1, keepdims=True) + eps) * g).astype(t.dtype)


def rel_bucket(rel):
    nb = NUM_BUCKETS // 2
    max_exact = nb // 2
    n = jnp.abs(rel)
    nf = jnp.maximum(n, 1).astype(F32)
    large = max_exact + (jnp.log(nf / max_exact) / math.log(REL_MAX_DISTANCE / max_exact)
                         * (nb - max_exact)).astype(jnp.int32)
    large = jnp.minimum(large, nb - 1)
    return jnp.where(rel > 0, nb, 0) + jnp.where(n < max_exact, n, large)


def dilated_branch(q, k, v, table, window, dil):
    b, h, s, d = q.shape
    radius = window // (2 * dil)
    L = s // dil
    nblk = -(-L // BAND_BLOCK)
    lp = nblk * BAND_BLOCK
    kw = BAND_BLOCK + 2 * radius

    def strided(t):
        return t.reshape(b, h, L, dil, d).swapaxes(2, 3)

    qs = jnp.pad(strided(q), ((0, 0), (0, 0), (0, 0), (0, lp - L), (0, 0)))
    kpad = ((0, 0), (0, 0), (0, 0), (radius, radius + lp - L), (0, 0))
    ks = jnp.pad(strided(k), kpad)
    vs = jnp.pad(strided(v), kpad)
    qb = qs.reshape(b, h, dil, nblk, BAND_BLOCK, d)
    idx = (jnp.arange(nblk) * BAND_BLOCK)[:, None] + jnp.arange(kw)[None, :]
    kb = ks[:, :, :, idx]
    vb = vs[:, :, :, idx]
    off = jnp.arange(kw)[None, :] - radius - jnp.arange(BAND_BLOCK)[:, None]
    bias = jnp.moveaxis(table[rel_bucket(off * dil)], -1, 0).astype(F32)
    kpos = idx - radius
    valid = (jnp.abs(off) <= radius)[None] & ((kpos >= 0) & (kpos < L))[:, None, :]
    logits = jnp.einsum('bhrnqd,bhrnkd->bhrnqk', qb, kb).astype(F32) * (d ** -0.5) + bias[:, None, None]
    logits = jnp.where(valid, logits, NEG_INF)
    m = jnp.max(logits, -1, keepdims=True)
    e = jnp.exp(logits - m)
    den = jnp.sum(e, -1, keepdims=True)
    o = jnp.einsum('bhrnqk,bhrnkd->bhrnqd', e, vb.astype(F32)) / den
    lse = (m + jnp.log(den))[..., 0]
    o = o.reshape(b, h, dil, lp, d)[:, :, :, :L].swapaxes(2, 3).reshape(b, h, s, d)
    lse = lse.reshape(b, h, dil, lp)[..., :L].swapaxes(2, 3).reshape(b, h, s)
    return o, lse


def mixer_dilated(pa, table_a):
    q, k, v, g = split_last(pa, A_SPLIT)
    q, k, v = (to_heads(t, A_HEADS) for t in (q, k, v))
    outs, lses = [], []
    for window, dil in A_GROUPS:
        o, lse = dilated_branch(q, k, v, table_a, window, dil)
        outs.append(o)
        lses.append(lse)
    wts = jax.nn.softmax(jnp.stack(lses), axis=0)
    o = jnp.sum(wts[..., None] * jnp.stack(outs), axis=0)
    return from_heads(o).astype(pa.dtype) * jax.nn.silu(g)


def rwkv7_step(state, inp):
    r, w, k, v, a, bb = inp
    sa = jnp.einsum('bhij,bhj->bhi', state, a)
    state = state * w[:, :, None, :] + sa[..., None] * bb[:, :, None, :] + v[..., None] * k[:, :, None, :]
    return state, jnp.einsum('bhij,bhj->bhi', state, r)


def mixer_rwkv(pb, mu, w0, w_up, a0, a_up, k_k, k_a, r_k, ln_g, ln_b):
    b, s, _ = pb.shape
    xm, g = pb[..., :B_MIX_COLS], pb[..., B_MIX_COLS:]
    prev = jnp.pad(xm[:, :-1], ((0, 0), (1, 0), (0, 0)))
    nxt = jnp.pad(xm[:, 1:], ((0, 0), (0, 1), (0, 0)))
    xm = xm + mu[0] * (prev - xm) + mu[1] * (nxt - xm)
    r, k, v, wd, ad = (t.astype(F32) for t in split_last(xm, B_SPLIT[:5]))
    wd = wd.reshape(b, s, 2, B_DECAY_LORA)
    ad = ad.reshape(b, s, 2, B_ICLR_LORA)
    w_log = -jax.nn.softplus(-(w0 + jnp.einsum('bser,erc->bsec', jnp.tanh(wd), w_up))) - 0.5
    decay = jnp.exp(-jnp.exp(w_log.astype(F32)))
    a = jax.nn.sigmoid((a0 + jnp.einsum('bser,erc->bsec', ad, a_up)).astype(F32))
    kk = (k * k_k).reshape(b, s, B_HEADS, HEAD_DIM)
    kk = (kk / jnp.maximum(jnp.linalg.norm(kk, axis=-1, keepdims=True), 1e-12)).reshape(b, s, B_WIDTH)

    def seq_heads(t):
        return t.reshape(b, s, B_HEADS, HEAD_DIM).transpose(1, 0, 2, 3)

    rh, vh, ah = seq_heads(r), seq_heads(v), seq_heads(-kk)
    ys, bonuses = [], []
    for e, rev in ((0, False), (1, True)):
        ke = k * (1.0 + (a[:, :, e] - 1.0) * k_a)
        be = kk * a[:, :, e]
        xs = (rh, seq_heads(decay[:, :, e]), seq_heads(ke), vh, ah, seq_heads(be))
        _, ye = lax.scan(rwkv7_step, jnp.zeros((b, B_HEADS, HEAD_DIM, HEAD_DIM), F32), xs, reverse=rev)
        ys.append(ye)
        bonuses.append(jnp.sum((r * ke * r_k.reshape(-1)).reshape(b, s, B_HEADS, HEAD_DIM), -1, keepdims=True))
    y = (ys[0] + ys[1]).transpose(1, 0, 2, 3)
    mu_y = jnp.mean(y, -1, keepdims=True)
    var_y = jnp.mean(jnp.square(y - mu_y), -1, keepdims=True)
    gn = ((y - mu_y) * lax.rsqrt(var_y + B_GN_EPS)).reshape(b, s, B_WIDTH) * ln_g + ln_b
    bonus = ((bonuses[0] + bonuses[1]) * v.reshape(b, s, B_HEADS, HEAD_DIM)).reshape(b, s, B_WIDTH)
    return (gn + bonus).astype(pb.dtype) * jax.nn.silu(g)


def axial_rope(t, row, col):
    d = t.shape[-1]
    half = d // 2
    qtr = half // 2
    freqs = ROPE_THETA ** (-(jnp.arange(qtr, dtype=F32) / qtr))

    def rot(u, pos):
        ang = pos.astype(F32)[:, None] * freqs[None, :]
        c, sn = jnp.cos(ang), jnp.sin(ang)
        u1, u2 = u[..., :qtr], u[..., qtr:]
        return jnp.concatenate([u1 * c - u2 * sn, u1 * sn + u2 * c], -1)

    tf = t.astype(F32)
    return jnp.concatenate([rot(tf[..., :half], row), rot(tf[..., half:], col)], -1).astype(t.dtype)


def mixer_axial_gqa(pc, qn_g, kn_g, row, col):
    b, s, _ = pc.shape
    q, k, v, g = split_last(pc, C_SPLIT)
    q = axial_rope(rms_norm(q.reshape(b, s, C_HEADS, HEAD_DIM), qn_g).transpose(0, 2, 1, 3), row, col)
    k = axial_rope(rms_norm(k.reshape(b, s, C_KV_HEADS, HEAD_DIM), kn_g).transpose(0, 2, 1, 3), row, col)
    v = to_heads(v, C_KV_HEADS)
    q = q.reshape(b, C_KV_HEADS, C_REP, s, HEAD_DIM)
    scale = HEAD_DIM ** -0.5

    def block(i):
        qb = lax.dynamic_slice_in_dim(q, i * QBLK, QBLK, axis=3)
        p = jax.nn.softmax(jnp.einsum('bgrqd,bgkd->bgrqk', qb, k).astype(F32) * scale, axis=-1)
        return jnp.einsum('bgrqk,bgkd->bgrqd', p, v.astype(F32))

    o = lax.map(block, jnp.arange(s // QBLK))
    o = jnp.moveaxis(o, 0, 3).reshape(b, C_HEADS, s, HEAD_DIM)
    return from_heads(o).astype(pc.dtype) * jax.nn.silu(g)


def mixer_diff(pd, lam_params, subln_g, table_d, layer_idx):
    b, s, _ = pd.shape
    q, k, v, g = split_last(pd, D_SPLIT)
    q = q.reshape(b, s, D_HEADS, 2, HEAD_DIM).transpose(3, 0, 2, 1, 4)
    k = k.reshape(b, s, D_HEADS, 2, HEAD_DIM).transpose(3, 0, 2, 1, 4)
    v = to_heads(v, D_HEADS).astype(F32)
    lam_init = 0.8 - 0.6 * math.exp(-0.3 * layer_idx)
    lq1, lk1, lq2, lk2 = lam_params[0], lam_params[1], lam_params[2], lam_params[3]
    lam = (jnp.exp(jnp.sum(lq1 * lk1)) - jnp.exp(jnp.sum(lq2 * lk2)) + lam_init).astype(F32)
    scale = HEAD_DIM ** -0.5
    kpos = jnp.arange(s)

    def block(i):
        start = i * QBLK
        qb = lax.dynamic_slice_in_dim(q, start, QBLK, axis=3)
        rel = kpos[None, :] - (start + jnp.arange(QBLK))[:, None]
        bias = jnp.moveaxis(table_d[rel_bucket(rel)], -1, 0).astype(F32)
        p = jax.nn.softmax(jnp.einsum('cbhqd,cbhkd->cbhqk', qb, k).astype(F32) * scale + bias, axis=-1)
        return jnp.einsum('bhqk,bhkd->bhqd', p[0] - lam * p[1], v)

    o = lax.map(block, jnp.arange(s // QBLK))
    o = jnp.moveaxis(o, 0, 2).reshape(b, D_HEADS, s, D_VDIM)
    o = rms_norm(o, subln_g, eps=1e-5) * (1.0 - lam_init)
    return from_heads(o).astype(pd.dtype) * jax.nn.silu(g)


def mixer_memory(pm, mem, w_mem_kv):
    q, g = split_last(pm, M_SPLIT)
    q = to_heads(q, M_HEADS)
    km, vm = split_last(jnp.einsum('bmd,dc->bmc', mem, w_mem_kv), (M_WIDTH, M_WIDTH))
    km, vm = to_heads(km, M_HEADS), to_heads(vm, M_HEADS)
    p = jax.nn.softmax(jnp.einsum('bhqd,bhmd->bhqm', q, km).astype(F32) * (HEAD_DIM ** -0.5), axis=-1)
    o = jnp.einsum('bhqm,bhmd->bhqd', p, vm.astype(F32))
    return from_heads(o).astype(pm.dtype) * jax.nn.silu(g)


def gated_merge(h, branch_outs, w_branch, w_gate, b_gate):
    d = h.shape[-1]
    terms, start = [], 0
    for i, (o, wdt) in enumerate(zip(branch_outs, BRANCH_WIDTHS)):
        proj = jnp.einsum('bsc,cd->bsd', o, w_branch[start:start + wdt])
        gate = jax.nn.sigmoid(jnp.einsum('bsd,de->bse', h, w_gate[:, i * d:(i + 1) * d]) + b_gate[i * d:(i + 1) * d])
        terms.append(gate * proj)
        start += wdt
    return sum(terms[1:], terms[0])


def setup_inputs(seed: int = 0) -> dict:
    key = jax.random.key(seed)
    ks = jax.random.split(key, 28)
    d = D_MODEL
    in_cols = sum(A_SPLIT) + sum(B_SPLIT) + sum(C_SPLIT) + sum(D_SPLIT) + sum(M_SPLIT)
    beta = (8 * DEPTH) ** -0.25

    def nrm(k, shape, scale):
        return scale * jax.random.normal(k, shape, F32)

    w_branch = jnp.concatenate(
        [nrm(jax.random.fold_in(ks[21], i), (DEPTH, wdt, d), beta * wdt ** -0.5)
         for i, wdt in enumerate(BRANCH_WIDTHS)], axis=1)
    return {
        'x': nrm(ks[0], (BATCH, SEQ, d), 1.0),
        'mem': nrm(ks[1], (BATCH, N_MEM, d), 1.0),
        'ln_in_g': 1.0 + nrm(ks[2], (d,), 0.02),
        'ln_in_b': nrm(ks[3], (d,), 0.02),
        'rel_bias': nrm(ks[4], (NUM_BUCKETS, A_HEADS + D_HEADS), 0.5),
        'w_in': nrm(ks[5], (DEPTH, d, in_cols), d ** -0.5),
        'shift_mu': jax.random.uniform(ks[6], (DEPTH, 2, B_MIX_COLS), F32, 0.0, 0.5),
        'rwkv_w0': jax.random.uniform(ks[7], (DEPTH, 2, B_WIDTH), F32, -6.5, -1.0),
        'rwkv_w_up': nrm(ks[8], (DEPTH, 2, B_DECAY_LORA, B_WIDTH), 0.1),
        'rwkv_a0': nrm(ks[9], (DEPTH, 2, B_WIDTH), 0.1),
        'rwkv_a_up': nrm(ks[10], (DEPTH, 2, B_ICLR_LORA, B_WIDTH), 0.5 * B_ICLR_LORA ** -0.5),
        'rwkv_k_k': 0.85 + nrm(ks[11], (DEPTH, B_WIDTH), 0.05),
        'rwkv_k_a': 1.0 + nrm(ks[12], (DEPTH, B_WIDTH), 0.05),
        'rwkv_r_k': nrm(ks[13], (DEPTH, B_HEADS, HEAD_DIM), 0.1),
        'rwkv_ln_g': 1.0 + nrm(ks[14], (DEPTH, B_WIDTH), 0.02),
        'rwkv_ln_b': nrm(ks[15], (DEPTH, B_WIDTH), 0.02),
        'c_qnorm_g': 1.0 + nrm(ks[16], (DEPTH, HEAD_DIM), 0.02),
        'c_knorm_g': 1.0 + nrm(ks[17], (DEPTH, HEAD_DIM), 0.02),
        'd_lambda': nrm(ks[18], (DEPTH, 4, HEAD_DIM), 0.1),
        'd_subln_g': 1.0 + nrm(ks[19], (DEPTH, D_VDIM), 0.02),
        'w_mem_kv': nrm(ks[20], (DEPTH, d, 2 * M_WIDTH), d ** -0.5),
        'w_branch': w_branch,
        'w_gate': nrm(ks[22], (DEPTH, d, N_BRANCHES * d), d ** -0.5),
        'b_gate': nrm(ks[23], (DEPTH, N_BRANCHES * d), 0.1),
        'w_out': nrm(ks[24], (DEPTH, d, d), beta * d ** -0.5),
        'ln_g': 1.0 + nrm(ks[25], (DEPTH, d), 0.02),
        'ln_b': nrm(ks[26], (DEPTH, d), 0.02),
    }


def reference(x, mem, ln_in_g, ln_in_b, rel_bias, w_in, shift_mu, rwkv_w0, rwkv_w_up, rwkv_a0,
              rwkv_a_up, rwkv_k_k, rwkv_k_a, rwkv_r_k, rwkv_ln_g, rwkv_ln_b, c_qnorm_g, c_knorm_g,
              d_lambda, d_subln_g, w_mem_kv, w_branch, w_gate, b_gate, w_out, ln_g, ln_b):
    s = x.shape[1]
    rows = s // GRID_W
    row = jnp.repeat(jnp.arange(rows, dtype=jnp.int32), GRID_W)
    col = jnp.tile(jnp.arange(GRID_W, dtype=jnp.int32), rows)
    table_a = rel_bias[:, :A_HEADS]
    table_d = rel_bias[:, A_HEADS:]
    alpha = (2 * DEPTH) ** 0.25
    mixer_sizes = (sum(A_SPLIT), sum(B_SPLIT), sum(C_SPLIT), sum(D_SPLIT), sum(M_SPLIT))
    h = layer_norm(x, ln_in_g, ln_in_b)
    for l in range(DEPTH):
        p = jnp.einsum('bsd,dc->bsc', h, w_in[l])
        pa, pb, pc, pd, pm = split_last(p, mixer_sizes)
        o_a = mixer_dilated(pa, table_a)
        o_b = mixer_rwkv(pb, shift_mu[l], rwkv_w0[l], rwkv_w_up[l], rwkv_a0[l], rwkv_a_up[l],
                         rwkv_k_k[l], rwkv_k_a[l], rwkv_r_k[l], rwkv_ln_g[l], rwkv_ln_b[l])
        o_c = mixer_axial_gqa(pc, c_qnorm_g[l], c_knorm_g[l], row, col)
        o_d = mixer_diff(pd, d_lambda[l], d_subln_g[l], table_d, l)
        o_m = mixer_memory(pm, mem, w_mem_kv[l])
        y = gated_merge(h, (o_a, o_b, o_c, o_d, o_m), w_branch[l], w_gate[l], b_gate[l])
        out = jnp.einsum('bsd,de->bse', y, w_out[l])
        h = layer_norm(alpha * h + out, ln_g[l], ln_b[l])
    return h
```

```python
import functools
import math

import numpy as np
import jax
import jax.numpy as jnp
from jax import lax
from jax.experimental import pallas as pl
from jax.experimental.pallas import tpu as pltpu

F32 = jnp.float32
BF16 = jnp.bfloat16

LANES = 128
HEAD = 64
VMEM_LIMIT = 56 * 1024 * 1024

D_MODEL = 1024
DEPTH = 2
GRID_W = 64
ROPE_THETA = 10000.0
NUM_BUCKETS = 32
REL_MAX_DISTANCE = 1024
A_HEADS = 8
D_HEADS = 4
A_GROUPS = ((128, 1), (512, 4), (2048, 16))
B_GN_EPS = 64e-5
NEG_INF = -1e30

A_Q, A_K, A_V, A_G = 0, 512, 1024, 1536
B_R, B_K, B_V, B_WA, B_G = 2048, 2560, 3072, 3584, 3840
C_Q, C_K, C_V, C_G = 4352, 4864, 4992, 5120
D_Q, D_K, D_V, D_G = 5632, 6144, 6656, 7168
M_Q, M_G = 7680, 7936
IN_COLS = 8192

TQ = 128
CHUNK = 64


def _dot(a, b):
    return jnp.dot(a, b, preferred_element_type=F32)


def _dot_nt(a, b):
    return lax.dot_general(a, b, (((1,), (1,)), ((), ())), preferred_element_type=F32)


def _silu(g):
    return g / (1.0 + jnp.exp(-g))


def _params(*sem):
    return pltpu.CompilerParams(dimension_semantics=sem, vmem_limit_bytes=VMEM_LIMIT)


def _ln_in_kernel(x_ref, g_ref, b_ref, h_ref, hb_ref):
    x = x_ref[...]
    mu = jnp.mean(x, -1, keepdims=True)
    d = x - mu
    var = jnp.mean(d * d, -1, keepdims=True)
    h = d * lax.rsqrt(var + 1e-5) * g_ref[...] + b_ref[...]
    h_ref[...] = h
    hb_ref[...] = h.astype(BF16)


def _ln_in(x2, g, b):
    n, d = x2.shape
    tm = 512
    return pl.pallas_call(
        _ln_in_kernel,
        grid=(n // tm,),
        in_specs=[pl.BlockSpec((tm, d), lambda i: (i, 0)),
                  pl.BlockSpec((1, d), lambda i: (0, 0)),
                  pl.BlockSpec((1, d), lambda i: (0, 0))],
        out_specs=[pl.BlockSpec((tm, d), lambda i: (i, 0)),
                   pl.BlockSpec((tm, d), lambda i: (i, 0))],
        out_shape=[jax.ShapeDtypeStruct((n, d), F32), jax.ShapeDtypeStruct((n, d), BF16)],
        compiler_params=_params("parallel"),
        name="ln_in",
    )(x2, g.reshape(1, d), b.reshape(1, d))


def _matmul_kernel(a_ref, w_ref, o_ref):
    o_ref[...] = _dot(a_ref[...], w_ref[...])


def _matmul(a, w, tm, tn, name):
    m, k = a.shape
    n = w.shape[1]
    return pl.pallas_call(
        _matmul_kernel,
        grid=(n // tn, m // tm),
        in_specs=[pl.BlockSpec((tm, k), lambda j, i: (i, 0)),
                  pl.BlockSpec((k, tn), lambda j, i: (0, j))],
        out_specs=pl.BlockSpec((tm, tn), lambda j, i: (i, j)),
        out_shape=jax.ShapeDtypeStruct((m, n), F32),
        compiler_params=_params("parallel", "parallel"),
        name=name,
    )(a, w)


def _rel_bucket_np(rel):
    nb = NUM_BUCKETS // 2
    max_exact = nb // 2
    n = np.abs(rel)
    nf = np.maximum(n, 1).astype(np.float32)
    large = max_exact + (np.log(nf / np.float32(max_exact)) / np.float32(math.log(REL_MAX_DISTANCE / max_exact))
                         * np.float32(nb - max_exact)).astype(np.int32)
    large = np.minimum(large, nb - 1)
    return (np.where(rel > 0, nb, 0) + np.where(n < max_exact, n, large)).astype(np.int32)


def _tile_deltas(n_blk):
    d = np.arange(2 * n_blk - 1)[:, None, None] - (n_blk - 1)
    r = np.arange(TQ)[None, :, None]
    c = np.arange(TQ)[None, None, :]
    return d * TQ + c - r


def _dilated_log_multiplicity(delta):
    mult = np.zeros(delta.shape, np.float32)
    for window, dil in A_GROUPS:
        mult += ((delta % dil == 0) & (np.abs(delta) <= window // 2)).astype(np.float32)
    with np.errstate(divide="ignore"):
        return np.where(mult > 0, np.log(np.maximum(mult, 1.0)), NEG_INF).astype(np.float32)


def _bias_kernel(table_ref, bucket_ref, base_ref, o_ref, *, n_a):
    h = pl.program_id(0)
    bucket = bucket_ref[...]
    base = base_ref[...]
    acc = jnp.where(h < n_a, base, 0.0)
    for b in range(NUM_BUCKETS):
        acc = acc + jnp.where(bucket == b, table_ref[b, h], 0.0)
    o_ref[0] = acc


def _bias_tiles(rel_bias, n_blk):
    delta = _tile_deltas(n_blk)
    bucket = jnp.asarray(_rel_bucket_np(delta))
    base = jnp.asarray(_dilated_log_multiplicity(delta))
    n_heads = rel_bias.shape[1]
    nd = 2 * n_blk - 1
    return pl.pallas_call(
        functools.partial(_bias_kernel, n_a=A_HEADS),
        grid=(n_heads,),
        in_specs=[pl.BlockSpec(memory_space=pltpu.SMEM),
                  pl.BlockSpec((nd, TQ, TQ), lambda h: (0, 0, 0)),
                  pl.BlockSpec((nd, TQ, TQ), lambda h: (0, 0, 0))],
        out_specs=pl.BlockSpec((1, nd, TQ, TQ), lambda h: (h, 0, 0, 0)),
        out_shape=jax.ShapeDtypeStruct((n_heads, nd, TQ, TQ), F32),
        compiler_params=_params("arbitrary"),
        name="bias_tiles",
    )(rel_bias, bucket, base)


def _lane_lo(rows):
    return lax.broadcasted_iota(jnp.int32, (rows, LANES), 1) < HEAD


def _softmax_pv(chunks, v_bf):
    m = chunks[0]
    for c in chunks[1:]:
        m = jnp.maximum(m, c)
    m = jnp.max(m, axis=1, keepdims=True)
    es = [jnp.exp(c - m) for c in chunks]
    l = es[0]
    for e in es[1:]:
        l = l + e
    l = jnp.sum(l, axis=1, keepdims=True)
    e_bf = jnp.concatenate([e.astype(BF16) for e in es], axis=1)
    return _dot(e_bf, v_bf) / l


def _split_chunks(s):
    return [s[:, j * LANES:(j + 1) * LANES] for j in range(s.shape[1] // LANES)]


def _seg_sum(x, lo):
    s0 = jnp.sum(jnp.where(lo, x, 0.0), axis=1, keepdims=True)
    s1 = jnp.sum(jnp.where(lo, 0.0, x), axis=1, keepdims=True)
    return jnp.where(lo, s0, s1)


def _mixer_a_kernel(q_ref, k_ref, v_ref, g_ref, bias_ref, o_ref, *, n_blk):
    k_bf = k_ref[0].astype(BF16)
    v_bf = v_ref[0].astype(BF16)
    lo = _lane_lo(TQ)

    def body(qi, carry):
        r0 = pl.multiple_of(qi * TQ, TQ)
        q = q_ref[0, pl.ds(r0, TQ), :] * (HEAD ** -0.5)
        outs = []
        for hh in range(2):
            qm = jnp.where(lo if hh == 0 else jnp.logical_not(lo), q, 0.0).astype(BF16)
            s = _dot_nt(qm, k_bf)
            chunks = [c + bias_ref[hh, n_blk - 1 - qi + j] for j, c in enumerate(_split_chunks(s))]
            outs.append(_softmax_pv(chunks, v_bf))
        o = jnp.where(lo, outs[0], outs[1])
        o_ref[0, pl.ds(r0, TQ), :] = o * _silu(g_ref[0, pl.ds(r0, TQ), :])
        return carry

    lax.fori_loop(0, n_blk, body, 0)


def _mixer_a(p, bias):
    b, s, _ = p.shape
    n_blk = s // TQ
    nd = 2 * n_blk - 1
    n_pairs = A_HEADS // 2

    def col(off):
        return pl.BlockSpec((1, s, LANES), lambda bi, j, off=off: (bi, 0, off // LANES + j))

    return pl.pallas_call(
        functools.partial(_mixer_a_kernel, n_blk=n_blk),
        grid=(b, n_pairs),
        in_specs=[col(A_Q), col(A_K), col(A_V), col(A_G),
                  pl.BlockSpec((2, nd, TQ, TQ), lambda bi, j: (j, 0, 0, 0))],
        out_specs=pl.BlockSpec((1, s, LANES), lambda bi, j: (bi, 0, j)),
        out_shape=jax.ShapeDtypeStruct((b, s, n_pairs * LANES), F32),
        compiler_params=_params("parallel", "parallel"),
        name="mixer_a",
    )(p, p, p, p, bias)


def _mixer_d_kernel(q_ref, k_ref, v_ref, g_ref, bias_ref, lam_ref, sg_ref, o_ref, *, n_blk, lam_init):
    k_bf = k_ref[0].astype(BF16)
    v_bf = v_ref[0].astype(BF16)
    lo = _lane_lo(TQ)
    dl = lam_ref[...]
    lam = (jnp.exp(jnp.sum(dl[0:1] * dl[1:2], axis=1, keepdims=True))
           - jnp.exp(jnp.sum(dl[2:3] * dl[3:4], axis=1, keepdims=True)) + lam_init)
    sg = sg_ref[...] * (1.0 - lam_init)

    def body(qi, carry):
        r0 = pl.multiple_of(qi * TQ, TQ)
        q = q_ref[0, pl.ds(r0, TQ), :] * (HEAD ** -0.5)
        outs = []
        for hh in range(2):
            qm = jnp.where(lo if hh == 0 else jnp.logical_not(lo), q, 0.0).astype(BF16)
            s = _dot_nt(qm, k_bf)
            chunks = [c + bias_ref[0, n_blk - 1 - qi + j] for j, c in enumerate(_split_chunks(s))]
            outs.append(_softmax_pv(chunks, v_bf))
        o = outs[0] - lam * outs[1]
        o = o * lax.rsqrt(jnp.mean(o * o, axis=1, keepdims=True) + 1e-5) * sg
        o_ref[0, pl.ds(r0, TQ), :] = o * _silu(g_ref[0, pl.ds(r0, TQ), :])
        return carry

    lax.fori_loop(0, n_blk, body, 0)


def _mixer_d(p, bias, d_lambda, subln_g, layer_idx):
    b, s, _ = p.shape
    n_blk = s // TQ
    nd = 2 * n_blk - 1
    lam_init = 0.8 - 0.6 * math.exp(-0.3 * layer_idx)

    def col(off):
        return pl.BlockSpec((1, s, LANES), lambda bi, j, off=off: (bi, 0, off // LANES + j))

    return pl.pallas_call(
        functools.partial(_mixer_d_kernel, n_blk=n_blk, lam_init=lam_init),
        grid=(b, D_HEADS),
        in_specs=[col(D_Q), col(D_K), col(D_V), col(D_G),
                  pl.BlockSpec((1, nd, TQ, TQ), lambda bi, j: (A_HEADS + j, 0, 0, 0)),
                  pl.BlockSpec((4, HEAD), lambda bi, j: (0, 0)),
                  pl.BlockSpec((1, LANES), lambda bi, j: (0, 0))],
        out_specs=pl.BlockSpec((1, s, LANES), lambda bi, j: (bi, 0, j)),
        out_shape=jax.ShapeDtypeStruct((b, s, D_HEADS * LANES), F32),
        compiler_params=_params("parallel", "parallel"),
        name="mixer_d",
    )(p, p, p, p, bias, d_lambda, subln_g.reshape(1, LANES))


def _rope_tables(s):
    t = np.arange(s)
    row, colp = t // GRID_W, t % GRID_W
    qtr = HEAD // 4
    freqs = ROPE_THETA ** (-(jnp.arange(qtr, dtype=F32) / qtr))
    lane = np.arange(LANES) % HEAD
    use_col = (lane // (HEAD // 2)) == 1
    second = (lane % (HEAD // 2)) >= qtr
    pos = jnp.where(use_col[None, :], jnp.asarray(colp, F32)[:, None], jnp.asarray(row, F32)[:, None])
    ang = pos * freqs[lane % qtr][None, :]
    cos = jnp.cos(ang)
    sin = jnp.where(second[None, :], jnp.sin(ang), -jnp.sin(ang))
    return cos, sin


def _norm_rope(x, gain, cos, sin, lo, first):
    ms = _seg_sum(x * x, lo) * (1.0 / HEAD)
    x = x * lax.rsqrt(ms + 1e-6) * gain
    qtr = HEAD // 4
    partner = jnp.where(first, pltpu.roll(x, LANES - qtr, axis=1), pltpu.roll(x, qtr, axis=1))
    return x * cos + partner * sin


def _mixer_c_kernel(q_ref, k_ref, v_ref, g_ref, cos_ref, sin_ref, qg_ref, kg_ref, o_ref, *, n_blk, s):
    pair = pl.program_id(1)
    grp = pair // 2
    lo_s = _lane_lo(s)
    lane_s = lax.broadcasted_iota(jnp.int32, (s, LANES), 1)
    first_s = (lane_s % (HEAD // 2)) < (HEAD // 4)
    k = _norm_rope(k_ref[0], kg_ref[...], cos_ref[...], sin_ref[...], lo_s, first_s)
    v = v_ref[0]
    k_sw = pltpu.roll(k, HEAD, axis=1)
    v_sw = pltpu.roll(v, HEAD, axis=1)
    keep = jnp.logical_xor(lo_s, grp == 1)
    k_bf = jnp.where(keep, k, k_sw).astype(BF16)
    v_bf = jnp.where(keep, v, v_sw).astype(BF16)
    lo = _lane_lo(TQ)
    lane = lax.broadcasted_iota(jnp.int32, (TQ, LANES), 1)
    first = (lane % (HEAD // 2)) < (HEAD // 4)

    def body(qi, carry):
        r0 = pl.multiple_of(qi * TQ, TQ)
        q = _norm_rope(q_ref[0, pl.ds(r0, TQ), :], qg_ref[...], cos_ref[pl.ds(r0, TQ), :],
                       sin_ref[pl.ds(r0, TQ), :], lo, first) * (HEAD ** -0.5)
        outs = []
        for hh in range(2):
            qm = jnp.where(lo if hh == 0 else jnp.logical_not(lo), q, 0.0).astype(BF16)
            outs.append(_softmax_pv(_split_chunks(_dot_nt(qm, k_bf)), v_bf))
        o = jnp.where(lo, outs[0], outs[1])
        o_ref[0, pl.ds(r0, TQ), :] = o * _silu(g_ref[0, pl.ds(r0, TQ), :])
        return carry

    lax.fori_loop(0, n_blk, body, 0)


def _mixer_c(p, cos, sin, qn_g, kn_g):
    b, s, _ = p.shape
    n_blk = s // TQ
    n_pairs = 4

    def col(off):
        return pl.BlockSpec((1, s, LANES), lambda bi, j, off=off: (bi, 0, off // LANES + j))

    def fixed(off):
        return pl.BlockSpec((1, s, LANES), lambda bi, j, off=off: (bi, 0, off // LANES))

    tab = pl.BlockSpec((s, LANES), lambda bi, j: (0, 0))
    gain = pl.BlockSpec((1, LANES), lambda bi, j: (0, 0))
    return pl.pallas_call(
        functools.partial(_mixer_c_kernel, n_blk=n_blk, s=s),
        grid=(b, n_pairs),
        in_specs=[col(C_Q), fixed(C_K), fixed(C_V), col(C_G), tab, tab, gain, gain],
        out_specs=pl.BlockSpec((1, s, LANES), lambda bi, j: (bi, 0, j)),
        out_shape=jax.ShapeDtypeStruct((b, s, n_pairs * LANES), F32),
        compiler_params=_params("parallel", "parallel"),
        name="mixer_c",
    )(p, p, p, p, cos, sin, jnp.tile(qn_g, 2).reshape(1, LANES), jnp.tile(kn_g, 2).reshape(1, LANES))


def _mixer_m_kernel(q_ref, g_ref, k_ref, v_ref, o_ref, *, n_blk):
    k_bf = k_ref[0].astype(BF16)
    v_bf = v_ref[0].astype(BF16)
    lo = _lane_lo(TQ)

    def body(qi, carry):
        r0 = pl.multiple_of(qi * TQ, TQ)
        q = q_ref[0, pl.ds(r0, TQ), :] * (HEAD ** -0.5)
        outs = []
        for hh in range(2):
            qm = jnp.where(lo if hh == 0 else jnp.logical_not(lo), q, 0.0).astype(BF16)
            outs.append(_softmax_pv(_split_chunks(_dot_nt(qm, k_bf)), v_bf))
        o = jnp.where(lo, outs[0], outs[1])
        o_ref[0, pl.ds(r0, TQ), :] = o * _silu(g_ref[0, pl.ds(r0, TQ), :])
        return carry

    lax.fori_loop(0, n_blk, body, 0)


def _mixer_m(p, kv):
    b, s, _ = p.shape
    n_mem = kv.shape[1]
    n_blk = s // TQ
    n_pairs = 2

    def col(off):
        return pl.BlockSpec((1, s, LANES), lambda bi, j, off=off: (bi, 0, off // LANES + j))

    return pl.pallas_call(
        functools.partial(_mixer_m_kernel, n_blk=n_blk),
        grid=(b, n_pairs),
        in_specs=[col(M_Q), col(M_G),
                  pl.BlockSpec((1, n_mem, LANES), lambda bi, j: (bi, 0, j)),
                  pl.BlockSpec((1, n_mem, LANES), lambda bi, j: (bi, 0, n_pairs + j))],
        out_specs=pl.BlockSpec((1, s, LANES), lambda bi, j: (bi, 0, j)),
        out_shape=jax.ShapeDtypeStruct((b, s, n_pairs * LANES), F32),
        compiler_params=_params("parallel", "parallel"),
        name="mixer_m",
    )(p, p, kv, kv)


def _split3_dot(tri_bf, x):
    h1 = x.astype(BF16)
    r1 = x - h1.astype(F32)
    h2 = r1.astype(BF16)
    h3 = (r1 - h2.astype(F32)).astype(BF16)
    return _dot(tri_bf, h1) + _dot(tri_bf, h2) + _dot(tri_bf, h3)


def _rwkv_kernel(r_ref, k_ref, v_ref, wa_ref, mu_ref, w0_ref, wup_ref, a0_ref, aup_ref,
                 kk_ref, ka_ref, rk_ref, yf_ref, yb_ref, bvf_ref, bvb_ref, state_ref, *, nc, s):
    C = CHUNK
    c = pl.program_id(1)

    @pl.when(c == 0)
    def _():
        state_ref[...] = jnp.zeros_like(state_ref)

    n_pairs = r_ref.shape[2] // LANES
    width = r_ref.shape[2]
    rr = lax.broadcasted_iota(jnp.int32, (LANES, LANES), 0)
    cc = lax.broadcasted_iota(jnp.int32, (LANES, LANES), 1)
    same_head = (rr // C) == (cc // C)
    eye = rr == cc
    tr = lax.broadcasted_iota(jnp.int32, (C, C), 0)
    tc = lax.broadcasted_iota(jnp.int32, (C, C), 1)
    lo = _lane_lo(C)
    hi = jnp.logical_not(lo)

    def stack(x):
        return jnp.concatenate([jnp.where(lo, x, 0.0), jnp.where(hi, x, 0.0)], axis=0)

    for e in range(2):
        rev = e == 1
        ce = (nc - 1 - c) if rev else c
        r0 = pl.multiple_of(ce * C, C)
        y_ref, bv_ref = (yb_ref, bvb_ref) if rev else (yf_ref, bvf_ref)
        before = (rr > cc) if not rev else (rr < cc)
        strict = jnp.logical_and(same_head, before)
        incl = jnp.logical_and(same_head, jnp.logical_or(before, eye))
        tri_bf = jnp.where((tr >= tc) if not rev else (tr <= tc), 1.0, 0.0).astype(BF16)
        last = 0 if rev else C - 1

        def shifted(ref, col0, w):
            x = ref[0, pl.ds(r0, C), :]
            prev_row = ref[0, pl.ds(jnp.maximum(r0 - 1, 0), 1), :]
            prev_row = jnp.where(ce == 0, 0.0, prev_row)
            next_row = ref[0, pl.ds(jnp.minimum(r0 + C, s - 1), 1), :]
            next_row = jnp.where(ce == nc - 1, 0.0, next_row)
            rowi = lax.broadcasted_iota(jnp.int32, (C, w), 0)
            prev = jnp.where(rowi == 0, prev_row, pltpu.roll(x, 1, axis=0))
            nxt = jnp.where(rowi == C - 1, next_row, pltpu.roll(x, C - 1, axis=0))
            mu0 = mu_ref[0:1, col0:col0 + w]
            mu1 = mu_ref[1:2, col0:col0 + w]
            return x + mu0 * (prev - x) + mu1 * (nxt - x)

        rs = shifted(r_ref, 0, width)
        ks = shifted(k_ref, width, width)
        vs = shifted(v_ref, 2 * width, width)
        wa = shifted(wa_ref, 3 * width, 2 * LANES)
        sel = lo if e == 0 else hi
        wd = jnp.where(sel, jnp.tanh(wa[:, :LANES]), 0.0).astype(BF16)
        ad = jnp.where(sel, wa[:, LANES:], 0.0).astype(BF16)
        zw = w0_ref[e:e + 1, :] + _dot(wd, wup_ref[...])
        w_log = -(jnp.maximum(-zw, 0.0) + jnp.log(1.0 + jnp.exp(-jnp.abs(zw)))) - 0.5
        lw = -jnp.exp(w_log)
        za = a0_ref[e:e + 1, :] + _dot(ad, aup_ref[...])
        a_sig = 1.0 / (1.0 + jnp.exp(-za))
        kk = ks * kk_ref[...]
        kk2 = kk * kk
        nrm2 = jnp.concatenate([_seg_sum(kk2[:, i * LANES:(i + 1) * LANES], lo) for i in range(n_pairs)], axis=1)
        kkn = kk / jnp.maximum(jnp.sqrt(nrm2), 1e-12)
        ke = ks * (1.0 + (a_sig - 1.0) * ka_ref[...])
        be = kkn * a_sig
        rkr = rs * ke * rk_ref[...]
        bonus = jnp.concatenate([_seg_sum(rkr[:, i * LANES:(i + 1) * LANES], lo) for i in range(n_pairs)], axis=1)
        bv_ref[0] = bonus * vs

        l_incl = _split3_dot(tri_bf, lw)
        l_tot = l_incl[last:last + 1, :]
        w_incl = jnp.exp(l_incl)
        w_inv = jnp.exp(-l_incl)
        w_end = jnp.exp(l_tot - l_incl)
        at = -kkn * jnp.exp(l_incl - lw)
        rt = rs * w_incl
        bt = be * w_inv
        kt = ke * w_inv
        bh = be * w_end
        kh = ke * w_end
        d_end = jnp.exp(l_tot)

        ys = []
        for pr in range(n_pairs):
            sl = slice(pr * LANES, (pr + 1) * LANES)
            at_st = stack(at[:, sl])
            rt_st = stack(rt[:, sl])
            v_st = stack(vs[:, sl])
            lhs = jnp.concatenate([at_st, rt_st], axis=0).astype(BF16)
            bt_p = bt[:, sl].astype(BF16)
            kt_p = kt[:, sl].astype(BF16)
            rhs = jnp.concatenate([bt_p, bt_p, kt_p, kt_p], axis=0)
            g = _dot_nt(lhs, rhs)
            a_ab = jnp.where(strict, g[:LANES, :LANES], 0.0)
            a_ak = jnp.where(strict, g[:LANES, LANES:], 0.0)
            a_rb = jnp.where(incl, g[LANES:, :LANES], 0.0)
            a_rk = jnp.where(incl, g[LANES:, LANES:], 0.0)
            m = 1
            t_inv = None
            while m < C:
                sh = int(math.log2(m)) + 1
                same_blk = ((rr ^ cc) >> sh) == 0
                up, dn = (rr, cc) if not rev else (cc, rr)
                off = jnp.logical_and(same_blk, jnp.logical_and((up & m) != 0, (dn & m) == 0))
                a_off = jnp.where(off, a_ab, 0.0)
                if t_inv is None:
                    t_inv = jnp.where(eye, 1.0, a_off)
                else:
                    t_bf = t_inv.astype(BF16)
                    t_inv = t_inv + _dot(t_bf, _dot(a_off.astype(BF16), t_bf).astype(BF16))
                m *= 2
            t_bf = t_inv.astype(BF16)
            v_bf = v_st.astype(BF16)
            x = jnp.concatenate([at_st.astype(BF16), _dot(a_ak.astype(BF16), v_bf).astype(BF16)], axis=1)
            pq = _dot(t_bf, x)
            ry = jnp.concatenate(
                [pq.astype(BF16), jnp.concatenate([jnp.zeros((LANES, LANES), BF16), v_bf], axis=1)], axis=0)
            ly = jnp.concatenate([a_rb, a_rk], axis=1).astype(BF16)
            yy = _dot(ly, ry)
            ls = jnp.concatenate([stack(bh[:, sl]), stack(kh[:, sl])], axis=0)
            zz = _dot(ls.T.astype(BF16), ry)
            st = state_ref[e, pr]
            st_bf = st.astype(BF16)
            y_st = _dot((rt_st + yy[:, :LANES]).astype(BF16), st_bf) + yy[:, LANES:]
            d_diag = jnp.where(eye, jnp.broadcast_to(d_end[:, sl], (LANES, LANES)), 0.0)
            d_col = jnp.sum(d_diag, axis=1, keepdims=True)
            state_ref[e, pr] = d_col * st + _dot(zz[:, :LANES].astype(BF16), st_bf) + zz[:, LANES:]
            ys.append(y_st[:C] + y_st[C:])
        y_ref[0] = jnp.concatenate(ys, axis=1)


def _mixer_b_scan(p, mu, w0, w_up, a0, a_up, k_k, k_a, r_k):
    b, s, _ = p.shape
    width = 512
    nc = s // CHUNK
    n_pairs = width // LANES

    def seq(off, w):
        return pl.BlockSpec((1, s, w), lambda bi, c, off=off, w=w: (bi, 0, off // w))

    def full(shape):
        return pl.BlockSpec(shape, lambda bi, c: tuple(0 for _ in shape))

    out_f = pl.BlockSpec((1, CHUNK, width), lambda bi, c: (bi, c, 0))
    out_b = pl.BlockSpec((1, CHUNK, width), lambda bi, c: (bi, nc - 1 - c, 0))
    sds = jax.ShapeDtypeStruct((b, s, width), F32)
    lora = 2 * HEAD
    return pl.pallas_call(
        functools.partial(_rwkv_kernel, nc=nc, s=s),
        grid=(b, nc),
        in_specs=[seq(B_R, width), seq(B_K, width), seq(B_V, width), seq(B_WA, 2 * LANES),
                  full((2, 3 * width + 2 * LANES)), full((2, width)), full((lora, width)),
                  full((2, width)), full((lora, width)), full((1, width)), full((1, width)), full((1, width))],
        out_specs=[out_f, out_b, out_f, out_b],
        out_shape=[sds, sds, sds, sds],
        scratch_shapes=[pltpu.VMEM((2, n_pairs, LANES, LANES), F32)],
        compiler_params=_params("parallel", "arbitrary"),
        name="mixer_b",
    )(p, p, p, p, mu, w0, w_up.reshape(lora, width).astype(BF16), a0, a_up.reshape(lora, width).astype(BF16),
      k_k.reshape(1, width), k_a.reshape(1, width), r_k.reshape(1, width))


def _merge_kernel(h_ref, hb_ref, oa_ref, yf_ref, yb_ref, bvf_ref, bvb_ref, gb0_ref, gb1_ref, oc_ref, od_ref,
                  om_ref, wg_ref, bg_ref, wb_ref, wo_ref, lng_ref, lnb_ref, bg_g_ref, bg_b_ref,
                  hn_ref, hnb_ref, *, alpha):
    d = h_ref.shape[1]
    tm = h_ref.shape[0]
    lo = _lane_lo(tm)
    y = yf_ref[...] + yb_ref[...]
    n_pairs = y.shape[1] // LANES
    gn = []
    for i in range(n_pairs):
        yp = y[:, i * LANES:(i + 1) * LANES]
        mu = _seg_sum(yp, lo) * (1.0 / HEAD)
        dy = yp - mu
        var = _seg_sum(dy * dy, lo) * (1.0 / HEAD)
        gn.append(dy * lax.rsqrt(var + B_GN_EPS))
    gn = jnp.concatenate(gn, axis=1) * bg_g_ref[...] + bg_b_ref[...]
    gb = jnp.concatenate([gb0_ref[...], gb1_ref[...]], axis=1)
    ob = (gn + bvf_ref[...] + bvb_ref[...]) * _silu(gb)

    hb = hb_ref[...]
    branches = (oa_ref[...], ob, oc_ref[...], od_ref[...], om_ref[...])
    acc = None
    row = 0
    for i, o in enumerate(branches):
        wdt = o.shape[1]
        gate = 1.0 / (1.0 + jnp.exp(-(_dot(hb, wg_ref[:, i * d:(i + 1) * d]) + bg_ref[:, i * d:(i + 1) * d])))
        proj = _dot(o.astype(BF16), wb_ref[row:row + wdt, :])
        term = gate * proj
        acc = term if acc is None else acc + term
        row += wdt
    out = _dot(acc.astype(BF16), wo_ref[...])
    z = alpha * h_ref[...] + out
    mu = jnp.mean(z, -1, keepdims=True)
    dz = z - mu
    var = jnp.mean(dz * dz, -1, keepdims=True)
    hn = dz * lax.rsqrt(var + 1e-5) * lng_ref[...] + lnb_ref[...]
    hn_ref[...] = hn
    hnb_ref[...] = hn.astype(BF16)


def _merge(h, hb, o_a, yf, yb, bvf, bvb, p2, o_c, o_d, o_m, w_gate, b_gate, w_branch, w_out, ln_g, ln_b,
           bln_g, bln_b):
    n, d = h.shape
    tm = 256
    alpha = (2 * DEPTH) ** 0.25

    def rows(w):
        return pl.BlockSpec((tm, w), lambda i: (i, 0))

    def full(shape):
        return pl.BlockSpec(shape, lambda i: tuple(0 for _ in shape))

    gb_specs = pl.BlockSpec((tm, 256), lambda i: (i, B_G // 256))
    gb_specs2 = pl.BlockSpec((tm, 256), lambda i: (i, B_G // 256 + 1))
    return pl.pallas_call(
        functools.partial(_merge_kernel, alpha=alpha),
        grid=(n // tm,),
        in_specs=[rows(d), rows(d), rows(512), rows(512), rows(512), rows(512), rows(512),
                  gb_specs, gb_specs2, rows(512), rows(512), rows(256),
                  full(w_gate.shape), full((1, b_gate.shape[0])), full(w_branch.shape), full(w_out.shape),
                  full((1, d)), full((1, d)), full((1, 512)), full((1, 512))],
        out_specs=[rows(d), rows(d)],
        out_shape=[jax.ShapeDtypeStruct((n, d), F32), jax.ShapeDtypeStruct((n, d), BF16)],
        compiler_params=_params("parallel"),
        name="merge",
    )(h, hb, o_a, yf, yb, bvf, bvb, p2, p2, o_c, o_d, o_m,
      w_gate, b_gate.reshape(1, -1), w_branch, w_out, ln_g.reshape(1, d), ln_b.reshape(1, d),
      bln_g.reshape(1, 512), bln_b.reshape(1, 512))


def kernel(x, mem, ln_in_g, ln_in_b, rel_bias, w_in, shift_mu, rwkv_w0, rwkv_w_up, rwkv_a0, rwkv_a_up,
           rwkv_k_k, rwkv_k_a, rwkv_r_k, rwkv_ln_g, rwkv_ln_b, c_qnorm_g, c_knorm_g, d_lambda, d_subln_g,
           w_mem_kv, w_branch, w_gate, b_gate, w_out, ln_g, ln_b):
    b, s, d = x.shape
    n = b * s
    bias = _bias_tiles(rel_bias, s // TQ)
    cos, sin = _rope_tables(s)
    mem_bf = mem.reshape(b * mem.shape[1], d).astype(BF16)
    h, hb = _ln_in(x.reshape(n, d), ln_in_g, ln_in_b)
    for l in range(DEPTH):
        p2 = _matmul(hb, w_in[l].astype(BF16), 1024, 1024, "proj_in")
        p = p2.reshape(b, s, IN_COLS)
        kv = _matmul(mem_bf, w_mem_kv[l].astype(BF16), 512, 512, "proj_mem").reshape(b, mem.shape[1], -1)
        o_a = _mixer_a(p, bias)
        yf, yb, bvf, bvb = _mixer_b_scan(p, shift_mu[l], rwkv_w0[l], rwkv_w_up[l], rwkv_a0[l], rwkv_a_up[l],
                                         rwkv_k_k[l], rwkv_k_a[l], rwkv_r_k[l])
        o_c = _mixer_c(p, cos, sin, c_qnorm_g[l], c_knorm_g[l])
        o_d = _mixer_d(p, bias, d_lambda[l], d_subln_g[l], l)
        o_m = _mixer_m(p, kv)
        flat = lambda t: t.reshape(n, t.shape[-1])
        h, hb = _merge(h, hb, flat(o_a), flat(yf), flat(yb), flat(bvf), flat(bvb), p2, flat(o_c), flat(o_d),
                       flat(o_m), w_gate[l].astype(BF16), b_gate[l], w_branch[l].astype(BF16),
                       w_out[l].astype(BF16), ln_g[l], ln_b[l], rwkv_ln_g[l], rwkv_ln_b[l])
    return h.reshape(b, s, d)
```

```python
import functools
import math

import numpy as np
import jax
import jax.numpy as jnp
from jax import lax
from jax.experimental import pallas as pl
from jax.experimental.pallas import tpu as pltpu

F32 = jnp.float32
BF16 = jnp.bfloat16

LANES = 128
HEAD = 64
VMEM_LIMIT = 56 * 1024 * 1024

D_MODEL = 1024
DEPTH = 2
GRID_W = 64
ROPE_THETA = 10000.0
NUM_BUCKETS = 32
REL_MAX_DISTANCE = 1024
A_HEADS = 8
D_HEADS = 4
A_GROUPS = ((128, 1), (512, 4), (2048, 16))
B_GN_EPS = 64e-5
NEG_INF = -1e30

A_Q, A_K, A_V, A_G = 0, 512, 1024, 1536
B_R, B_K, B_V, B_WA, B_G = 2048, 2560, 3072, 3584, 3840
C_Q, C_K, C_V, C_G = 4352, 4864, 4992, 5120
D_Q, D_K, D_V, D_G = 5632, 6144, 6656, 7168
M_Q, M_G = 7680, 7936
IN_COLS = 8192

TB = 128
TQ = 256
LOG2E = math.log2(math.e)
CHUNK = 64


def _dot(a, b):
    return jnp.dot(a, b, preferred_element_type=F32)


def _dot_nt(a, b):
    return lax.dot_general(a, b, (((1,), (1,)), ((), ())), preferred_element_type=F32)


def _silu(g):
    return g / (1.0 + jnp.exp(-g))


def _params(*sem):
    return pltpu.CompilerParams(dimension_semantics=sem, vmem_limit_bytes=VMEM_LIMIT)


def _ln_in_kernel(x_ref, g_ref, b_ref, h_ref, hb_ref):
    x = x_ref[...]
    mu = jnp.mean(x, -1, keepdims=True)
    d = x - mu
    var = jnp.mean(d * d, -1, keepdims=True)
    h = d * lax.rsqrt(var + 1e-5) * g_ref[...] + b_ref[...]
    h_ref[...] = h
    hb_ref[...] = h.astype(BF16)


def _ln_in(x2, g, b):
    n, d = x2.shape
    tm = 512
    return pl.pallas_call(
        _ln_in_kernel,
        grid=(n // tm,),
        in_specs=[pl.BlockSpec((tm, d), lambda i: (i, 0)),
                  pl.BlockSpec((1, d), lambda i: (0, 0)),
                  pl.BlockSpec((1, d), lambda i: (0, 0))],
        out_specs=[pl.BlockSpec((tm, d), lambda i: (i, 0)),
                   pl.BlockSpec((tm, d), lambda i: (i, 0))],
        out_shape=[jax.ShapeDtypeStruct((n, d), F32), jax.ShapeDtypeStruct((n, d), BF16)],
        compiler_params=_params("parallel"),
        name="ln_in",
    )(x2, g.reshape(1, d), b.reshape(1, d))


def _matmul_kernel(a_ref, w_ref, o_ref):
    o_ref[...] = _dot(a_ref[...], w_ref[...])


def _matmul(a, w, tm, tn, name):
    m, k = a.shape
    n = w.shape[1]
    return pl.pallas_call(
        _matmul_kernel,
        grid=(n // tn, m // tm),
        in_specs=[pl.BlockSpec((tm, k), lambda j, i: (i, 0)),
                  pl.BlockSpec((k, tn), lambda j, i: (0, j))],
        out_specs=pl.BlockSpec((tm, tn), lambda j, i: (i, j)),
        out_shape=jax.ShapeDtypeStruct((m, n), F32),
        compiler_params=_params("parallel", "parallel"),
        name=name,
    )(a, w)


def _rel_bucket_np(rel):
    nb = NUM_BUCKETS // 2
    max_exact = nb // 2
    n = np.abs(rel)
    nf = np.maximum(n, 1).astype(np.float32)
    large = max_exact + (np.log(nf / np.float32(max_exact)) / np.float32(math.log(REL_MAX_DISTANCE / max_exact))
                         * np.float32(nb - max_exact)).astype(np.int32)
    large = np.minimum(large, nb - 1)
    return (np.where(rel > 0, nb, 0) + np.where(n < max_exact, n, large)).astype(np.int32)


def _tile_deltas(n_blk):
    d = np.arange(2 * n_blk - 1)[:, None, None] - (n_blk - 1)
    r = np.arange(TB)[None, :, None]
    c = np.arange(TB)[None, None, :]
    return d * TB + c - r


def _dilated_log_multiplicity(delta):
    mult = np.zeros(delta.shape, np.float32)
    for window, dil in A_GROUPS:
        mult += ((delta % dil == 0) & (np.abs(delta) <= window // 2)).astype(np.float32)
    with np.errstate(divide="ignore"):
        return np.where(mult > 0, np.log(np.maximum(mult, 1.0)), NEG_INF).astype(np.float32)


def _bias_kernel(table_ref, bucket_ref, base_ref, o_ref, *, n_a):
    h = pl.program_id(0)
    bucket = bucket_ref[...]
    base = base_ref[...]
    acc = jnp.where(h < n_a, base, 0.0)
    for b in range(NUM_BUCKETS):
        acc = acc + jnp.where(bucket == b, table_ref[b, h], 0.0)
    o_ref[0] = acc * LOG2E


def _bias_tiles(rel_bias, n_blk):
    delta = _tile_deltas(n_blk)
    bucket = jnp.asarray(_rel_bucket_np(delta))
    base = jnp.asarray(_dilated_log_multiplicity(delta))
    n_heads = rel_bias.shape[1]
    nd = 2 * n_blk - 1
    return pl.pallas_call(
        functools.partial(_bias_kernel, n_a=A_HEADS),
        grid=(n_heads,),
        in_specs=[pl.BlockSpec(memory_space=pltpu.SMEM),
                  pl.BlockSpec((nd, TB, TB), lambda h: (0, 0, 0)),
                  pl.BlockSpec((nd, TB, TB), lambda h: (0, 0, 0))],
        out_specs=pl.BlockSpec((1, nd, TB, TB), lambda h: (h, 0, 0, 0)),
        out_shape=jax.ShapeDtypeStruct((n_heads, nd, TB, TB), F32),
        compiler_params=_params("arbitrary"),
        name="bias_tiles",
    )(rel_bias, bucket, base)


def _lane_lo(rows):
    return lax.broadcasted_iota(jnp.int32, (rows, LANES), 1) < HEAD


def _with_ones(v_bf):
    return jnp.concatenate([v_bf, jnp.ones(v_bf.shape, BF16)], axis=1)


def _softmax_pv(s, v_ext, bias_fn=None):
    rows, keys = s.shape
    es = []
    for rb in range(rows // TB):
        chunks = [s[rb * TB:(rb + 1) * TB, j * LANES:(j + 1) * LANES] for j in range(keys // LANES)]
        if bias_fn is not None:
            chunks = [c + bias_fn(rb, j) for j, c in enumerate(chunks)]
        m = chunks[0]
        for c in chunks[1:]:
            m = jnp.maximum(m, c)
        m = jnp.max(m, axis=1, keepdims=True)
        es.append(jnp.concatenate([jnp.exp2(c - m).astype(BF16) for c in chunks], axis=1))
    o = _dot(jnp.concatenate(es, axis=0), v_ext)
    return o[:, :LANES] / o[:, LANES:]


def _seg_sum(x, lo):
    s0 = jnp.sum(jnp.where(lo, x, 0.0), axis=1, keepdims=True)
    s1 = jnp.sum(jnp.where(lo, 0.0, x), axis=1, keepdims=True)
    return jnp.where(lo, s0, s1)


def _mixer_a_kernel(q_ref, k_ref, v_ref, g_ref, bias_ref, o_ref, *, n_blk):
    k_bf = k_ref[0].astype(BF16)
    v_ext = _with_ones(v_ref[0].astype(BF16))
    lo = _lane_lo(TQ)
    rpb = TQ // TB

    def body(qi, carry):
        r0 = pl.multiple_of(qi * TQ, TQ)
        q = q_ref[0, pl.ds(r0, TQ), :] * (HEAD ** -0.5 * LOG2E)
        outs = []
        for hh in range(2):
            qm = jnp.where(lo if hh == 0 else jnp.logical_not(lo), q, 0.0).astype(BF16)
            outs.append(_softmax_pv(_dot_nt(qm, k_bf), v_ext,
                                    lambda rb, j, hh=hh: bias_ref[hh, n_blk - 1 - (qi * rpb + rb) + j]))
        o = jnp.where(lo, outs[0], outs[1])
        o_ref[0, pl.ds(r0, TQ), :] = o * _silu(g_ref[0, pl.ds(r0, TQ), :])
        return carry

    lax.fori_loop(0, n_blk * TB // TQ, body, 0)


def _mixer_a(p, bias):
    b, s, _ = p.shape
    n_blk = s // TB
    nd = 2 * n_blk - 1
    n_pairs = A_HEADS // 2

    def col(off):
        return pl.BlockSpec((1, s, LANES), lambda bi, j, off=off: (bi, 0, off // LANES + j))

    return pl.pallas_call(
        functools.partial(_mixer_a_kernel, n_blk=n_blk),
        grid=(b, n_pairs),
        in_specs=[col(A_Q), col(A_K), col(A_V), col(A_G),
                  pl.BlockSpec((2, nd, TB, TB), lambda bi, j: (j, 0, 0, 0))],
        out_specs=pl.BlockSpec((1, s, LANES), lambda bi, j: (bi, 0, j)),
        out_shape=jax.ShapeDtypeStruct((b, s, n_pairs * LANES), F32),
        compiler_params=_params("parallel", "parallel"),
        name="mixer_a",
    )(p, p, p, p, bias)


def _mixer_d_kernel(q_ref, k_ref, v_ref, g_ref, bias_ref, lam_ref, sg_ref, o_ref, *, n_blk, lam_init):
    k_bf = k_ref[0].astype(BF16)
    v_ext = _with_ones(v_ref[0].astype(BF16))
    lo = _lane_lo(TQ)
    rpb = TQ // TB
    dl = lam_ref[...]
    lam = (jnp.exp(jnp.sum(dl[0:1] * dl[1:2], axis=1, keepdims=True))
           - jnp.exp(jnp.sum(dl[2:3] * dl[3:4], axis=1, keepdims=True)) + lam_init)
    sg = sg_ref[...] * (1.0 - lam_init)

    def body(qi, carry):
        r0 = pl.multiple_of(qi * TQ, TQ)
        q = q_ref[0, pl.ds(r0, TQ), :] * (HEAD ** -0.5 * LOG2E)
        outs = []
        for hh in range(2):
            qm = jnp.where(lo if hh == 0 else jnp.logical_not(lo), q, 0.0).astype(BF16)
            outs.append(_softmax_pv(_dot_nt(qm, k_bf), v_ext,
                                    lambda rb, j: bias_ref[0, n_blk - 1 - (qi * rpb + rb) + j]))
        o = outs[0] - lam * outs[1]
        o = o * lax.rsqrt(jnp.mean(o * o, axis=1, keepdims=True) + 1e-5) * sg
        o_ref[0, pl.ds(r0, TQ), :] = o * _silu(g_ref[0, pl.ds(r0, TQ), :])
        return carry

    lax.fori_loop(0, n_blk * TB // TQ, body, 0)


def _mixer_d(p, bias, d_lambda, subln_g, layer_idx):
    b, s, _ = p.shape
    n_blk = s // TB
    nd = 2 * n_blk - 1
    lam_init = 0.8 - 0.6 * math.exp(-0.3 * layer_idx)

    def col(off):
        return pl.BlockSpec((1, s, LANES), lambda bi, j, off=off: (bi, 0, off // LANES + j))

    return pl.pallas_call(
        functools.partial(_mixer_d_kernel, n_blk=n_blk, lam_init=lam_init),
        grid=(b, D_HEADS),
        in_specs=[col(D_Q), col(D_K), col(D_V), col(D_G),
                  pl.BlockSpec((1, nd, TB, TB), lambda bi, j: (A_HEADS + j, 0, 0, 0)),
                  pl.BlockSpec((4, HEAD), lambda bi, j: (0, 0)),
                  pl.BlockSpec((1, LANES), lambda bi, j: (0, 0))],
        out_specs=pl.BlockSpec((1, s, LANES), lambda bi, j: (bi, 0, j)),
        out_shape=jax.ShapeDtypeStruct((b, s, D_HEADS * LANES), F32),
        compiler_params=_params("parallel", "parallel"),
        name="mixer_d",
    )(p, p, p, p, bias, d_lambda, subln_g.reshape(1, LANES))


def _rope_tables(s):
    t = np.arange(s)
    row, colp = t // GRID_W, t % GRID_W
    qtr = HEAD // 4
    freqs = ROPE_THETA ** (-(jnp.arange(qtr, dtype=F32) / qtr))
    lane = np.arange(LANES) % HEAD
    use_col = (lane // (HEAD // 2)) == 1
    second = (lane % (HEAD // 2)) >= qtr
    pos = jnp.where(use_col[None, :], jnp.asarray(colp, F32)[:, None], jnp.asarray(row, F32)[:, None])
    ang = pos * freqs[lane % qtr][None, :]
    cos = jnp.cos(ang)
    sin = jnp.where(second[None, :], jnp.sin(ang), -jnp.sin(ang))
    return cos, sin


def _norm_rope(x, gain, cos, sin, lo, first):
    ms = _seg_sum(x * x, lo) * (1.0 / HEAD)
    x = x * lax.rsqrt(ms + 1e-6) * gain
    qtr = HEAD // 4
    partner = jnp.where(first, pltpu.roll(x, LANES - qtr, axis=1), pltpu.roll(x, qtr, axis=1))
    return x * cos + partner * sin


def _mixer_c_kernel(q_ref, k_ref, v_ref, g_ref, cos_ref, sin_ref, qg_ref, kg_ref, o_ref, *, n_blk, s):
    pair = pl.program_id(1)
    grp = pair // 2
    lo_s = _lane_lo(s)
    lane_s = lax.broadcasted_iota(jnp.int32, (s, LANES), 1)
    first_s = (lane_s % (HEAD // 2)) < (HEAD // 4)
    k = _norm_rope(k_ref[0], kg_ref[...], cos_ref[...], sin_ref[...], lo_s, first_s)
    v = v_ref[0]
    k_sw = pltpu.roll(k, HEAD, axis=1)
    v_sw = pltpu.roll(v, HEAD, axis=1)
    keep = jnp.logical_xor(lo_s, grp == 1)
    k_bf = jnp.where(keep, k, k_sw).astype(BF16)
    v_ext = _with_ones(jnp.where(keep, v, v_sw).astype(BF16))
    lo = _lane_lo(TQ)
    lane = lax.broadcasted_iota(jnp.int32, (TQ, LANES), 1)
    first = (lane % (HEAD // 2)) < (HEAD // 4)

    def body(qi, carry):
        r0 = pl.multiple_of(qi * TQ, TQ)
        q = _norm_rope(q_ref[0, pl.ds(r0, TQ), :], qg_ref[...], cos_ref[pl.ds(r0, TQ), :],
                       sin_ref[pl.ds(r0, TQ), :], lo, first) * (HEAD ** -0.5 * LOG2E)
        outs = []
        for hh in range(2):
            qm = jnp.where(lo if hh == 0 else jnp.logical_not(lo), q, 0.0).astype(BF16)
            outs.append(_softmax_pv(_dot_nt(qm, k_bf), v_ext))
        o = jnp.where(lo, outs[0], outs[1])
        o_ref[0, pl.ds(r0, TQ), :] = o * _silu(g_ref[0, pl.ds(r0, TQ), :])
        return carry

    lax.fori_loop(0, n_blk, body, 0)


def _mixer_c(p, cos, sin, qn_g, kn_g):
    b, s, _ = p.shape
    n_blk = s // TQ
    n_pairs = 4

    def col(off):
        return pl.BlockSpec((1, s, LANES), lambda bi, j, off=off: (bi, 0, off // LANES + j))

    def fixed(off):
        return pl.BlockSpec((1, s, LANES), lambda bi, j, off=off: (bi, 0, off // LANES))

    tab = pl.BlockSpec((s, LANES), lambda bi, j: (0, 0))
    gain = pl.BlockSpec((1, LANES), lambda bi, j: (0, 0))
    return pl.pallas_call(
        functools.partial(_mixer_c_kernel, n_blk=n_blk, s=s),
        grid=(b, n_pairs),
        in_specs=[col(C_Q), fixed(C_K), fixed(C_V), col(C_G), tab, tab, gain, gain],
        out_specs=pl.BlockSpec((1, s, LANES), lambda bi, j: (bi, 0, j)),
        out_shape=jax.ShapeDtypeStruct((b, s, n_pairs * LANES), F32),
        compiler_params=_params("parallel", "parallel"),
        name="mixer_c",
    )(p, p, p, p, cos, sin, jnp.tile(qn_g, 2).reshape(1, LANES), jnp.tile(kn_g, 2).reshape(1, LANES))


def _mixer_m_kernel(q_ref, g_ref, k_ref, v_ref, o_ref, *, n_blk):
    k_bf = k_ref[0].astype(BF16)
    v_ext = _with_ones(v_ref[0].astype(BF16))
    lo = _lane_lo(TQ)

    def body(qi, carry):
        r0 = pl.multiple_of(qi * TQ, TQ)
        q = q_ref[0, pl.ds(r0, TQ), :] * (HEAD ** -0.5 * LOG2E)
        outs = []
        for hh in range(2):
            qm = jnp.where(lo if hh == 0 else jnp.logical_not(lo), q, 0.0).astype(BF16)
            outs.append(_softmax_pv(_dot_nt(qm, k_bf), v_ext))
        o = jnp.where(lo, outs[0], outs[1])
        o_ref[0, pl.ds(r0, TQ), :] = o * _silu(g_ref[0, pl.ds(r0, TQ), :])
        return carry

    lax.fori_loop(0, n_blk, body, 0)


def _mixer_m(p, kv):
    b, s, _ = p.shape
    n_mem = kv.shape[1]
    n_blk = s // TQ
    n_pairs = 2

    def col(off):
        return pl.BlockSpec((1, s, LANES), lambda bi, j, off=off: (bi, 0, off // LANES + j))

    return pl.pallas_call(
        functools.partial(_mixer_m_kernel, n_blk=n_blk),
        grid=(b, n_pairs),
        in_specs=[col(M_Q), col(M_G),
                  pl.BlockSpec((1, n_mem, LANES), lambda bi, j: (bi, 0, j)),
                  pl.BlockSpec((1, n_mem, LANES), lambda bi, j: (bi, 0, n_pairs + j))],
        out_specs=pl.BlockSpec((1, s, LANES), lambda bi, j: (bi, 0, j)),
        out_shape=jax.ShapeDtypeStruct((b, s, n_pairs * LANES), F32),
        compiler_params=_params("parallel", "parallel"),
        name="mixer_m",
    )(p, p, kv, kv)


def _split3_dot(tri_bf, x):
    h1 = x.astype(BF16)
    r1 = x - h1.astype(F32)
    h2 = r1.astype(BF16)
    h3 = (r1 - h2.astype(F32)).astype(BF16)
    return _dot(tri_bf, h1) + _dot(tri_bf, h2) + _dot(tri_bf, h3)


def _rwkv_kernel(r_ref, k_ref, v_ref, wa_ref, mu_ref, w0_ref, wup_ref, a0_ref, aup_ref,
                 kk_ref, ka_ref, rk_ref, yf_ref, yb_ref, bvf_ref, bvb_ref, state_ref, *, nb, s, nch):
    C = CHUNK
    R = nch * C
    c = pl.program_id(1)

    @pl.when(c == 0)
    def _():
        state_ref[...] = jnp.zeros_like(state_ref)

    n_pairs = r_ref.shape[2] // LANES
    width = r_ref.shape[2]
    rr = lax.broadcasted_iota(jnp.int32, (LANES, LANES), 0)
    cc = lax.broadcasted_iota(jnp.int32, (LANES, LANES), 1)
    same_head = (rr // C) == (cc // C)
    eye = rr == cc
    tr = lax.broadcasted_iota(jnp.int32, (R, R), 0)
    tc = lax.broadcasted_iota(jnp.int32, (R, R), 1)
    same_chunk = (tr // C) == (tc // C)
    lo = _lane_lo(C)
    hi = jnp.logical_not(lo)
    lo_r = _lane_lo(R)

    def stack(x):
        return jnp.concatenate([jnp.where(lo, x, 0.0), jnp.where(hi, x, 0.0)], axis=0)

    units = []
    for e in range(2):
        rev = e == 1
        blk = (nb - 1 - c) if rev else c
        r0 = pl.multiple_of(blk * R, R)
        bv_ref = bvb_ref if rev else bvf_ref
        before = (rr < cc) if rev else (rr > cc)
        strict = jnp.logical_and(same_head, before)
        incl = jnp.logical_and(same_head, jnp.logical_or(before, eye))
        tri_bf = jnp.where(jnp.logical_and(same_chunk, (tr <= tc) if rev else (tr >= tc)), 1.0, 0.0).astype(BF16)
        last = 0 if rev else C - 1
        up, dn = (cc, rr) if rev else (rr, cc)
        level_masks = []
        m = 1
        while m < C:
            same_blk = ((rr ^ cc) >> (int(math.log2(m)) + 1)) == 0
            level_masks.append(jnp.logical_and(same_blk, jnp.logical_and((up & m) != 0, (dn & m) == 0)))
            m *= 2

        def shifted(ref, col0, w):
            x = ref[0, pl.ds(r0, R), :]
            prev_row = ref[0, pl.ds(jnp.maximum(r0 - 1, 0), 1), :]
            prev_row = jnp.where(blk == 0, 0.0, prev_row)
            next_row = ref[0, pl.ds(jnp.minimum(r0 + R, s - 1), 1), :]
            next_row = jnp.where(blk == nb - 1, 0.0, next_row)
            rowi = lax.broadcasted_iota(jnp.int32, (R, w), 0)
            prev = jnp.where(rowi == 0, prev_row, pltpu.roll(x, 1, axis=0))
            nxt = jnp.where(rowi == R - 1, next_row, pltpu.roll(x, R - 1, axis=0))
            mu0 = mu_ref[0:1, col0:col0 + w]
            mu1 = mu_ref[1:2, col0:col0 + w]
            return x + mu0 * (prev - x) + mu1 * (nxt - x)

        rs = shifted(r_ref, 0, width)
        ks = shifted(k_ref, width, width)
        vs = shifted(v_ref, 2 * width, width)
        wa = shifted(wa_ref, 3 * width, 2 * LANES)
        sel = lo_r if e == 0 else jnp.logical_not(lo_r)
        wd = jnp.where(sel, jnp.tanh(wa[:, :LANES]), 0.0).astype(BF16)
        ad = jnp.where(sel, wa[:, LANES:], 0.0).astype(BF16)
        zw = w0_ref[e:e + 1, :] + _dot(wd, wup_ref[...])
        w_log = -(jnp.maximum(-zw, 0.0) + jnp.log(1.0 + jnp.exp(-jnp.abs(zw)))) - 0.5
        lw = -jnp.exp(w_log)
        za = a0_ref[e:e + 1, :] + _dot(ad, aup_ref[...])
        a_sig = 1.0 / (1.0 + jnp.exp(-za))
        kk = ks * kk_ref[...]
        kk2 = kk * kk
        nrm2 = jnp.concatenate([_seg_sum(kk2[:, i * LANES:(i + 1) * LANES], lo_r) for i in range(n_pairs)], axis=1)
        kkn = kk / jnp.maximum(jnp.sqrt(nrm2), 1e-12)
        ke = ks * (1.0 + (a_sig - 1.0) * ka_ref[...])
        be = kkn * a_sig
        rkr = rs * ke * rk_ref[...]
        bonus = jnp.concatenate([_seg_sum(rkr[:, i * LANES:(i + 1) * LANES], lo_r) for i in range(n_pairs)], axis=1)
        bv_ref[0] = bonus * vs

        l_incl = _split3_dot(tri_bf, lw)
        l_tot = jnp.concatenate(
            [jnp.broadcast_to(l_incl[j * C + last:j * C + last + 1, :], (C, width)) for j in range(nch)], axis=0)
        w_inv = jnp.exp(-l_incl)
        w_end = jnp.exp(l_tot - l_incl)
        at = -kkn * jnp.exp(l_incl - lw)
        rt = rs * jnp.exp(l_incl)
        bt = be * w_inv
        kt = ke * w_inv
        bh = be * w_end
        kh = ke * w_end
        d_end = jnp.exp(l_tot)
        for j in range(nch):
            rows = slice(j * C, (j + 1) * C)
            for pr in range(n_pairs):
                sl = slice(pr * LANES, (pr + 1) * LANES)
                units.append(dict(
                    e=e, j=j, pr=pr, strict=strict, incl=incl, levels=level_masks,
                    at=at[rows, sl], rt=rt[rows, sl], bt=bt[rows, sl], kt=kt[rows, sl],
                    bh=bh[rows, sl], kh=kh[rows, sl], v=vs[rows, sl], d_end=d_end[j * C:j * C + 1, sl]))

    for u in units:
        u["at_st"] = stack(u["at"]).astype(BF16)
        u["rt_st"] = stack(u["rt"])
        u["v_bf"] = stack(u["v"]).astype(BF16)
        lhs = jnp.concatenate([u["at_st"], u["rt_st"].astype(BF16)], axis=0)
        bt_p = u["bt"].astype(BF16)
        kt_p = u["kt"].astype(BF16)
        g = _dot_nt(lhs, jnp.concatenate([bt_p, bt_p, kt_p, kt_p], axis=0))
        u["a_ab"] = jnp.where(u["strict"], g[:LANES, :LANES], 0.0)
        u["a_ak"] = jnp.where(u["strict"], g[:LANES, LANES:], 0.0).astype(BF16)
        u["ly"] = jnp.concatenate([jnp.where(u["incl"], g[LANES:, :LANES], 0.0),
                                   jnp.where(u["incl"], g[LANES:, LANES:], 0.0)], axis=1).astype(BF16)
        u["t"] = jnp.where(eye, 1.0, jnp.where(u["levels"][0], u["a_ab"], 0.0))
    for lvl in range(1, len(units[0]["levels"])):
        for u in units:
            u["t_bf"] = u["t"].astype(BF16)
            u["tmp"] = _dot(jnp.where(u["levels"][lvl], u["a_ab"], 0.0).astype(BF16), u["t_bf"]).astype(BF16)
        for u in units:
            u["t"] = u["t"] + _dot(u["t_bf"], u["tmp"])
    for u in units:
        u["akv"] = _dot(u["a_ak"], u["v_bf"]).astype(BF16)
    for u in units:
        pq = _dot(u["t"].astype(BF16), jnp.concatenate([u["at_st"], u["akv"]], axis=1))
        u["ry"] = jnp.concatenate(
            [pq.astype(BF16), jnp.concatenate([jnp.zeros((LANES, LANES), BF16), u["v_bf"]], axis=1)], axis=0)
        ls = jnp.concatenate([stack(u["bh"]), stack(u["kh"])], axis=0)
        u["ls_t"] = ls.T.astype(BF16)
    for u in units:
        yy = _dot(u["ly"], u["ry"])
        zz = _dot(u["ls_t"], u["ry"])
        u["rp"] = (u["rt_st"] + yy[:, :LANES]).astype(BF16)
        u["y0"] = yy[:, LANES:]
        u["mm"] = zz[:, :LANES].astype(BF16)
        u["z"] = zz[:, LANES:]
        d_diag = jnp.where(eye, jnp.broadcast_to(u["d_end"], (LANES, LANES)), 0.0)
        u["d_col"] = jnp.sum(d_diag, axis=1, keepdims=True)

    by_key = {(u["e"], u["j"], u["pr"]): u for u in units}
    for e in range(2):
        y_ref = yb_ref if e == 1 else yf_ref
        order = list(range(nch))[::-1] if e == 1 else list(range(nch))
        ys = {}
        states = [state_ref[e, pr] for pr in range(n_pairs)]
        for j in order:
            for pr in range(n_pairs):
                u = by_key[(e, j, pr)]
                st_bf = states[pr].astype(BF16)
                y_st = _dot(u["rp"], st_bf) + u["y0"]
                states[pr] = u["d_col"] * states[pr] + _dot(u["mm"], st_bf) + u["z"]
                ys[(j, pr)] = y_st[:C] + y_st[C:]
        for pr in range(n_pairs):
            state_ref[e, pr] = states[pr]
        y_ref[0] = jnp.concatenate(
            [jnp.concatenate([ys[(j, pr)] for pr in range(n_pairs)], axis=1) for j in range(nch)], axis=0)


RWKV_CHUNKS_PER_STEP = 2


def _mixer_b_scan(p, mu, w0, w_up, a0, a_up, k_k, k_a, r_k):
    b, s, _ = p.shape
    width = 512
    nch = RWKV_CHUNKS_PER_STEP
    rows = nch * CHUNK
    nb = s // rows
    n_pairs = width // LANES

    def seq(off, w):
        return pl.BlockSpec((1, s, w), lambda bi, c, off=off, w=w: (bi, 0, off // w))

    def full(shape):
        return pl.BlockSpec(shape, lambda bi, c: tuple(0 for _ in shape))

    out_f = pl.BlockSpec((1, rows, width), lambda bi, c: (bi, c, 0))
    out_b = pl.BlockSpec((1, rows, width), lambda bi, c: (bi, nb - 1 - c, 0))
    sds = jax.ShapeDtypeStruct((b, s, width), F32)
    lora = 2 * HEAD
    return pl.pallas_call(
        functools.partial(_rwkv_kernel, nb=nb, s=s, nch=nch),
        grid=(b, nb),
        in_specs=[seq(B_R, width), seq(B_K, width), seq(B_V, width), seq(B_WA, 2 * LANES),
                  full((2, 3 * width + 2 * LANES)), full((2, width)), full((lora, width)),
                  full((2, width)), full((lora, width)), full((1, width)), full((1, width)), full((1, width))],
        out_specs=[out_f, out_b, out_f, out_b],
        out_shape=[sds, sds, sds, sds],
        scratch_shapes=[pltpu.VMEM((2, n_pairs, LANES, LANES), F32)],
        compiler_params=_params("parallel", "arbitrary"),
        name="mixer_b",
    )(p, p, p, p, mu, w0, w_up.reshape(lora, width).astype(BF16), a0, a_up.reshape(lora, width).astype(BF16),
      k_k.reshape(1, width), k_a.reshape(1, width), r_k.reshape(1, width))


def _merge_kernel(h_ref, hb_ref, oa_ref, yf_ref, yb_ref, bvf_ref, bvb_ref, gb0_ref, gb1_ref, oc_ref, od_ref,
                  om_ref, wg_ref, bg_ref, wb_ref, wo_ref, lng_ref, lnb_ref, bg_g_ref, bg_b_ref,
                  hn_ref, hnb_ref, *, alpha):
    d = h_ref.shape[1]
    tm = h_ref.shape[0]
    lo = _lane_lo(tm)
    y = yf_ref[...] + yb_ref[...]
    n_pairs = y.shape[1] // LANES
    gn = []
    for i in range(n_pairs):
        yp = y[:, i * LANES:(i + 1) * LANES]
        mu = _seg_sum(yp, lo) * (1.0 / HEAD)
        dy = yp - mu
        var = _seg_sum(dy * dy, lo) * (1.0 / HEAD)
        gn.append(dy * lax.rsqrt(var + B_GN_EPS))
    gn = jnp.concatenate(gn, axis=1) * bg_g_ref[...] + bg_b_ref[...]
    gb = jnp.concatenate([gb0_ref[...], gb1_ref[...]], axis=1)
    ob = (gn + bvf_ref[...] + bvb_ref[...]) * _silu(gb)

    hb = hb_ref[...]
    branches = (oa_ref[...], ob, oc_ref[...], od_ref[...], om_ref[...])
    acc = None
    row = 0
    for i, o in enumerate(branches):
        wdt = o.shape[1]
        gate = 1.0 / (1.0 + jnp.exp(-(_dot(hb, wg_ref[:, i * d:(i + 1) * d]) + bg_ref[:, i * d:(i + 1) * d])))
        proj = _dot(o.astype(BF16), wb_ref[row:row + wdt, :])
        term = gate * proj
        acc = term if acc is None else acc + term
        row += wdt
    out = _dot(acc.astype(BF16), wo_ref[...])
    z = alpha * h_ref[...] + out
    mu = jnp.mean(z, -1, keepdims=True)
    dz = z - mu
    var = jnp.mean(dz * dz, -1, keepdims=True)
    hn = dz * lax.rsqrt(var + 1e-5) * lng_ref[...] + lnb_ref[...]
    hn_ref[...] = hn
    hnb_ref[...] = hn.astype(BF16)


def _merge(h, hb, o_a, yf, yb, bvf, bvb, p2, o_c, o_d, o_m, w_gate, b_gate, w_branch, w_out, ln_g, ln_b,
           bln_g, bln_b):
    n, d = h.shape
    tm = 256
    alpha = (2 * DEPTH) ** 0.25

    def rows(w):
        return pl.BlockSpec((tm, w), lambda i: (i, 0))

    def full(shape):
        return pl.BlockSpec(shape, lambda i: tuple(0 for _ in shape))

    gb_specs = pl.BlockSpec((tm, 256), lambda i: (i, B_G // 256))
    gb_specs2 = pl.BlockSpec((tm, 256), lambda i: (i, B_G // 256 + 1))
    return pl.pallas_call(
        functools.partial(_merge_kernel, alpha=alpha),
        grid=(n // tm,),
        in_specs=[rows(d), rows(d), rows(512), rows(512), rows(512), rows(512), rows(512),
                  gb_specs, gb_specs2, rows(512), rows(512), rows(256),
                  full(w_gate.shape), full((1, b_gate.shape[0])), full(w_branch.shape), full(w_out.shape),
                  full((1, d)), full((1, d)), full((1, 512)), full((1, 512))],
        out_specs=[rows(d), rows(d)],
        out_shape=[jax.ShapeDtypeStruct((n, d), F32), jax.ShapeDtypeStruct((n, d), BF16)],
        compiler_params=_params("parallel"),
        name="merge",
    )(h, hb, o_a, yf, yb, bvf, bvb, p2, p2, o_c, o_d, o_m,
      w_gate, b_gate.reshape(1, -1), w_branch, w_out, ln_g.reshape(1, d), ln_b.reshape(1, d),
      bln_g.reshape(1, 512), bln_b.reshape(1, 512))


def kernel(x, mem, ln_in_g, ln_in_b, rel_bias, w_in, shift_mu, rwkv_w0, rwkv_w_up, rwkv_a0, rwkv_a_up,
           rwkv_k_k, rwkv_k_a, rwkv_r_k, rwkv_ln_g, rwkv_ln_b, c_qnorm_g, c_knorm_g, d_lambda, d_subln_g,
           w_mem_kv, w_branch, w_gate, b_gate, w_out, ln_g, ln_b):
    b, s, d = x.shape
    n = b * s
    bias = _bias_tiles(rel_bias, s // TB)
    cos, sin = _rope_tables(s)
    mem_bf = mem.reshape(b * mem.shape[1], d).astype(BF16)
    h, hb = _ln_in(x.reshape(n, d), ln_in_g, ln_in_b)
    for l in range(DEPTH):
        p2 = _matmul(hb, w_in[l].astype(BF16), 1024, 1024, "proj_in")
        p = p2.reshape(b, s, IN_COLS)
        kv = _matmul(mem_bf, w_mem_kv[l].astype(BF16), 512, 512, "proj_mem").reshape(b, mem.shape[1], -1)
        o_a = _mixer_a(p, bias)
        yf, yb, bvf, bvb = _mixer_b_scan(p, shift_mu[l], rwkv_w0[l], rwkv_w_up[l], rwkv_a0[l], rwkv_a_up[l],
                                         rwkv_k_k[l], rwkv_k_a[l], rwkv_r_k[l])
        o_c = _mixer_c(p, cos, sin, c_qnorm_g[l], c_knorm_g[l])
        o_d = _mixer_d(p, bias, d_lambda[l], d_subln_g[l], l)
        o_m = _mixer_m(p, kv)
        flat = lambda t: t.reshape(n, t.shape[-1])
        h, hb = _merge(h, hb, flat(o_a), flat(yf), flat(yb), flat(bvf), flat(bvb), p2, flat(o_c), flat(o_d),
                       flat(o_m), w_gate[l].astype(BF16), b_gate[l], w_branch[l].astype(BF16),
                       w_out[l].astype(BF16), ln_g[l], ln_b[l], rwkv_ln_g[l], rwkv_ln_b[l])
    return h.reshape(b, s, d)
```

```python
import functools
import math

import numpy as np
import jax
import jax.numpy as jnp
from jax import lax
from jax.experimental import pallas as pl
from jax.experimental.pallas import tpu as pltpu

F32 = jnp.float32
BF16 = jnp.bfloat16

LANES = 128
HEAD = 64
VMEM_LIMIT = 56 * 1024 * 1024

D_MODEL = 1024
DEPTH = 2
GRID_W = 64
ROPE_THETA = 10000.0
NUM_BUCKETS = 32
REL_MAX_DISTANCE = 1024
A_HEADS = 8
D_HEADS = 4
A_GROUPS = ((128, 1), (512, 4), (2048, 16))
B_GN_EPS = 64e-5
NEG_INF = -1e30

A_Q, A_K, A_V, A_G = 0, 512, 1024, 1536
B_R, B_K, B_V, B_WA, B_G = 2048, 2560, 3072, 3584, 3840
C_Q, C_K, C_V, C_G = 4352, 4864, 4992, 5120
D_Q, D_K, D_V, D_G = 5632, 6144, 6656, 7168
M_Q, M_G = 7680, 7936
IN_COLS = 8192

TB = 128
TQ = 256
LOG2E = math.log2(math.e)
CHUNK = 64


def _dot(a, b):
    return jnp.dot(a, b, preferred_element_type=F32)


def _dot_nt(a, b):
    return lax.dot_general(a, b, (((1,), (1,)), ((), ())), preferred_element_type=F32)


def _silu(g):
    return g / (1.0 + jnp.exp(-g))


def _params(*sem):
    return pltpu.CompilerParams(dimension_semantics=sem, vmem_limit_bytes=VMEM_LIMIT)


def _ln_in_kernel(x_ref, g_ref, b_ref, h_ref, hb_ref):
    x = x_ref[...]
    mu = jnp.mean(x, -1, keepdims=True)
    d = x - mu
    var = jnp.mean(d * d, -1, keepdims=True)
    h = d * lax.rsqrt(var + 1e-5) * g_ref[...] + b_ref[...]
    h_ref[...] = h
    hb_ref[...] = h.astype(BF16)


def _ln_in(x2, g, b):
    n, d = x2.shape
    tm = 512
    return pl.pallas_call(
        _ln_in_kernel,
        grid=(n // tm,),
        in_specs=[pl.BlockSpec((tm, d), lambda i: (i, 0)),
                  pl.BlockSpec((1, d), lambda i: (0, 0)),
                  pl.BlockSpec((1, d), lambda i: (0, 0))],
        out_specs=[pl.BlockSpec((tm, d), lambda i: (i, 0)),
                   pl.BlockSpec((tm, d), lambda i: (i, 0))],
        out_shape=[jax.ShapeDtypeStruct((n, d), F32), jax.ShapeDtypeStruct((n, d), BF16)],
        compiler_params=_params("parallel"),
        name="ln_in",
    )(x2, g.reshape(1, d), b.reshape(1, d))


def _matmul_kernel(a_ref, w_ref, o_ref):
    o_ref[...] = _dot(a_ref[...], w_ref[...])


def _matmul(a, w, tm, tn, name):
    m, k = a.shape
    n = w.shape[1]
    return pl.pallas_call(
        _matmul_kernel,
        grid=(n // tn, m // tm),
        in_specs=[pl.BlockSpec((tm, k), lambda j, i: (i, 0)),
                  pl.BlockSpec((k, tn), lambda j, i: (0, j))],
        out_specs=pl.BlockSpec((tm, tn), lambda j, i: (i, j)),
        out_shape=jax.ShapeDtypeStruct((m, n), F32),
        compiler_params=_params("parallel", "parallel"),
        name=name,
    )(a, w)


def _rel_bucket_np(rel):
    nb = NUM_BUCKETS // 2
    max_exact = nb // 2
    n = np.abs(rel)
    nf = np.maximum(n, 1).astype(np.float32)
    large = max_exact + (np.log(nf / np.float32(max_exact)) / np.float32(math.log(REL_MAX_DISTANCE / max_exact))
                         * np.float32(nb - max_exact)).astype(np.int32)
    large = np.minimum(large, nb - 1)
    return (np.where(rel > 0, nb, 0) + np.where(n < max_exact, n, large)).astype(np.int32)


def _tile_deltas(n_blk):
    d = np.arange(2 * n_blk - 1)[:, None, None] - (n_blk - 1)
    r = np.arange(TB)[None, :, None]
    c = np.arange(TB)[None, None, :]
    return d * TB + c - r


def _dilated_log_multiplicity(delta):
    mult = np.zeros(delta.shape, np.float32)
    for window, dil in A_GROUPS:
        mult += ((delta % dil == 0) & (np.abs(delta) <= window // 2)).astype(np.float32)
    with np.errstate(divide="ignore"):
        return np.where(mult > 0, np.log(np.maximum(mult, 1.0)), NEG_INF).astype(np.float32)


def _bias_kernel(table_ref, bucket_ref, base_ref, o_ref, *, n_a, tile_bucket):
    h = pl.program_id(0)
    for d, const in enumerate(tile_bucket):
        acc = jnp.where(h < n_a, base_ref[d], 0.0)
        if const is not None:
            acc = acc + table_ref[const, h]
        else:
            bucket = bucket_ref[d]
            for b in range(NUM_BUCKETS):
                acc = acc + jnp.where(bucket == b, table_ref[b, h], 0.0)
        o_ref[0, d] = acc * LOG2E


def _bias_tiles(rel_bias, n_blk):
    delta = _tile_deltas(n_blk)
    bucket_np = _rel_bucket_np(delta)
    tile_bucket = tuple(int(t.flat[0]) if (t == t.flat[0]).all() else None for t in bucket_np)
    bucket = jnp.asarray(bucket_np)
    base = jnp.asarray(_dilated_log_multiplicity(delta))
    n_heads = rel_bias.shape[1]
    nd = 2 * n_blk - 1
    return pl.pallas_call(
        functools.partial(_bias_kernel, n_a=A_HEADS, tile_bucket=tile_bucket),
        grid=(n_heads,),
        in_specs=[pl.BlockSpec(memory_space=pltpu.SMEM),
                  pl.BlockSpec((nd, TB, TB), lambda h: (0, 0, 0)),
                  pl.BlockSpec((nd, TB, TB), lambda h: (0, 0, 0))],
        out_specs=pl.BlockSpec((1, nd, TB, TB), lambda h: (h, 0, 0, 0)),
        out_shape=jax.ShapeDtypeStruct((n_heads, nd, TB, TB), F32),
        compiler_params=_params("arbitrary"),
        name="bias_tiles",
    )(rel_bias, bucket, base)


def _lane_lo(rows):
    return lax.broadcasted_iota(jnp.int32, (rows, LANES), 1) < HEAD


def _with_ones(v_bf):
    return jnp.concatenate([v_bf, jnp.ones(v_bf.shape, BF16)], axis=1)


def _store_logits(buf, hh, s, bias_fn=None):
    rows, keys = s.shape
    for rb in range(rows // TB):
        for j in range(keys // LANES):
            c = s[rb * TB:(rb + 1) * TB, j * LANES:(j + 1) * LANES]
            if bias_fn is not None:
                c = c + bias_fn(rb, j)
            buf[hh, rb * TB:(rb + 1) * TB, j * LANES:(j + 1) * LANES] = c


def _softmax_pv(buf, hh, v_ext):
    _, rows, keys = buf.shape
    es = []
    for rb in range(rows // TB):
        def chunk(j):
            return buf[hh, rb * TB:(rb + 1) * TB, j * LANES:(j + 1) * LANES]
        m = chunk(0)
        for j in range(1, keys // LANES):
            m = jnp.maximum(m, chunk(j))
        m = jnp.max(m, axis=1, keepdims=True)
        es.append(jnp.concatenate([jnp.exp2(chunk(j) - m).astype(BF16) for j in range(keys // LANES)], axis=1))
    o = _dot(jnp.concatenate(es, axis=0), v_ext)
    return o[:, :LANES] / o[:, LANES:]


def _attn_loop(n_q, qk_store, finish, s_a, s_b):
    qk_store(s_a, 0)

    def body(i, carry):
        q0 = 2 * i
        qk_store(s_b, q0 + 1)
        finish(s_a, q0)
        qk_store(s_a, q0 + 2)
        finish(s_b, q0 + 1)
        return carry

    lax.fori_loop(0, n_q // 2 - 1, body, 0)
    qk_store(s_b, n_q - 1)
    finish(s_a, n_q - 2)
    finish(s_b, n_q - 1)


def _logit_bufs(keys):
    return [pltpu.VMEM((2, TQ, keys), F32), pltpu.VMEM((2, TQ, keys), F32)]


def _seg_sum(x, lo):
    s0 = jnp.sum(jnp.where(lo, x, 0.0), axis=1, keepdims=True)
    s1 = jnp.sum(jnp.where(lo, 0.0, x), axis=1, keepdims=True)
    return jnp.where(lo, s0, s1)


def _mixer_a_kernel(q_ref, k_ref, v_ref, g_ref, bias_ref, o_ref, s_a, s_b, *, n_blk):
    k_bf = k_ref[0].astype(BF16)
    v_ext = _with_ones(v_ref[0].astype(BF16))
    lo = _lane_lo(TQ)
    rpb = TQ // TB

    def qk_store(buf, qi):
        r0 = pl.multiple_of(qi * TQ, TQ)
        q = q_ref[0, pl.ds(r0, TQ), :] * (HEAD ** -0.5 * LOG2E)
        for hh in range(2):
            qm = jnp.where(lo if hh == 0 else jnp.logical_not(lo), q, 0.0).astype(BF16)
            _store_logits(buf, hh, _dot_nt(qm, k_bf),
                          lambda rb, j, hh=hh: bias_ref[hh, n_blk - 1 - (qi * rpb + rb) + j])

    def finish(buf, qi):
        r0 = pl.multiple_of(qi * TQ, TQ)
        o = jnp.where(lo, _softmax_pv(buf, 0, v_ext), _softmax_pv(buf, 1, v_ext))
        o_ref[0, pl.ds(r0, TQ), :] = o * _silu(g_ref[0, pl.ds(r0, TQ), :])

    _attn_loop(n_blk // rpb, qk_store, finish, s_a, s_b)


def _mixer_a(p, bias):
    b, s, _ = p.shape
    n_blk = s // TB
    nd = 2 * n_blk - 1
    n_pairs = A_HEADS // 2

    def col(off):
        return pl.BlockSpec((1, s, LANES), lambda bi, j, off=off: (bi, 0, off // LANES + j))

    return pl.pallas_call(
        functools.partial(_mixer_a_kernel, n_blk=n_blk),
        grid=(b, n_pairs),
        in_specs=[col(A_Q), col(A_K), col(A_V), col(A_G),
                  pl.BlockSpec((2, nd, TB, TB), lambda bi, j: (j, 0, 0, 0))],
        out_specs=pl.BlockSpec((1, s, LANES), lambda bi, j: (bi, 0, j)),
        out_shape=jax.ShapeDtypeStruct((b, s, n_pairs * LANES), F32),
        scratch_shapes=_logit_bufs(s),
        compiler_params=_params("parallel", "parallel"),
        name="mixer_a",
    )(p, p, p, p, bias)


def _mixer_d_kernel(q_ref, k_ref, v_ref, g_ref, bias_ref, lam_ref, sg_ref, o_ref, s_a, s_b, *, n_blk, lam_init):
    k_bf = k_ref[0].astype(BF16)
    v_ext = _with_ones(v_ref[0].astype(BF16))
    lo = _lane_lo(TQ)
    rpb = TQ // TB
    dl = lam_ref[...]
    lam = (jnp.exp(jnp.sum(dl[0:1] * dl[1:2], axis=1, keepdims=True))
           - jnp.exp(jnp.sum(dl[2:3] * dl[3:4], axis=1, keepdims=True)) + lam_init)
    sg = sg_ref[...] * (1.0 - lam_init)

    def qk_store(buf, qi):
        r0 = pl.multiple_of(qi * TQ, TQ)
        q = q_ref[0, pl.ds(r0, TQ), :] * (HEAD ** -0.5 * LOG2E)
        for hh in range(2):
            qm = jnp.where(lo if hh == 0 else jnp.logical_not(lo), q, 0.0).astype(BF16)
            _store_logits(buf, hh, _dot_nt(qm, k_bf),
                          lambda rb, j: bias_ref[0, n_blk - 1 - (qi * rpb + rb) + j])

    def finish(buf, qi):
        r0 = pl.multiple_of(qi * TQ, TQ)
        o = _softmax_pv(buf, 0, v_ext) - lam * _softmax_pv(buf, 1, v_ext)
        o = o * lax.rsqrt(jnp.mean(o * o, axis=1, keepdims=True) + 1e-5) * sg
        o_ref[0, pl.ds(r0, TQ), :] = o * _silu(g_ref[0, pl.ds(r0, TQ), :])

    _attn_loop(n_blk // rpb, qk_store, finish, s_a, s_b)


def _mixer_d(p, bias, d_lambda, subln_g, layer_idx):
    b, s, _ = p.shape
    n_blk = s // TB
    nd = 2 * n_blk - 1
    lam_init = 0.8 - 0.6 * math.exp(-0.3 * layer_idx)

    def col(off):
        return pl.BlockSpec((1, s, LANES), lambda bi, j, off=off: (bi, 0, off // LANES + j))

    return pl.pallas_call(
        functools.partial(_mixer_d_kernel, n_blk=n_blk, lam_init=lam_init),
        grid=(b, D_HEADS),
        in_specs=[col(D_Q), col(D_K), col(D_V), col(D_G),
                  pl.BlockSpec((1, nd, TB, TB), lambda bi, j: (A_HEADS + j, 0, 0, 0)),
                  pl.BlockSpec((4, HEAD), lambda bi, j: (0, 0)),
                  pl.BlockSpec((1, LANES), lambda bi, j: (0, 0))],
        out_specs=pl.BlockSpec((1, s, LANES), lambda bi, j: (bi, 0, j)),
        out_shape=jax.ShapeDtypeStruct((b, s, D_HEADS * LANES), F32),
        scratch_shapes=_logit_bufs(s),
        compiler_params=_params("parallel", "parallel"),
        name="mixer_d",
    )(p, p, p, p, bias, d_lambda, subln_g.reshape(1, LANES))


def _rope_tables(s):
    t = np.arange(s)
    row, colp = t // GRID_W, t % GRID_W
    qtr = HEAD // 4
    freqs = ROPE_THETA ** (-(jnp.arange(qtr, dtype=F32) / qtr))
    lane = np.arange(LANES) % HEAD
    use_col = (lane // (HEAD // 2)) == 1
    second = (lane % (HEAD // 2)) >= qtr
    pos = jnp.where(use_col[None, :], jnp.asarray(colp, F32)[:, None], jnp.asarray(row, F32)[:, None])
    ang = pos * freqs[lane % qtr][None, :]
    cos = jnp.cos(ang)
    sin = jnp.where(second[None, :], jnp.sin(ang), -jnp.sin(ang))
    return cos, sin


def _norm_rope(x, gain, cos, sin, lo, first):
    ms = _seg_sum(x * x, lo) * (1.0 / HEAD)
    x = x * lax.rsqrt(ms + 1e-6) * gain
    qtr = HEAD // 4
    partner = jnp.where(first, pltpu.roll(x, LANES - qtr, axis=1), pltpu.roll(x, qtr, axis=1))
    return x * cos + partner * sin


def _mixer_c_kernel(q_ref, k_ref, v_ref, g_ref, cos_ref, sin_ref, qg_ref, kg_ref, o_ref, s_a, s_b, krot_ref,
                    *, n_blk, s):
    pair = pl.program_id(1)
    grp = pair // 2
    lo_s = _lane_lo(s)

    @pl.when(pair == 0)
    def _():
        lane_s = lax.broadcasted_iota(jnp.int32, (s, LANES), 1)
        first_s = (lane_s % (HEAD // 2)) < (HEAD // 4)
        krot_ref[...] = _norm_rope(k_ref[0], kg_ref[...], cos_ref[...], sin_ref[...], lo_s, first_s)

    k = krot_ref[...]
    v = v_ref[0]
    k_sw = pltpu.roll(k, HEAD, axis=1)
    v_sw = pltpu.roll(v, HEAD, axis=1)
    keep = jnp.logical_xor(lo_s, grp == 1)
    k_bf = jnp.where(keep, k, k_sw).astype(BF16)
    v_ext = _with_ones(jnp.where(keep, v, v_sw).astype(BF16))
    lo = _lane_lo(TQ)
    lane = lax.broadcasted_iota(jnp.int32, (TQ, LANES), 1)
    first = (lane % (HEAD // 2)) < (HEAD // 4)

    def qk_store(buf, qi):
        r0 = pl.multiple_of(qi * TQ, TQ)
        q = _norm_rope(q_ref[0, pl.ds(r0, TQ), :], qg_ref[...], cos_ref[pl.ds(r0, TQ), :],
                       sin_ref[pl.ds(r0, TQ), :], lo, first) * (HEAD ** -0.5 * LOG2E)
        for hh in range(2):
            qm = jnp.where(lo if hh == 0 else jnp.logical_not(lo), q, 0.0).astype(BF16)
            _store_logits(buf, hh, _dot_nt(qm, k_bf))

    def finish(buf, qi):
        r0 = pl.multiple_of(qi * TQ, TQ)
        o = jnp.where(lo, _softmax_pv(buf, 0, v_ext), _softmax_pv(buf, 1, v_ext))
        o_ref[0, pl.ds(r0, TQ), :] = o * _silu(g_ref[0, pl.ds(r0, TQ), :])

    _attn_loop(n_blk, qk_store, finish, s_a, s_b)


def _mixer_c(p, cos, sin, qn_g, kn_g):
    b, s, _ = p.shape
    n_blk = s // TQ
    n_pairs = 4

    def col(off):
        return pl.BlockSpec((1, s, LANES), lambda bi, j, off=off: (bi, 0, off // LANES + j))

    def fixed(off):
        return pl.BlockSpec((1, s, LANES), lambda bi, j, off=off: (bi, 0, off // LANES))

    tab = pl.BlockSpec((s, LANES), lambda bi, j: (0, 0))
    gain = pl.BlockSpec((1, LANES), lambda bi, j: (0, 0))
    return pl.pallas_call(
        functools.partial(_mixer_c_kernel, n_blk=n_blk, s=s),
        grid=(b, n_pairs),
        in_specs=[col(C_Q), fixed(C_K), fixed(C_V), col(C_G), tab, tab, gain, gain],
        out_specs=pl.BlockSpec((1, s, LANES), lambda bi, j: (bi, 0, j)),
        out_shape=jax.ShapeDtypeStruct((b, s, n_pairs * LANES), F32),
        scratch_shapes=_logit_bufs(s) + [pltpu.VMEM((s, LANES), F32)],
        compiler_params=_params("parallel", "arbitrary"),
        name="mixer_c",
    )(p, p, p, p, cos, sin, jnp.tile(qn_g, 2).reshape(1, LANES), jnp.tile(kn_g, 2).reshape(1, LANES))


def _mixer_m_kernel(q_ref, g_ref, k_ref, v_ref, o_ref, s_a, s_b, *, n_blk):
    k_bf = k_ref[0].astype(BF16)
    v_ext = _with_ones(v_ref[0].astype(BF16))
    lo = _lane_lo(TQ)

    def qk_store(buf, qi):
        r0 = pl.multiple_of(qi * TQ, TQ)
        q = q_ref[0, pl.ds(r0, TQ), :] * (HEAD ** -0.5 * LOG2E)
        for hh in range(2):
            qm = jnp.where(lo if hh == 0 else jnp.logical_not(lo), q, 0.0).astype(BF16)
            _store_logits(buf, hh, _dot_nt(qm, k_bf))

    def finish(buf, qi):
        r0 = pl.multiple_of(qi * TQ, TQ)
        o = jnp.where(lo, _softmax_pv(buf, 0, v_ext), _softmax_pv(buf, 1, v_ext))
        o_ref[0, pl.ds(r0, TQ), :] = o * _silu(g_ref[0, pl.ds(r0, TQ), :])

    _attn_loop(n_blk, qk_store, finish, s_a, s_b)


def _mixer_m(p, kv):
    b, s, _ = p.shape
    n_mem = kv.shape[1]
    n_blk = s // TQ
    n_pairs = 2

    def col(off):
        return pl.BlockSpec((1, s, LANES), lambda bi, j, off=off: (bi, 0, off // LANES + j))

    return pl.pallas_call(
        functools.partial(_mixer_m_kernel, n_blk=n_blk),
        grid=(b, n_pairs),
        in_specs=[col(M_Q), col(M_G),
                  pl.BlockSpec((1, n_mem, LANES), lambda bi, j: (bi, 0, j)),
                  pl.BlockSpec((1, n_mem, LANES), lambda bi, j: (bi, 0, n_pairs + j))],
        out_specs=pl.BlockSpec((1, s, LANES), lambda bi, j: (bi, 0, j)),
        out_shape=jax.ShapeDtypeStruct((b, s, n_pairs * LANES), F32),
        scratch_shapes=_logit_bufs(n_mem),
        compiler_params=_params("parallel", "parallel"),
        name="mixer_m",
    )(p, p, kv, kv)


def _split3_dot(tri_bf, x):
    h1 = x.astype(BF16)
    r1 = x - h1.astype(F32)
    h2 = r1.astype(BF16)
    h3 = (r1 - h2.astype(F32)).astype(BF16)
    return _dot(tri_bf, h1) + _dot(tri_bf, h2) + _dot(tri_bf, h3)


def _rwkv_kernel(r_ref, k_ref, v_ref, wa_ref, mu_ref, w0_ref, wup_ref, a0_ref, aup_ref,
                 kk_ref, ka_ref, rk_ref, yf_ref, yb_ref, bvf_ref, bvb_ref, state_ref, *, nb, s, nch):
    C = CHUNK
    R = nch * C
    c = pl.program_id(1)

    @pl.when(c == 0)
    def _():
        state_ref[...] = jnp.zeros_like(state_ref)

    n_pairs = r_ref.shape[2] // LANES
    width = r_ref.shape[2]
    rr = lax.broadcasted_iota(jnp.int32, (LANES, LANES), 0)
    cc = lax.broadcasted_iota(jnp.int32, (LANES, LANES), 1)
    same_head = (rr // C) == (cc // C)
    eye = rr == cc
    eye_bf = jnp.where(eye, 1.0, 0.0).astype(BF16)
    tr = lax.broadcasted_iota(jnp.int32, (R, R), 0)
    tc = lax.broadcasted_iota(jnp.int32, (R, R), 1)
    same_chunk = (tr // C) == (tc // C)
    lo = _lane_lo(C)
    hi = jnp.logical_not(lo)
    lo_r = _lane_lo(R)

    def stack(x):
        return jnp.concatenate([jnp.where(lo, x, 0.0), jnp.where(hi, x, 0.0)], axis=0)

    units = []
    for e in range(2):
        rev = e == 1
        blk = (nb - 1 - c) if rev else c
        r0 = pl.multiple_of(blk * R, R)
        bv_ref = bvb_ref if rev else bvf_ref
        before = (rr < cc) if rev else (rr > cc)
        strict = jnp.logical_and(same_head, before)
        incl = jnp.logical_and(same_head, jnp.logical_or(before, eye))
        tri_bf = jnp.where(jnp.logical_and(same_chunk, (tr <= tc) if rev else (tr >= tc)), 1.0, 0.0).astype(BF16)
        last = 0 if rev else C - 1
        up, dn = (cc, rr) if rev else (rr, cc)
        level_masks = []
        m = 1
        while m < C:
            same_blk = ((rr ^ cc) >> (int(math.log2(m)) + 1)) == 0
            lvl_mask = jnp.logical_and(same_blk, jnp.logical_and((up & m) != 0, (dn & m) == 0))
            level_masks.append(jnp.where(lvl_mask, 1.0, 0.0).astype(BF16))
            m *= 2

        def shifted(ref, col0, w):
            x = ref[0, pl.ds(r0, R), :]
            prev_row = ref[0, pl.ds(jnp.maximum(r0 - 1, 0), 1), :]
            prev_row = jnp.where(blk == 0, 0.0, prev_row)
            next_row = ref[0, pl.ds(jnp.minimum(r0 + R, s - 1), 1), :]
            next_row = jnp.where(blk == nb - 1, 0.0, next_row)
            rowi = lax.broadcasted_iota(jnp.int32, (R, w), 0)
            prev = jnp.where(rowi == 0, prev_row, pltpu.roll(x, 1, axis=0))
            nxt = jnp.where(rowi == R - 1, next_row, pltpu.roll(x, R - 1, axis=0))
            mu0 = mu_ref[0:1, col0:col0 + w]
            mu1 = mu_ref[1:2, col0:col0 + w]
            return x + mu0 * (prev - x) + mu1 * (nxt - x)

        rs = shifted(r_ref, 0, width)
        ks = shifted(k_ref, width, width)
        vs = shifted(v_ref, 2 * width, width)
        wa = shifted(wa_ref, 3 * width, 2 * LANES)
        sel = lo_r if e == 0 else jnp.logical_not(lo_r)
        wd = jnp.where(sel, jnp.tanh(wa[:, :LANES]), 0.0).astype(BF16)
        ad = jnp.where(sel, wa[:, LANES:], 0.0).astype(BF16)
        zw = w0_ref[e:e + 1, :] + _dot(wd, wup_ref[...])
        w_log = -(jnp.maximum(-zw, 0.0) + jnp.log(1.0 + jnp.exp(-jnp.abs(zw)))) - 0.5
        lw = -jnp.exp(w_log)
        za = a0_ref[e:e + 1, :] + _dot(ad, aup_ref[...])
        a_sig = 1.0 / (1.0 + jnp.exp(-za))
        kk = ks * kk_ref[...]
        kk2 = kk * kk
        nrm2 = jnp.concatenate([_seg_sum(kk2[:, i * LANES:(i + 1) * LANES], lo_r) for i in range(n_pairs)], axis=1)
        kkn = kk / jnp.maximum(jnp.sqrt(nrm2), 1e-12)
        ke = ks * (1.0 + (a_sig - 1.0) * ka_ref[...])
        be = kkn * a_sig
        rkr = rs * ke * rk_ref[...]
        bonus = jnp.concatenate([_seg_sum(rkr[:, i * LANES:(i + 1) * LANES], lo_r) for i in range(n_pairs)], axis=1)
        bv_ref[0] = bonus * vs

        l_incl = _split3_dot(tri_bf, lw)
        l_tot = jnp.concatenate(
            [jnp.broadcast_to(l_incl[j * C + last:j * C + last + 1, :], (C, width)) for j in range(nch)], axis=0)
        w_inv = jnp.exp(-l_incl)
        w_end = jnp.exp(l_tot - l_incl)
        at = -kkn * jnp.exp(l_incl - lw)
        rt = rs * jnp.exp(l_incl)
        bt = be * w_inv
        kt = ke * w_inv
        bh = be * w_end
        kh = ke * w_end
        d_end = jnp.exp(l_tot)
        for j in range(nch):
            rows = slice(j * C, (j + 1) * C)
            for pr in range(n_pairs):
                sl = slice(pr * LANES, (pr + 1) * LANES)
                units.append(dict(
                    e=e, j=j, pr=pr, strict=strict, incl=incl, levels=level_masks,
                    at=at[rows, sl], rt=rt[rows, sl], bt=bt[rows, sl], kt=kt[rows, sl],
                    bh=bh[rows, sl], kh=kh[rows, sl], v=vs[rows, sl], d_end=d_end[j * C:j * C + 1, sl]))

    for u in units:
        u["at_st"] = stack(u["at"]).astype(BF16)
        u["rt_st"] = stack(u["rt"])
        u["v_bf"] = stack(u["v"]).astype(BF16)
        lhs = jnp.concatenate([u["at_st"], u["rt_st"].astype(BF16)], axis=0)
        bt_p = u["bt"].astype(BF16)
        kt_p = u["kt"].astype(BF16)
        g = _dot_nt(lhs, jnp.concatenate([bt_p, bt_p, kt_p, kt_p], axis=0))
        u["a_ab"] = jnp.where(u["strict"], g[:LANES, :LANES], 0.0).astype(BF16)
        u["a_ak"] = jnp.where(u["strict"], g[:LANES, LANES:], 0.0).astype(BF16)
        u["ly"] = jnp.concatenate([jnp.where(u["incl"], g[LANES:, :LANES], 0.0),
                                   jnp.where(u["incl"], g[LANES:, LANES:], 0.0)], axis=1).astype(BF16)
        u["t"] = eye_bf + u["a_ab"] * u["levels"][0]
    for lvl in range(1, len(units[0]["levels"])):
        for u in units:
            u["tmp"] = _dot(u["a_ab"] * u["levels"][lvl], u["t"]).astype(BF16)
        for u in units:
            u["t"] = u["t"] + _dot(u["t"], u["tmp"]).astype(BF16)
    for u in units:
        u["akv"] = _dot(u["a_ak"], u["v_bf"]).astype(BF16)
    for u in units:
        pq = _dot(u["t"], jnp.concatenate([u["at_st"], u["akv"]], axis=1))
        u["ry"] = jnp.concatenate(
            [pq.astype(BF16), jnp.concatenate([jnp.zeros((LANES, LANES), BF16), u["v_bf"]], axis=1)], axis=0)
        ls = jnp.concatenate([stack(u["bh"]), stack(u["kh"])], axis=0)
        u["ls_t"] = ls.T.astype(BF16)
    for u in units:
        yy = _dot(u["ly"], u["ry"])
        zz = _dot(u["ls_t"], u["ry"])
        u["rp"] = (u["rt_st"] + yy[:, :LANES]).astype(BF16)
        u["y0"] = yy[:, LANES:]
        u["mm"] = zz[:, :LANES].astype(BF16)
        u["z"] = zz[:, LANES:]
        d_diag = jnp.where(eye, jnp.broadcast_to(u["d_end"], (LANES, LANES)), 0.0)
        u["d_col"] = jnp.sum(d_diag, axis=1, keepdims=True)

    by_key = {(u["e"], u["j"], u["pr"]): u for u in units}
    for e in range(2):
        y_ref = yb_ref if e == 1 else yf_ref
        order = list(range(nch))[::-1] if e == 1 else list(range(nch))
        ys = {}
        states = [state_ref[e, pr] for pr in range(n_pairs)]
        for j in order:
            for pr in range(n_pairs):
                u = by_key[(e, j, pr)]
                st_bf = states[pr].astype(BF16)
                y_st = _dot(u["rp"], st_bf) + u["y0"]
                states[pr] = u["d_col"] * states[pr] + _dot(u["mm"], st_bf) + u["z"]
                ys[(j, pr)] = y_st[:C] + y_st[C:]
        for pr in range(n_pairs):
            state_ref[e, pr] = states[pr]
        y_ref[0] = jnp.concatenate(
            [jnp.concatenate([ys[(j, pr)] for pr in range(n_pairs)], axis=1) for j in range(nch)], axis=0)


RWKV_CHUNKS_PER_STEP = 2


def _mixer_b_scan(p, mu, w0, w_up, a0, a_up, k_k, k_a, r_k):
    b, s, _ = p.shape
    width = 512
    nch = RWKV_CHUNKS_PER_STEP
    rows = nch * CHUNK
    nb = s // rows
    n_pairs = width // LANES

    def seq(off, w):
        return pl.BlockSpec((1, s, w), lambda bi, c, off=off, w=w: (bi, 0, off // w))

    def full(shape):
        return pl.BlockSpec(shape, lambda bi, c: tuple(0 for _ in shape))

    out_f = pl.BlockSpec((1, rows, width), lambda bi, c: (bi, c, 0))
    out_b = pl.BlockSpec((1, rows, width), lambda bi, c: (bi, nb - 1 - c, 0))
    sds = jax.ShapeDtypeStruct((b, s, width), F32)
    lora = 2 * HEAD
    return pl.pallas_call(
        functools.partial(_rwkv_kernel, nb=nb, s=s, nch=nch),
        grid=(b, nb),
        in_specs=[seq(B_R, width), seq(B_K, width), seq(B_V, width), seq(B_WA, 2 * LANES),
                  full((2, 3 * width + 2 * LANES)), full((2, width)), full((lora, width)),
                  full((2, width)), full((lora, width)), full((1, width)), full((1, width)), full((1, width))],
        out_specs=[out_f, out_b, out_f, out_b],
        out_shape=[sds, sds, sds, sds],
        scratch_shapes=[pltpu.VMEM((2, n_pairs, LANES, LANES), F32)],
        compiler_params=_params("parallel", "arbitrary"),
        name="mixer_b",
    )(p, p, p, p, mu, w0, w_up.reshape(lora, width).astype(BF16), a0, a_up.reshape(lora, width).astype(BF16),
      k_k.reshape(1, width), k_a.reshape(1, width), r_k.reshape(1, width))


def _merge_kernel(h_ref, hb_ref, oa_ref, yf_ref, yb_ref, bvf_ref, bvb_ref, gb0_ref, gb1_ref, oc_ref, od_ref,
                  om_ref, wg_ref, bg_ref, wb_ref, wo_ref, lng_ref, lnb_ref, bg_g_ref, bg_b_ref,
                  hn_ref, hnb_ref, *, alpha):
    d = h_ref.shape[1]
    tm = h_ref.shape[0]
    lo = _lane_lo(tm)
    y = yf_ref[...] + yb_ref[...]
    n_pairs = y.shape[1] // LANES
    gn = []
    for i in range(n_pairs):
        yp = y[:, i * LANES:(i + 1) * LANES]
        mu = _seg_sum(yp, lo) * (1.0 / HEAD)
        dy = yp - mu
        var = _seg_sum(dy * dy, lo) * (1.0 / HEAD)
        gn.append(dy * lax.rsqrt(var + B_GN_EPS))
    gn = jnp.concatenate(gn, axis=1) * bg_g_ref[...] + bg_b_ref[...]
    gb = jnp.concatenate([gb0_ref[...], gb1_ref[...]], axis=1)
    ob = (gn + bvf_ref[...] + bvb_ref[...]) * _silu(gb)

    hb = hb_ref[...]
    branches = (oa_ref[...], ob, oc_ref[...], od_ref[...], om_ref[...])
    acc = None
    row = 0
    for i, o in enumerate(branches):
        wdt = o.shape[1]
        gate = 1.0 / (1.0 + jnp.exp(-(_dot(hb, wg_ref[:, i * d:(i + 1) * d]) + bg_ref[:, i * d:(i + 1) * d])))
        proj = _dot(o.astype(BF16), wb_ref[row:row + wdt, :])
        term = gate * proj
        acc = term if acc is None else acc + term
        row += wdt
    out = _dot(acc.astype(BF16), wo_ref[...])
    z = alpha * h_ref[...] + out
    mu = jnp.mean(z, -1, keepdims=True)
    dz = z - mu
    var = jnp.mean(dz * dz, -1, keepdims=True)
    hn = dz * lax.rsqrt(var + 1e-5) * lng_ref[...] + lnb_ref[...]
    hn_ref[...] = hn
    hnb_ref[...] = hn.astype(BF16)


def _merge(h, hb, o_a, yf, yb, bvf, bvb, p2, o_c, o_d, o_m, w_gate, b_gate, w_branch, w_out, ln_g, ln_b,
           bln_g, bln_b):
    n, d = h.shape
    tm = 256
    alpha = (2 * DEPTH) ** 0.25

    def rows(w):
        return pl.BlockSpec((tm, w), lambda i: (i, 0))

    def full(shape):
        return pl.BlockSpec(shape, lambda i: tuple(0 for _ in shape))

    gb_specs = pl.BlockSpec((tm, 256), lambda i: (i, B_G // 256))
    gb_specs2 = pl.BlockSpec((tm, 256), lambda i: (i, B_G // 256 + 1))
    return pl.pallas_call(
        functools.partial(_merge_kernel, alpha=alpha),
        grid=(n // tm,),
        in_specs=[rows(d), rows(d), rows(512), rows(512), rows(512), rows(512), rows(512),
                  gb_specs, gb_specs2, rows(512), rows(512), rows(256),
                  full(w_gate.shape), full((1, b_gate.shape[0])), full(w_branch.shape), full(w_out.shape),
                  full((1, d)), full((1, d)), full((1, 512)), full((1, 512))],
        out_specs=[rows(d), rows(d)],
        out_shape=[jax.ShapeDtypeStruct((n, d), F32), jax.ShapeDtypeStruct((n, d), BF16)],
        compiler_params=_params("parallel"),
        name="merge",
    )(h, hb, o_a, yf, yb, bvf, bvb, p2, p2, o_c, o_d, o_m,
      w_gate, b_gate.reshape(1, -1), w_branch, w_out, ln_g.reshape(1, d), ln_b.reshape(1, d),
      bln_g.reshape(1, 512), bln_b.reshape(1, 512))


def kernel(x, mem, ln_in_g, ln_in_b, rel_bias, w_in, shift_mu, rwkv_w0, rwkv_w_up, rwkv_a0, rwkv_a_up,
           rwkv_k_k, rwkv_k_a, rwkv_r_k, rwkv_ln_g, rwkv_ln_b, c_qnorm_g, c_knorm_g, d_lambda, d_subln_g,
           w_mem_kv, w_branch, w_gate, b_gate, w_out, ln_g, ln_b):
    b, s, d = x.shape
    n = b * s
    bias = _bias_tiles(rel_bias, s // TB)
    cos, sin = _rope_tables(s)
    mem_bf = mem.reshape(b * mem.shape[1], d).astype(BF16)
    h, hb = _ln_in(x.reshape(n, d), ln_in_g, ln_in_b)
    w_in, w_mem_kv, w_gate, w_branch, w_out = (w.astype(BF16) for w in (w_in, w_mem_kv, w_gate, w_branch, w_out))
    for l in range(DEPTH):
        p2 = _matmul(hb, w_in[l], 1024, 2048, "proj_in")
        p = p2.reshape(b, s, IN_COLS)
        kv = _matmul(mem_bf, w_mem_kv[l], 512, 512, "proj_mem").reshape(b, mem.shape[1], -1)
        o_a = _mixer_a(p, bias)
        yf, yb, bvf, bvb = _mixer_b_scan(p, shift_mu[l], rwkv_w0[l], rwkv_w_up[l], rwkv_a0[l], rwkv_a_up[l],
                                         rwkv_k_k[l], rwkv_k_a[l], rwkv_r_k[l])
        o_c = _mixer_c(p, cos, sin, c_qnorm_g[l], c_knorm_g[l])
        o_d = _mixer_d(p, bias, d_lambda[l], d_subln_g[l], l)
        o_m = _mixer_m(p, kv)
        flat = lambda t: t.reshape(n, t.shape[-1])
        h, hb = _merge(h, hb, flat(o_a), flat(yf), flat(yb), flat(bvf), flat(bvb), p2, flat(o_c), flat(o_d),
                       flat(o_m), w_gate[l], b_gate[l], w_branch[l], w_out[l], ln_g[l], ln_b[l],
                       rwkv_ln_g[l], rwkv_ln_b[l])
    return h.reshape(b, s, d)
```

```python
import functools
import math

import numpy as np
import jax
import jax.numpy as jnp
from jax import lax
from jax.experimental import pallas as pl
from jax.experimental.pallas import tpu as pltpu

F32 = jnp.float32
BF16 = jnp.bfloat16

LANES = 128
HEAD = 64
VMEM_LIMIT = 56 * 1024 * 1024

D_MODEL = 1024
DEPTH = 2
GRID_W = 64
ROPE_THETA = 10000.0
NUM_BUCKETS = 32
REL_MAX_DISTANCE = 1024
A_HEADS = 8
D_HEADS = 4
A_GROUPS = ((128, 1), (512, 4), (2048, 16))
B_GN_EPS = 64e-5
NEG_INF = -1e30

A_Q, A_K, A_V, A_G = 0, 512, 1024, 1536
B_R, B_K, B_V, B_WA, B_G = 2048, 2560, 3072, 3584, 3840
C_Q, C_K, C_V, C_G = 4352, 4864, 4992, 5120
D_Q, D_K, D_V, D_G = 5632, 6144, 6656, 7168
M_Q, M_G = 7680, 7936
IN_COLS = 8192

TB = 128
TQ = 256
LOG2E = math.log2(math.e)
CHUNK = 64


def _dot(a, b):
    return jnp.dot(a, b, preferred_element_type=F32)


def _dot_nt(a, b):
    return lax.dot_general(a, b, (((1,), (1,)), ((), ())), preferred_element_type=F32)


def _silu(g):
    return g / (1.0 + jnp.exp(-g))


def _params(*sem):
    return pltpu.CompilerParams(dimension_semantics=sem, vmem_limit_bytes=VMEM_LIMIT)


def _ln_in_kernel(x_ref, g_ref, b_ref, h_ref, hb_ref):
    x = x_ref[...]
    mu = jnp.mean(x, -1, keepdims=True)
    d = x - mu
    var = jnp.mean(d * d, -1, keepdims=True)
    h = d * lax.rsqrt(var + 1e-5) * g_ref[...] + b_ref[...]
    h_ref[...] = h
    hb_ref[...] = h.astype(BF16)


def _ln_in(x2, g, b):
    n, d = x2.shape
    tm = 512
    return pl.pallas_call(
        _ln_in_kernel,
        grid=(n // tm,),
        in_specs=[pl.BlockSpec((tm, d), lambda i: (i, 0)),
                  pl.BlockSpec((1, d), lambda i: (0, 0)),
                  pl.BlockSpec((1, d), lambda i: (0, 0))],
        out_specs=[pl.BlockSpec((tm, d), lambda i: (i, 0)),
                   pl.BlockSpec((tm, d), lambda i: (i, 0))],
        out_shape=[jax.ShapeDtypeStruct((n, d), F32), jax.ShapeDtypeStruct((n, d), BF16)],
        compiler_params=_params("parallel"),
        name="ln_in",
    )(x2, g.reshape(1, d), b.reshape(1, d))


def _matmul_kernel(a_ref, w_ref, o_ref):
    o_ref[...] = _dot(a_ref[...], w_ref[...])


def _matmul(a, w, tm, tn, name):
    m, k = a.shape
    n = w.shape[1]
    return pl.pallas_call(
        _matmul_kernel,
        grid=(n // tn, m // tm),
        in_specs=[pl.BlockSpec((tm, k), lambda j, i: (i, 0)),
                  pl.BlockSpec((k, tn), lambda j, i: (0, j))],
        out_specs=pl.BlockSpec((tm, tn), lambda j, i: (i, j)),
        out_shape=jax.ShapeDtypeStruct((m, n), F32),
        compiler_params=_params("parallel", "parallel"),
        name=name,
    )(a, w)


def _rel_bucket_np(rel):
    nb = NUM_BUCKETS // 2
    max_exact = nb // 2
    n = np.abs(rel)
    nf = np.maximum(n, 1).astype(np.float32)
    large = max_exact + (np.log(nf / np.float32(max_exact)) / np.float32(math.log(REL_MAX_DISTANCE / max_exact))
                         * np.float32(nb - max_exact)).astype(np.int32)
    large = np.minimum(large, nb - 1)
    return (np.where(rel > 0, nb, 0) + np.where(n < max_exact, n, large)).astype(np.int32)


def _tile_deltas(n_blk):
    d = np.arange(2 * n_blk - 1)[:, None, None] - (n_blk - 1)
    r = np.arange(TB)[None, :, None]
    c = np.arange(TB)[None, None, :]
    return d * TB + c - r


def _dilated_log_multiplicity(delta):
    mult = np.zeros(delta.shape, np.float32)
    for window, dil in A_GROUPS:
        mult += ((delta % dil == 0) & (np.abs(delta) <= window // 2)).astype(np.float32)
    with np.errstate(divide="ignore"):
        return np.where(mult > 0, np.log(np.maximum(mult, 1.0)), NEG_INF).astype(np.float32)


def _bias_kernel(table_ref, bucket_ref, base_ref, o_ref, *, n_a, tile_bucket):
    h = pl.program_id(0)
    for d, const in enumerate(tile_bucket):
        acc = jnp.where(h < n_a, base_ref[d], 0.0)
        if const is not None:
            acc = acc + table_ref[const, h]
        else:
            bucket = bucket_ref[d]
            for b in range(NUM_BUCKETS):
                acc = acc + jnp.where(bucket == b, table_ref[b, h], 0.0)
        o_ref[0, d] = acc * LOG2E


def _bias_tiles(rel_bias, n_blk):
    delta = _tile_deltas(n_blk)
    bucket_np = _rel_bucket_np(delta)
    tile_bucket = tuple(int(t.flat[0]) if (t == t.flat[0]).all() else None for t in bucket_np)
    bucket = jnp.asarray(bucket_np)
    base = jnp.asarray(_dilated_log_multiplicity(delta))
    n_heads = rel_bias.shape[1]
    nd = 2 * n_blk - 1
    return pl.pallas_call(
        functools.partial(_bias_kernel, n_a=A_HEADS, tile_bucket=tile_bucket),
        grid=(n_heads,),
        in_specs=[pl.BlockSpec(memory_space=pltpu.SMEM),
                  pl.BlockSpec((nd, TB, TB), lambda h: (0, 0, 0)),
                  pl.BlockSpec((nd, TB, TB), lambda h: (0, 0, 0))],
        out_specs=pl.BlockSpec((1, nd, TB, TB), lambda h: (h, 0, 0, 0)),
        out_shape=jax.ShapeDtypeStruct((n_heads, nd, TB, TB), F32),
        compiler_params=_params("arbitrary"),
        name="bias_tiles",
    )(rel_bias, bucket, base)


def _lane_lo(rows):
    return lax.broadcasted_iota(jnp.int32, (rows, LANES), 1) < HEAD


def _with_ones(v_bf):
    return jnp.concatenate([v_bf, jnp.ones(v_bf.shape, BF16)], axis=1)


def _store_logits(buf, hh, s, bias_fn=None):
    rows, keys = s.shape
    for rb in range(rows // TB):
        for j in range(keys // LANES):
            c = s[rb * TB:(rb + 1) * TB, j * LANES:(j + 1) * LANES]
            if bias_fn is not None:
                c = c + bias_fn(rb, j)
            buf[hh, rb * TB:(rb + 1) * TB, j * LANES:(j + 1) * LANES] = c


def _softmax_pv(buf, hh, v_ext):
    _, rows, keys = buf.shape
    es = []
    for rb in range(rows // TB):
        def chunk(j):
            return buf[hh, rb * TB:(rb + 1) * TB, j * LANES:(j + 1) * LANES]
        m = chunk(0)
        for j in range(1, keys // LANES):
            m = jnp.maximum(m, chunk(j))
        m = jnp.max(m, axis=1, keepdims=True)
        es.append(jnp.concatenate([jnp.exp2(chunk(j) - m).astype(BF16) for j in range(keys // LANES)], axis=1))
    o = _dot(jnp.concatenate(es, axis=0), v_ext)
    return o[:, :LANES] / o[:, LANES:]


def _attn_loop(n_q, qk_store, finish, s_a, s_b):
    qk_store(s_a, 0)

    def body(i, carry):
        q0 = 2 * i
        qk_store(s_b, q0 + 1)
        finish(s_a, q0)
        qk_store(s_a, q0 + 2)
        finish(s_b, q0 + 1)
        return carry

    lax.fori_loop(0, n_q // 2 - 1, body, 0)
    qk_store(s_b, n_q - 1)
    finish(s_a, n_q - 2)
    finish(s_b, n_q - 1)


def _logit_bufs(keys):
    return [pltpu.VMEM((2, TQ, keys), F32), pltpu.VMEM((2, TQ, keys), F32)]


def _seg_sum(x, lo):
    s0 = jnp.sum(jnp.where(lo, x, 0.0), axis=1, keepdims=True)
    s1 = jnp.sum(jnp.where(lo, 0.0, x), axis=1, keepdims=True)
    return jnp.where(lo, s0, s1)


def _mixer_a_kernel(q_ref, k_ref, v_ref, g_ref, bias_ref, o_ref, s_a, s_b, *, n_blk):
    k_bf = k_ref[0].astype(BF16)
    v_ext = _with_ones(v_ref[0].astype(BF16))
    lo = _lane_lo(TQ)
    rpb = TQ // TB

    def qk_store(buf, qi):
        r0 = pl.multiple_of(qi * TQ, TQ)
        q = q_ref[0, pl.ds(r0, TQ), :] * (HEAD ** -0.5 * LOG2E)
        for hh in range(2):
            qm = jnp.where(lo if hh == 0 else jnp.logical_not(lo), q, 0.0).astype(BF16)
            _store_logits(buf, hh, _dot_nt(qm, k_bf),
                          lambda rb, j, hh=hh: bias_ref[hh, n_blk - 1 - (qi * rpb + rb) + j])

    def finish(buf, qi):
        r0 = pl.multiple_of(qi * TQ, TQ)
        o = jnp.where(lo, _softmax_pv(buf, 0, v_ext), _softmax_pv(buf, 1, v_ext))
        o_ref[0, pl.ds(r0, TQ), :] = o * _silu(g_ref[0, pl.ds(r0, TQ), :])

    _attn_loop(n_blk // rpb, qk_store, finish, s_a, s_b)


def _mixer_a(p, bias):
    b, s, _ = p.shape
    n_blk = s // TB
    nd = 2 * n_blk - 1
    n_pairs = A_HEADS // 2

    def col(off):
        return pl.BlockSpec((1, s, LANES), lambda bi, j, off=off: (bi, 0, off // LANES + j))

    return pl.pallas_call(
        functools.partial(_mixer_a_kernel, n_blk=n_blk),
        grid=(b, n_pairs),
        in_specs=[col(A_Q), col(A_K), col(A_V), col(A_G),
                  pl.BlockSpec((2, nd, TB, TB), lambda bi, j: (j, 0, 0, 0))],
        out_specs=pl.BlockSpec((1, s, LANES), lambda bi, j: (bi, 0, j)),
        out_shape=jax.ShapeDtypeStruct((b, s, n_pairs * LANES), F32),
        scratch_shapes=_logit_bufs(s),
        compiler_params=_params("parallel", "parallel"),
        name="mixer_a",
    )(p, p, p, p, bias)


def _mixer_d_kernel(q_ref, k_ref, v_ref, g_ref, bias_ref, lam_ref, sg_ref, o_ref, s_a, s_b, *, n_blk, lam_init):
    k_bf = k_ref[0].astype(BF16)
    v_ext = _with_ones(v_ref[0].astype(BF16))
    lo = _lane_lo(TQ)
    rpb = TQ // TB
    dl = lam_ref[...]
    lam = (jnp.exp(jnp.sum(dl[0:1] * dl[1:2], axis=1, keepdims=True))
           - jnp.exp(jnp.sum(dl[2:3] * dl[3:4], axis=1, keepdims=True)) + lam_init)
    sg = sg_ref[...] * (1.0 - lam_init)

    def qk_store(buf, qi):
        r0 = pl.multiple_of(qi * TQ, TQ)
        q = q_ref[0, pl.ds(r0, TQ), :] * (HEAD ** -0.5 * LOG2E)
        for hh in range(2):
            qm = jnp.where(lo if hh == 0 else jnp.logical_not(lo), q, 0.0).astype(BF16)
            _store_logits(buf, hh, _dot_nt(qm, k_bf),
                          lambda rb, j: bias_ref[0, n_blk - 1 - (qi * rpb + rb) + j])

    def finish(buf, qi):
        r0 = pl.multiple_of(qi * TQ, TQ)
        o = _softmax_pv(buf, 0, v_ext) - lam * _softmax_pv(buf, 1, v_ext)
        o = o * lax.rsqrt(jnp.mean(o * o, axis=1, keepdims=True) + 1e-5) * sg
        o_ref[0, pl.ds(r0, TQ), :] = o * _silu(g_ref[0, pl.ds(r0, TQ), :])

    _attn_loop(n_blk // rpb, qk_store, finish, s_a, s_b)


def _mixer_d(p, bias, d_lambda, subln_g, layer_idx):
    b, s, _ = p.shape
    n_blk = s // TB
    nd = 2 * n_blk - 1
    lam_init = 0.8 - 0.6 * math.exp(-0.3 * layer_idx)

    def col(off):
        return pl.BlockSpec((1, s, LANES), lambda bi, j, off=off: (bi, 0, off // LANES + j))

    return pl.pallas_call(
        functools.partial(_mixer_d_kernel, n_blk=n_blk, lam_init=lam_init),
        grid=(b, D_HEADS),
        in_specs=[col(D_Q), col(D_K), col(D_V), col(D_G),
                  pl.BlockSpec((1, nd, TB, TB), lambda bi, j: (A_HEADS + j, 0, 0, 0)),
                  pl.BlockSpec((4, HEAD), lambda bi, j: (0, 0)),
                  pl.BlockSpec((1, LANES), lambda bi, j: (0, 0))],
        out_specs=pl.BlockSpec((1, s, LANES), lambda bi, j: (bi, 0, j)),
        out_shape=jax.ShapeDtypeStruct((b, s, D_HEADS * LANES), F32),
        scratch_shapes=_logit_bufs(s),
        compiler_params=_params("parallel", "parallel"),
        name="mixer_d",
    )(p, p, p, p, bias, d_lambda, subln_g.reshape(1, LANES))


def _rope_tables(s):
    t = np.arange(s)
    row, colp = t // GRID_W, t % GRID_W
    qtr = HEAD // 4
    freqs = ROPE_THETA ** (-(jnp.arange(qtr, dtype=F32) / qtr))
    lane = np.arange(LANES) % HEAD
    use_col = (lane // (HEAD // 2)) == 1
    second = (lane % (HEAD // 2)) >= qtr
    pos = jnp.where(use_col[None, :], jnp.asarray(colp, F32)[:, None], jnp.asarray(row, F32)[:, None])
    ang = pos * freqs[lane % qtr][None, :]
    cos = jnp.cos(ang)
    sin = jnp.where(second[None, :], jnp.sin(ang), -jnp.sin(ang))
    return cos, sin


def _norm_rope(x, gain, cos, sin, lo, first):
    ms = _seg_sum(x * x, lo) * (1.0 / HEAD)
    x = x * lax.rsqrt(ms + 1e-6) * gain
    qtr = HEAD // 4
    partner = jnp.where(first, pltpu.roll(x, LANES - qtr, axis=1), pltpu.roll(x, qtr, axis=1))
    return x * cos + partner * sin


def _mixer_c_kernel(q_ref, k_ref, v_ref, g_ref, cos_ref, sin_ref, qg_ref, kg_ref, o_ref, s_a, s_b, krot_ref,
                    *, n_blk, s):
    pair = pl.program_id(1)
    grp = pair // 2
    lo_s = _lane_lo(s)

    @pl.when(pair == 0)
    def _():
        lane_s = lax.broadcasted_iota(jnp.int32, (s, LANES), 1)
        first_s = (lane_s % (HEAD // 2)) < (HEAD // 4)
        krot_ref[...] = _norm_rope(k_ref[0], kg_ref[...], cos_ref[...], sin_ref[...], lo_s, first_s)

    k = krot_ref[...]
    v = v_ref[0]
    k_sw = pltpu.roll(k, HEAD, axis=1)
    v_sw = pltpu.roll(v, HEAD, axis=1)
    keep = jnp.logical_xor(lo_s, grp == 1)
    k_bf = jnp.where(keep, k, k_sw).astype(BF16)
    v_ext = _with_ones(jnp.where(keep, v, v_sw).astype(BF16))
    lo = _lane_lo(TQ)
    lane = lax.broadcasted_iota(jnp.int32, (TQ, LANES), 1)
    first = (lane % (HEAD // 2)) < (HEAD // 4)

    def qk_store(buf, qi):
        r0 = pl.multiple_of(qi * TQ, TQ)
        q = _norm_rope(q_ref[0, pl.ds(r0, TQ), :], qg_ref[...], cos_ref[pl.ds(r0, TQ), :],
                       sin_ref[pl.ds(r0, TQ), :], lo, first) * (HEAD ** -0.5 * LOG2E)
        for hh in range(2):
            qm = jnp.where(lo if hh == 0 else jnp.logical_not(lo), q, 0.0).astype(BF16)
            _store_logits(buf, hh, _dot_nt(qm, k_bf))

    def finish(buf, qi):
        r0 = pl.multiple_of(qi * TQ, TQ)
        o = jnp.where(lo, _softmax_pv(buf, 0, v_ext), _softmax_pv(buf, 1, v_ext))
        o_ref[0, pl.ds(r0, TQ), :] = o * _silu(g_ref[0, pl.ds(r0, TQ), :])

    _attn_loop(n_blk, qk_store, finish, s_a, s_b)


def _mixer_c(p, cos, sin, qn_g, kn_g):
    b, s, _ = p.shape
    n_blk = s // TQ
    n_pairs = 4

    def col(off):
        return pl.BlockSpec((1, s, LANES), lambda bi, j, off=off: (bi, 0, off // LANES + j))

    def fixed(off):
        return pl.BlockSpec((1, s, LANES), lambda bi, j, off=off: (bi, 0, off // LANES))

    tab = pl.BlockSpec((s, LANES), lambda bi, j: (0, 0))
    gain = pl.BlockSpec((1, LANES), lambda bi, j: (0, 0))
    return pl.pallas_call(
        functools.partial(_mixer_c_kernel, n_blk=n_blk, s=s),
        grid=(b, n_pairs),
        in_specs=[col(C_Q), fixed(C_K), fixed(C_V), col(C_G), tab, tab, gain, gain],
        out_specs=pl.BlockSpec((1, s, LANES), lambda bi, j: (bi, 0, j)),
        out_shape=jax.ShapeDtypeStruct((b, s, n_pairs * LANES), F32),
        scratch_shapes=_logit_bufs(s) + [pltpu.VMEM((s, LANES), F32)],
        compiler_params=_params("parallel", "arbitrary"),
        name="mixer_c",
    )(p, p, p, p, cos, sin, jnp.tile(qn_g, 2).reshape(1, LANES), jnp.tile(kn_g, 2).reshape(1, LANES))


def _mixer_m_kernel(q_ref, g_ref, k_ref, v_ref, o_ref, s_a, s_b, *, n_blk):
    k_bf = k_ref[0].astype(BF16)
    v_ext = _with_ones(v_ref[0].astype(BF16))
    lo = _lane_lo(TQ)

    def qk_store(buf, qi):
        r0 = pl.multiple_of(qi * TQ, TQ)
        q = q_ref[0, pl.ds(r0, TQ), :] * (HEAD ** -0.5 * LOG2E)
        for hh in range(2):
            qm = jnp.where(lo if hh == 0 else jnp.logical_not(lo), q, 0.0).astype(BF16)
            _store_logits(buf, hh, _dot_nt(qm, k_bf))

    def finish(buf, qi):
        r0 = pl.multiple_of(qi * TQ, TQ)
        o = jnp.where(lo, _softmax_pv(buf, 0, v_ext), _softmax_pv(buf, 1, v_ext))
        o_ref[0, pl.ds(r0, TQ), :] = o * _silu(g_ref[0, pl.ds(r0, TQ), :])

    _attn_loop(n_blk, qk_store, finish, s_a, s_b)


def _mixer_m(p, kv):
    b, s, _ = p.shape
    n_mem = kv.shape[1]
    n_blk = s // TQ
    n_pairs = 2

    def col(off):
        return pl.BlockSpec((1, s, LANES), lambda bi, j, off=off: (bi, 0, off // LANES + j))

    return pl.pallas_call(
        functools.partial(_mixer_m_kernel, n_blk=n_blk),
        grid=(b, n_pairs),
        in_specs=[col(M_Q), col(M_G),
                  pl.BlockSpec((1, n_mem, LANES), lambda bi, j: (bi, 0, j)),
                  pl.BlockSpec((1, n_mem, LANES), lambda bi, j: (bi, 0, n_pairs + j))],
        out_specs=pl.BlockSpec((1, s, LANES), lambda bi, j: (bi, 0, j)),
        out_shape=jax.ShapeDtypeStruct((b, s, n_pairs * LANES), F32),
        scratch_shapes=_logit_bufs(n_mem),
        compiler_params=_params("parallel", "parallel"),
        name="mixer_m",
    )(p, p, kv, kv)


def _split3_dot(tri_bf, x):
    h1 = x.astype(BF16)
    r1 = x - h1.astype(F32)
    h2 = r1.astype(BF16)
    h3 = (r1 - h2.astype(F32)).astype(BF16)
    return _dot(tri_bf, h1) + _dot(tri_bf, h2) + _dot(tri_bf, h3)


def _rwkv_kernel(r_ref, k_ref, v_ref, wa_ref, mu_ref, w0_ref, wup_ref, a0_ref, aup_ref,
                 kk_ref, ka_ref, rk_ref, yf_ref, yb_ref, bvf_ref, bvb_ref, state_ref, *, nb, s, nch):
    C = CHUNK
    R = nch * C
    c = pl.program_id(1)

    @pl.when(c == 0)
    def _():
        state_ref[...] = jnp.zeros_like(state_ref)

    n_pairs = r_ref.shape[2] // LANES
    width = r_ref.shape[2]
    tt = lax.broadcasted_iota(jnp.int32, (C, LANES), 0)
    ss = lax.broadcasted_iota(jnp.int32, (C, LANES), 1) % C
    eye_bf = jnp.where(tt == ss, 1.0, 0.0).astype(BF16)
    rr = lax.broadcasted_iota(jnp.int32, (LANES, LANES), 0)
    cc = lax.broadcasted_iota(jnp.int32, (LANES, LANES), 1)
    eye = rr == cc
    tr = lax.broadcasted_iota(jnp.int32, (R, R), 0)
    tc = lax.broadcasted_iota(jnp.int32, (R, R), 1)
    same_chunk = (tr // C) == (tc // C)
    lo = _lane_lo(C)
    hi = jnp.logical_not(lo)
    lo_bf = jnp.where(lo, 1.0, 0.0).astype(BF16)
    hi_bf = jnp.where(hi, 1.0, 0.0).astype(BF16)
    lo_r = _lane_lo(R)

    def stack(x):
        return jnp.concatenate([jnp.where(lo, x, 0.0), jnp.where(hi, x, 0.0)], axis=0)

    def stack_bf(x):
        return jnp.concatenate([x * lo_bf, x * hi_bf], axis=0)

    units = []
    for e in range(2):
        rev = e == 1
        blk = (nb - 1 - c) if rev else c
        r0 = pl.multiple_of(blk * R, R)
        bv_ref = bvb_ref if rev else bvf_ref
        strict = (tt < ss) if rev else (tt > ss)
        incl = (tt <= ss) if rev else (tt >= ss)
        tri_bf = jnp.where(jnp.logical_and(same_chunk, (tr <= tc) if rev else (tr >= tc)), 1.0, 0.0).astype(BF16)
        last = 0 if rev else C - 1
        up, dn = (ss, tt) if rev else (tt, ss)
        level_masks = []
        m = 1
        while m < C:
            same_blk = ((tt ^ ss) >> (int(math.log2(m)) + 1)) == 0
            lvl_mask = jnp.logical_and(same_blk, jnp.logical_and((up & m) != 0, (dn & m) == 0))
            level_masks.append(jnp.where(lvl_mask, 1.0, 0.0).astype(BF16))
            m *= 2

        def shifted(ref, col0, w):
            x = ref[0, pl.ds(r0, R), :]
            prev_row = ref[0, pl.ds(jnp.maximum(r0 - 1, 0), 1), :]
            prev_row = jnp.where(blk == 0, 0.0, prev_row)
            next_row = ref[0, pl.ds(jnp.minimum(r0 + R, s - 1), 1), :]
            next_row = jnp.where(blk == nb - 1, 0.0, next_row)
            sub = lax.broadcasted_iota(jnp.int32, (8, w), 0)
            down = pltpu.roll(x, 1, axis=0)
            up_ = pltpu.roll(x, R - 1, axis=0)
            prev = jnp.concatenate([jnp.where(sub == 0, prev_row, down[:8]), down[8:]], axis=0)
            nxt = jnp.concatenate([up_[:R - 8], jnp.where(sub == 7, next_row, up_[R - 8:])], axis=0)
            mu0 = mu_ref[0:1, col0:col0 + w]
            mu1 = mu_ref[1:2, col0:col0 + w]
            return x * (1.0 - mu0 - mu1) + mu0 * prev + mu1 * nxt

        rs = shifted(r_ref, 0, width)
        ks = shifted(k_ref, width, width)
        vs = shifted(v_ref, 2 * width, width)
        wa = shifted(wa_ref, 3 * width, 2 * LANES)
        sel = lo_r if e == 0 else jnp.logical_not(lo_r)
        wd = jnp.where(sel, jnp.tanh(wa[:, :LANES]), 0.0).astype(BF16)
        ad = jnp.where(sel, wa[:, LANES:], 0.0).astype(BF16)
        zw = w0_ref[e:e + 1, :] + _dot(wd, wup_ref[...])
        lw = -math.exp(-0.5) / (1.0 + jnp.exp(-zw))
        za = a0_ref[e:e + 1, :] + _dot(ad, aup_ref[...])
        a_sig = 1.0 / (1.0 + jnp.exp(-za))
        kk = ks * kk_ref[...]
        kk2 = kk * kk
        nrm2 = jnp.concatenate([_seg_sum(kk2[:, i * LANES:(i + 1) * LANES], lo_r) for i in range(n_pairs)], axis=1)
        kkn = kk / jnp.maximum(jnp.sqrt(nrm2), 1e-12)
        ka = ka_ref[...]
        ke = ks * ((1.0 - ka) + ka * a_sig)
        be = kkn * a_sig
        rkr = rs * ke * rk_ref[...]
        bonus = jnp.concatenate([_seg_sum(rkr[:, i * LANES:(i + 1) * LANES], lo_r) for i in range(n_pairs)], axis=1)
        bv_ref[0] = bonus * vs

        l_incl = _split3_dot(tri_bf, lw)
        l_tot = jnp.concatenate(
            [jnp.broadcast_to(l_incl[j * C + last:j * C + last + 1, :], (C, width)) for j in range(nch)], axis=0)
        w_inv = jnp.exp(-l_incl)
        d_end = jnp.exp(l_tot)
        w_end = d_end * w_inv
        at = -kkn * jnp.exp(l_incl - lw)
        rt = rs * jnp.exp(l_incl)
        bt = be * w_inv
        kt = ke * w_inv
        bh = be * w_end
        kh = ke * w_end
        for j in range(nch):
            rows = slice(j * C, (j + 1) * C)
            for pr in range(n_pairs):
                sl = slice(pr * LANES, (pr + 1) * LANES)
                units.append(dict(
                    e=e, j=j, pr=pr, strict=strict, incl=incl, levels=level_masks,
                    at=at[rows, sl], rt=rt[rows, sl], bt=bt[rows, sl], kt=kt[rows, sl],
                    bh=bh[rows, sl], kh=kh[rows, sl], v=vs[rows, sl], d_end=d_end[j * C:j * C + 1, sl]))

    for u in units:
        u["at_bf"] = u["at"].astype(BF16)
        u["v_st"] = stack_bf(u["v"].astype(BF16))
        lhs = jnp.concatenate([u["at_bf"], u["rt"].astype(BF16)], axis=0)
        rhs = jnp.concatenate([stack_bf(u["bt"].astype(BF16)), stack_bf(u["kt"].astype(BF16))], axis=0)
        g = _dot_nt(lhs, rhs)
        u["a_ab"] = jnp.where(u["strict"], g[:C, :LANES], 0.0).astype(BF16)
        u["a_ak"] = jnp.where(u["strict"], g[:C, LANES:], 0.0).astype(BF16)
        u["ly"] = jnp.concatenate([jnp.where(u["incl"], g[C:, :LANES], 0.0),
                                   jnp.where(u["incl"], g[C:, LANES:], 0.0)], axis=1).astype(BF16)
        u["t"] = eye_bf + u["a_ab"] * u["levels"][0]
    for lvl in range(1, len(units[0]["levels"])):
        for u in units:
            u["tmp"] = _dot(u["a_ab"] * u["levels"][lvl], stack_bf(u["t"])).astype(BF16)
        for u in units:
            u["t"] = u["t"] + _dot(u["t"], stack_bf(u["tmp"])).astype(BF16)
    for u in units:
        u["akv"] = _dot(u["a_ak"], u["v_st"]).astype(BF16)
    for u in units:
        x = jnp.concatenate([stack_bf(u["at_bf"]), stack_bf(u["akv"])], axis=1)
        pq = _dot(u["t"], x).astype(BF16)
        u["ry"] = jnp.concatenate(
            [jnp.concatenate([stack_bf(pq[:, :LANES]), stack_bf(pq[:, LANES:])], axis=1),
             jnp.concatenate([jnp.zeros((LANES, LANES), BF16), u["v_st"]], axis=1)], axis=0)
        ls = jnp.concatenate([stack(u["bh"]), stack(u["kh"])], axis=0)
        u["lyz"] = jnp.concatenate([u["ly"], ls.T.astype(BF16)], axis=0)
    for u in units:
        yz = _dot(u["lyz"], u["ry"])
        yy = yz[:C]
        zz = yz[C:]
        u["rpm"] = jnp.concatenate([u["rt"] + yy[:, :LANES], zz[:, :LANES]], axis=0).astype(BF16)
        u["y0"] = yy[:, LANES:]
        u["z"] = zz[:, LANES:]
        d_diag = jnp.where(eye, jnp.broadcast_to(u["d_end"], (LANES, LANES)), 0.0)
        u["d_col"] = jnp.sum(d_diag, axis=1, keepdims=True)

    by_key = {(u["e"], u["j"], u["pr"]): u for u in units}
    for e in range(2):
        y_ref = yb_ref if e == 1 else yf_ref
        order = list(range(nch))[::-1] if e == 1 else list(range(nch))
        ys = {}
        states = [state_ref[e, pr] for pr in range(n_pairs)]
        for j in order:
            for pr in range(n_pairs):
                u = by_key[(e, j, pr)]
                ym = _dot(u["rpm"], states[pr].astype(BF16))
                ys[(j, pr)] = ym[:C] + u["y0"]
                states[pr] = u["d_col"] * states[pr] + ym[C:] + u["z"]
        for pr in range(n_pairs):
            state_ref[e, pr] = states[pr]
        y_ref[0] = jnp.concatenate(
            [jnp.concatenate([ys[(j, pr)] for pr in range(n_pairs)], axis=1) for j in range(nch)], axis=0)


RWKV_CHUNKS_PER_STEP = 2


def _mixer_b_scan(p, mu, w0, w_up, a0, a_up, k_k, k_a, r_k):
    b, s, _ = p.shape
    width = 512
    nch = RWKV_CHUNKS_PER_STEP
    rows = nch * CHUNK
    nb = s // rows
    n_pairs = width // LANES

    def seq(off, w):
        return pl.BlockSpec((1, s, w), lambda bi, c, off=off, w=w: (bi, 0, off // w))

    def full(shape):
        return pl.BlockSpec(shape, lambda bi, c: tuple(0 for _ in shape))

    out_f = pl.BlockSpec((1, rows, width), lambda bi, c: (bi, c, 0))
    out_b = pl.BlockSpec((1, rows, width), lambda bi, c: (bi, nb - 1 - c, 0))
    sds = jax.ShapeDtypeStruct((b, s, width), F32)
    lora = 2 * HEAD
    return pl.pallas_call(
        functools.partial(_rwkv_kernel, nb=nb, s=s, nch=nch),
        grid=(b, nb),
        in_specs=[seq(B_R, width), seq(B_K, width), seq(B_V, width), seq(B_WA, 2 * LANES),
                  full((2, 3 * width + 2 * LANES)), full((2, width)), full((lora, width)),
                  full((2, width)), full((lora, width)), full((1, width)), full((1, width)), full((1, width))],
        out_specs=[out_f, out_b, out_f, out_b],
        out_shape=[sds, sds, sds, sds],
        scratch_shapes=[pltpu.VMEM((2, n_pairs, LANES, LANES), F32)],
        compiler_params=_params("parallel", "arbitrary"),
        name="mixer_b",
    )(p, p, p, p, mu, w0, w_up.reshape(lora, width).astype(BF16), a0, a_up.reshape(lora, width).astype(BF16),
      k_k.reshape(1, width), k_a.reshape(1, width), r_k.reshape(1, width))


def _merge_kernel(h_ref, hb_ref, oa_ref, yf_ref, yb_ref, bvf_ref, bvb_ref, gb0_ref, gb1_ref, oc_ref, od_ref,
                  om_ref, wg_ref, bg_ref, wb_ref, wo_ref, lng_ref, lnb_ref, bg_g_ref, bg_b_ref,
                  hn_ref, hnb_ref, *, alpha):
    d = h_ref.shape[1]
    tm = h_ref.shape[0]
    lo = _lane_lo(tm)
    y = yf_ref[...] + yb_ref[...]
    n_pairs = y.shape[1] // LANES
    gn = []
    for i in range(n_pairs):
        yp = y[:, i * LANES:(i + 1) * LANES]
        mu = _seg_sum(yp, lo) * (1.0 / HEAD)
        dy = yp - mu
        var = _seg_sum(dy * dy, lo) * (1.0 / HEAD)
        gn.append(dy * lax.rsqrt(var + B_GN_EPS))
    gn = jnp.concatenate(gn, axis=1) * bg_g_ref[...] + bg_b_ref[...]
    gb = jnp.concatenate([gb0_ref[...], gb1_ref[...]], axis=1)
    ob = (gn + bvf_ref[...] + bvb_ref[...]) * _silu(gb)

    hb = hb_ref[...]
    branches = (oa_ref[...], ob, oc_ref[...], od_ref[...], om_ref[...])
    acc = None
    row = 0
    for i, o in enumerate(branches):
        wdt = o.shape[1]
        gate = 1.0 / (1.0 + jnp.exp(-(_dot(hb, wg_ref[:, i * d:(i + 1) * d]) + bg_ref[:, i * d:(i + 1) * d])))
        proj = _dot(o.astype(BF16), wb_ref[row:row + wdt, :])
        term = gate * proj
        acc = term if acc is None else acc + term
        row += wdt
    out = _dot(acc.astype(BF16), wo_ref[...])
    z = alpha * h_ref[...] + out
    mu = jnp.mean(z, -1, keepdims=True)
    dz = z - mu
    var = jnp.mean(dz * dz, -1, keepdims=True)
    hn = dz * lax.rsqrt(var + 1e-5) * lng_ref[...] + lnb_ref[...]
    hn_ref[...] = hn
    hnb_ref[...] = hn.astype(BF16)


def _merge(h, hb, o_a, yf, yb, bvf, bvb, p2, o_c, o_d, o_m, w_gate, b_gate, w_branch, w_out, ln_g, ln_b,
           bln_g, bln_b):
    n, d = h.shape
    tm = 256
    alpha = (2 * DEPTH) ** 0.25

    def rows(w):
        return pl.BlockSpec((tm, w), lambda i: (i, 0))

    def full(shape):
        return pl.BlockSpec(shape, lambda i: tuple(0 for _ in shape))

    gb_specs = pl.BlockSpec((tm, 256), lambda i: (i, B_G // 256))
    gb_specs2 = pl.BlockSpec((tm, 256), lambda i: (i, B_G // 256 + 1))
    return pl.pallas_call(
        functools.partial(_merge_kernel, alpha=alpha),
        grid=(n // tm,),
        in_specs=[rows(d), rows(d), rows(512), rows(512), rows(512), rows(512), rows(512),
                  gb_specs, gb_specs2, rows(512), rows(512), rows(256),
                  full(w_gate.shape), full((1, b_gate.shape[0])), full(w_branch.shape), full(w_out.shape),
                  full((1, d)), full((1, d)), full((1, 512)), full((1, 512))],
        out_specs=[rows(d), rows(d)],
        out_shape=[jax.ShapeDtypeStruct((n, d), F32), jax.ShapeDtypeStruct((n, d), BF16)],
        compiler_params=_params("parallel"),
        name="merge",
    )(h, hb, o_a, yf, yb, bvf, bvb, p2, p2, o_c, o_d, o_m,
      w_gate, b_gate.reshape(1, -1), w_branch, w_out, ln_g.reshape(1, d), ln_b.reshape(1, d),
      bln_g.reshape(1, 512), bln_b.reshape(1, 512))


def kernel(x, mem, ln_in_g, ln_in_b, rel_bias, w_in, shift_mu, rwkv_w0, rwkv_w_up, rwkv_a0, rwkv_a_up,
           rwkv_k_k, rwkv_k_a, rwkv_r_k, rwkv_ln_g, rwkv_ln_b, c_qnorm_g, c_knorm_g, d_lambda, d_subln_g,
           w_mem_kv, w_branch, w_gate, b_gate, w_out, ln_g, ln_b):
    b, s, d = x.shape
    n = b * s
    bias = _bias_tiles(rel_bias, s // TB)
    cos, sin = _rope_tables(s)
    mem_bf = mem.reshape(b * mem.shape[1], d).astype(BF16)
    h, hb = _ln_in(x.reshape(n, d), ln_in_g, ln_in_b)
    w_in, w_mem_kv, w_gate, w_branch, w_out = (w.astype(BF16) for w in (w_in, w_mem_kv, w_gate, w_branch, w_out))
    for l in range(DEPTH):
        p2 = _matmul(hb, w_in[l], 1024, 2048, "proj_in")
        p = p2.reshape(b, s, IN_COLS)
        kv = _matmul(mem_bf, w_mem_kv[l], 512, 512, "proj_mem").reshape(b, mem.shape[1], -1)
        o_a = _mixer_a(p, bias)
        yf, yb, bvf, bvb = _mixer_b_scan(p, shift_mu[l], rwkv_w0[l], rwkv_w_up[l], rwkv_a0[l], rwkv_a_up[l],
                                         rwkv_k_k[l], rwkv_k_a[l], rwkv_r_k[l])
        o_c = _mixer_c(p, cos, sin, c_qnorm_g[l], c_knorm_g[l])
        o_d = _mixer_d(p, bias, d_lambda[l], d_subln_g[l], l)
        o_m = _mixer_m(p, kv)
        flat = lambda t: t.reshape(n, t.shape[-1])
        h, hb = _merge(h, hb, flat(o_a), flat(yf), flat(yb), flat(bvf), flat(bvb), p2, flat(o_c), flat(o_d),
                       flat(o_m), w_gate[l], b_gate[l], w_branch[l], w_out[l], ln_g[l], ln_b[l],
                       rwkv_ln_g[l], rwkv_ln_b[l])
    return h.reshape(b, s, d)
```

```python
import functools
import math

import numpy as np
import jax
import jax.numpy as jnp
from jax import lax
from jax.experimental import pallas as pl
from jax.experimental.pallas import tpu as pltpu

F32 = jnp.float32
BF16 = jnp.bfloat16

LANES = 128
HEAD = 64
VMEM_LIMIT = 56 * 1024 * 1024

D_MODEL = 1024
DEPTH = 2
GRID_W = 64
ROPE_THETA = 10000.0
NUM_BUCKETS = 32
REL_MAX_DISTANCE = 1024
A_HEADS = 8
D_HEADS = 4
A_GROUPS = ((128, 1), (512, 4), (2048, 16))
B_GN_EPS = 64e-5
NEG_INF = -1e30

A_Q, A_K, A_V, A_G = 0, 512, 1024, 1536
B_R, B_K, B_V, B_WA, B_G = 2048, 2560, 3072, 3584, 3840
C_Q, C_K, C_V, C_G = 4352, 4864, 4992, 5120
D_Q, D_K, D_V, D_G = 5632, 6144, 6656, 7168
M_Q, M_G = 7680, 7936
IN_COLS = 8192

TB = 128
TQ = 256
LOG2E = math.log2(math.e)
CHUNK = 64


def _dot(a, b):
    return jnp.dot(a, b, preferred_element_type=F32)


def _dot_nt(a, b):
    return lax.dot_general(a, b, (((1,), (1,)), ((), ())), preferred_element_type=F32)


def _silu(g):
    return g / (1.0 + jnp.exp(-g))


def _params(*sem):
    return pltpu.CompilerParams(dimension_semantics=sem, vmem_limit_bytes=VMEM_LIMIT)


def _ln_in_kernel(x_ref, g_ref, b_ref, h_ref, hb_ref):
    x = x_ref[...]
    mu = jnp.mean(x, -1, keepdims=True)
    d = x - mu
    var = jnp.mean(d * d, -1, keepdims=True)
    h = d * lax.rsqrt(var + 1e-5) * g_ref[...] + b_ref[...]
    h_ref[...] = h
    hb_ref[...] = h.astype(BF16)


def _ln_in(x2, g, b):
    n, d = x2.shape
    tm = 512
    return pl.pallas_call(
        _ln_in_kernel,
        grid=(n // tm,),
        in_specs=[pl.BlockSpec((tm, d), lambda i: (i, 0)),
                  pl.BlockSpec((1, d), lambda i: (0, 0)),
                  pl.BlockSpec((1, d), lambda i: (0, 0))],
        out_specs=[pl.BlockSpec((tm, d), lambda i: (i, 0)),
                   pl.BlockSpec((tm, d), lambda i: (i, 0))],
        out_shape=[jax.ShapeDtypeStruct((n, d), F32), jax.ShapeDtypeStruct((n, d), BF16)],
        compiler_params=_params("parallel"),
        name="ln_in",
    )(x2, g.reshape(1, d), b.reshape(1, d))


def _matmul_kernel(a_ref, w_ref, o_ref, wbf_ref):
    @pl.when(pl.program_id(1) == 0)
    def _():
        wbf_ref[...] = w_ref[...].astype(BF16)

    o_ref[...] = _dot(a_ref[...], wbf_ref[...])


def _matmul(a, w, layer, tm, tn, name):
    m, k = a.shape
    n = w.shape[2]
    return pl.pallas_call(
        _matmul_kernel,
        grid=(n // tn, m // tm),
        in_specs=[pl.BlockSpec((tm, k), lambda j, i: (i, 0)),
                  pl.BlockSpec((None, k, tn), lambda j, i: (layer, 0, j))],
        out_specs=pl.BlockSpec((tm, tn), lambda j, i: (i, j)),
        out_shape=jax.ShapeDtypeStruct((m, n), F32),
        scratch_shapes=[pltpu.VMEM((k, tn), BF16)],
        compiler_params=_params("parallel", "arbitrary"),
        name=name,
    )(a, w)


def _rel_bucket_np(rel):
    nb = NUM_BUCKETS // 2
    max_exact = nb // 2
    n = np.abs(rel)
    nf = np.maximum(n, 1).astype(np.float32)
    large = max_exact + (np.log(nf / np.float32(max_exact)) / np.float32(math.log(REL_MAX_DISTANCE / max_exact))
                         * np.float32(nb - max_exact)).astype(np.int32)
    large = np.minimum(large, nb - 1)
    return (np.where(rel > 0, nb, 0) + np.where(n < max_exact, n, large)).astype(np.int32)


def _tile_deltas(n_blk):
    d = np.arange(2 * n_blk - 1)[:, None, None] - (n_blk - 1)
    r = np.arange(TB)[None, :, None]
    c = np.arange(TB)[None, None, :]
    return d * TB + c - r


def _dilated_log_multiplicity(delta):
    mult = np.zeros(delta.shape, np.float32)
    for window, dil in A_GROUPS:
        mult += ((delta % dil == 0) & (np.abs(delta) <= window // 2)).astype(np.float32)
    with np.errstate(divide="ignore"):
        return np.where(mult > 0, np.log(np.maximum(mult, 1.0)), NEG_INF).astype(np.float32)


def _bias_kernel(table_ref, bucket_ref, base_ref, o_ref, *, n_a, tile_bucket):
    h = pl.program_id(0)
    for d, const in enumerate(tile_bucket):
        acc = jnp.where(h < n_a, base_ref[d], 0.0)
        if const is not None:
            acc = acc + table_ref[const, h]
        else:
            bucket = bucket_ref[d]
            for b in range(NUM_BUCKETS):
                acc = acc + jnp.where(bucket == b, table_ref[b, h], 0.0)
        o_ref[0, d] = acc * LOG2E


def _bias_tiles(rel_bias, n_blk):
    delta = _tile_deltas(n_blk)
    bucket_np = _rel_bucket_np(delta)
    tile_bucket = tuple(int(t.flat[0]) if (t == t.flat[0]).all() else None for t in bucket_np)
    bucket = jnp.asarray(bucket_np)
    base = jnp.asarray(_dilated_log_multiplicity(delta))
    n_heads = rel_bias.shape[1]
    nd = 2 * n_blk - 1
    return pl.pallas_call(
        functools.partial(_bias_kernel, n_a=A_HEADS, tile_bucket=tile_bucket),
        grid=(n_heads,),
        in_specs=[pl.BlockSpec(memory_space=pltpu.SMEM),
                  pl.BlockSpec((nd, TB, TB), lambda h: (0, 0, 0)),
                  pl.BlockSpec((nd, TB, TB), lambda h: (0, 0, 0))],
        out_specs=pl.BlockSpec((1, nd, TB, TB), lambda h: (h, 0, 0, 0)),
        out_shape=jax.ShapeDtypeStruct((n_heads, nd, TB, TB), F32),
        compiler_params=_params("arbitrary"),
        name="bias_tiles",
    )(rel_bias, bucket, base)


def _lane_lo(rows):
    return lax.broadcasted_iota(jnp.int32, (rows, LANES), 1) < HEAD


def _with_ones(v_bf):
    return jnp.concatenate([v_bf, jnp.ones(v_bf.shape, BF16)], axis=1)


def _store_logits(buf, hh, s, bias_fn=None):
    rows, keys = s.shape
    for rb in range(rows // TB):
        for j in range(keys // LANES):
            c = s[rb * TB:(rb + 1) * TB, j * LANES:(j + 1) * LANES]
            if bias_fn is not None:
                c = c + bias_fn(rb, j)
            buf[hh, rb * TB:(rb + 1) * TB, j * LANES:(j + 1) * LANES] = c


def _softmax_pv(buf, hh, v_ext):
    _, rows, keys = buf.shape
    es = []
    for rb in range(rows // TB):
        def chunk(j):
            return buf[hh, rb * TB:(rb + 1) * TB, j * LANES:(j + 1) * LANES]
        m = chunk(0)
        for j in range(1, keys // LANES):
            m = jnp.maximum(m, chunk(j))
        m = jnp.max(m, axis=1, keepdims=True)
        es.append(jnp.concatenate([jnp.exp2(chunk(j) - m).astype(BF16) for j in range(keys // LANES)], axis=1))
    o = _dot(jnp.concatenate(es, axis=0), v_ext)
    return o[:, :LANES] / o[:, LANES:]


def _attn_loop(n_q, qk_store, finish, s_a, s_b):
    qk_store(s_a, 0)

    def body(i, carry):
        q0 = 2 * i
        qk_store(s_b, q0 + 1)
        finish(s_a, q0)
        qk_store(s_a, q0 + 2)
        finish(s_b, q0 + 1)
        return carry

    lax.fori_loop(0, n_q // 2 - 1, body, 0)
    qk_store(s_b, n_q - 1)
    finish(s_a, n_q - 2)
    finish(s_b, n_q - 1)


def _logit_bufs(keys):
    return [pltpu.VMEM((2, TQ, keys), F32), pltpu.VMEM((2, TQ, keys), F32)]


def _seg_sum(x, lo):
    s0 = jnp.sum(jnp.where(lo, x, 0.0), axis=1, keepdims=True)
    s1 = jnp.sum(jnp.where(lo, 0.0, x), axis=1, keepdims=True)
    return jnp.where(lo, s0, s1)


def _mixer_a_kernel(q_ref, k_ref, v_ref, g_ref, bias_ref, o_ref, s_a, s_b, *, n_blk):
    k_bf = k_ref[0].astype(BF16)
    v_ext = _with_ones(v_ref[0].astype(BF16))
    lo = _lane_lo(TQ)
    rpb = TQ // TB

    def qk_store(buf, qi):
        r0 = pl.multiple_of(qi * TQ, TQ)
        q = q_ref[0, pl.ds(r0, TQ), :] * (HEAD ** -0.5 * LOG2E)
        for hh in range(2):
            qm = jnp.where(lo if hh == 0 else jnp.logical_not(lo), q, 0.0).astype(BF16)
            _store_logits(buf, hh, _dot_nt(qm, k_bf),
                          lambda rb, j, hh=hh: bias_ref[hh, n_blk - 1 - (qi * rpb + rb) + j])

    def finish(buf, qi):
        r0 = pl.multiple_of(qi * TQ, TQ)
        o = jnp.where(lo, _softmax_pv(buf, 0, v_ext), _softmax_pv(buf, 1, v_ext))
        o_ref[0, pl.ds(r0, TQ), :] = o * _silu(g_ref[0, pl.ds(r0, TQ), :])

    _attn_loop(n_blk // rpb, qk_store, finish, s_a, s_b)


def _mixer_a(p, bias):
    b, s, _ = p.shape
    n_blk = s // TB
    nd = 2 * n_blk - 1
    n_pairs = A_HEADS // 2

    def col(off):
        return pl.BlockSpec((1, s, LANES), lambda bi, j, off=off: (bi, 0, off // LANES + j))

    return pl.pallas_call(
        functools.partial(_mixer_a_kernel, n_blk=n_blk),
        grid=(b, n_pairs),
        in_specs=[col(A_Q), col(A_K), col(A_V), col(A_G),
                  pl.BlockSpec((2, nd, TB, TB), lambda bi, j: (j, 0, 0, 0))],
        out_specs=pl.BlockSpec((1, s, LANES), lambda bi, j: (bi, 0, j)),
        out_shape=jax.ShapeDtypeStruct((b, s, n_pairs * LANES), F32),
        scratch_shapes=_logit_bufs(s),
        compiler_params=_params("parallel", "parallel"),
        name="mixer_a",
    )(p, p, p, p, bias)


def _mixer_d_kernel(q_ref, k_ref, v_ref, g_ref, bias_ref, lam_ref, sg_ref, o_ref, s_a, s_b, *, n_blk, lam_init):
    k_bf = k_ref[0].astype(BF16)
    v_ext = _with_ones(v_ref[0].astype(BF16))
    lo = _lane_lo(TQ)
    rpb = TQ // TB
    dl = lam_ref[...]
    lam = (jnp.exp(jnp.sum(dl[0:1] * dl[1:2], axis=1, keepdims=True))
           - jnp.exp(jnp.sum(dl[2:3] * dl[3:4], axis=1, keepdims=True)) + lam_init)
    sg = sg_ref[...] * (1.0 - lam_init)

    def qk_store(buf, qi):
        r0 = pl.multiple_of(qi * TQ, TQ)
        q = q_ref[0, pl.ds(r0, TQ), :] * (HEAD ** -0.5 * LOG2E)
        for hh in range(2):
            qm = jnp.where(lo if hh == 0 else jnp.logical_not(lo), q, 0.0).astype(BF16)
            _store_logits(buf, hh, _dot_nt(qm, k_bf),
                          lambda rb, j: bias_ref[0, n_blk - 1 - (qi * rpb + rb) + j])

    def finish(buf, qi):
        r0 = pl.multiple_of(qi * TQ, TQ)
        o = _softmax_pv(buf, 0, v_ext) - lam * _softmax_pv(buf, 1, v_ext)
        o = o * lax.rsqrt(jnp.mean(o * o, axis=1, keepdims=True) + 1e-5) * sg
        o_ref[0, pl.ds(r0, TQ), :] = o * _silu(g_ref[0, pl.ds(r0, TQ), :])

    _attn_loop(n_blk // rpb, qk_store, finish, s_a, s_b)


def _mixer_d(p, bias, d_lambda, subln_g, layer_idx):
    b, s, _ = p.shape
    n_blk = s // TB
    nd = 2 * n_blk - 1
    lam_init = 0.8 - 0.6 * math.exp(-0.3 * layer_idx)

    def col(off):
        return pl.BlockSpec((1, s, LANES), lambda bi, j, off=off: (bi, 0, off // LANES + j))

    return pl.pallas_call(
        functools.partial(_mixer_d_kernel, n_blk=n_blk, lam_init=lam_init),
        grid=(b, D_HEADS),
        in_specs=[col(D_Q), col(D_K), col(D_V), col(D_G),
                  pl.BlockSpec((1, nd, TB, TB), lambda bi, j: (A_HEADS + j, 0, 0, 0)),
                  pl.BlockSpec((4, HEAD), lambda bi, j: (0, 0)),
                  pl.BlockSpec((1, LANES), lambda bi, j: (0, 0))],
        out_specs=pl.BlockSpec((1, s, LANES), lambda bi, j: (bi, 0, j)),
        out_shape=jax.ShapeDtypeStruct((b, s, D_HEADS * LANES), F32),
        scratch_shapes=_logit_bufs(s),
        compiler_params=_params("parallel", "parallel"),
        name="mixer_d",
    )(p, p, p, p, bias, d_lambda, subln_g.reshape(1, LANES))


def _rope_tables(s):
    t = np.arange(s)
    row, colp = t // GRID_W, t % GRID_W
    qtr = HEAD // 4
    freqs = np.float32(ROPE_THETA) ** (-(np.arange(qtr, dtype=np.float32) / np.float32(qtr)))
    lane = np.arange(LANES) % HEAD
    use_col = (lane // (HEAD // 2)) == 1
    second = (lane % (HEAD // 2)) >= qtr
    pos = np.where(use_col[None, :], colp[:, None], row[:, None]).astype(np.float32)
    ang = (pos * freqs[lane % qtr][None, :]).astype(np.float32)
    cos = np.cos(ang).astype(np.float32)
    sin = np.where(second[None, :], np.sin(ang), -np.sin(ang)).astype(np.float32)
    return jnp.asarray(cos), jnp.asarray(sin)


def _norm_rope(x, gain, cos, sin, lo, first):
    ms = _seg_sum(x * x, lo) * (1.0 / HEAD)
    x = x * lax.rsqrt(ms + 1e-6) * gain
    qtr = HEAD // 4
    partner = jnp.where(first, pltpu.roll(x, LANES - qtr, axis=1), pltpu.roll(x, qtr, axis=1))
    return x * cos + partner * sin


def _mixer_c_kernel(q_ref, k_ref, v_ref, g_ref, cos_ref, sin_ref, qg_ref, kg_ref, o_ref, s_a, s_b, krot_ref,
                    *, n_blk, s):
    pair = pl.program_id(1)
    grp = pair // 2
    lo_s = _lane_lo(s)

    @pl.when(pair == 0)
    def _():
        lane_s = lax.broadcasted_iota(jnp.int32, (s, LANES), 1)
        first_s = (lane_s % (HEAD // 2)) < (HEAD // 4)
        krot_ref[...] = _norm_rope(k_ref[0], kg_ref[...], cos_ref[...], sin_ref[...], lo_s, first_s)

    k = krot_ref[...]
    v = v_ref[0]
    k_sw = pltpu.roll(k, HEAD, axis=1)
    v_sw = pltpu.roll(v, HEAD, axis=1)
    keep = jnp.logical_xor(lo_s, grp == 1)
    k_bf = jnp.where(keep, k, k_sw).astype(BF16)
    v_ext = _with_ones(jnp.where(keep, v, v_sw).astype(BF16))
    lo = _lane_lo(TQ)
    lane = lax.broadcasted_iota(jnp.int32, (TQ, LANES), 1)
    first = (lane % (HEAD // 2)) < (HEAD // 4)

    def qk_store(buf, qi):
        r0 = pl.multiple_of(qi * TQ, TQ)
        q = _norm_rope(q_ref[0, pl.ds(r0, TQ), :], qg_ref[...], cos_ref[pl.ds(r0, TQ), :],
                       sin_ref[pl.ds(r0, TQ), :], lo, first) * (HEAD ** -0.5 * LOG2E)
        for hh in range(2):
            qm = jnp.where(lo if hh == 0 else jnp.logical_not(lo), q, 0.0).astype(BF16)
            _store_logits(buf, hh, _dot_nt(qm, k_bf))

    def finish(buf, qi):
        r0 = pl.multiple_of(qi * TQ, TQ)
        o = jnp.where(lo, _softmax_pv(buf, 0, v_ext), _softmax_pv(buf, 1, v_ext))
        o_ref[0, pl.ds(r0, TQ), :] = o * _silu(g_ref[0, pl.ds(r0, TQ), :])

    _attn_loop(n_blk, qk_store, finish, s_a, s_b)


def _mixer_c(p, cos, sin, qn_g, kn_g):
    b, s, _ = p.shape
    n_blk = s // TQ
    n_pairs = 4

    def col(off):
        return pl.BlockSpec((1, s, LANES), lambda bi, j, off=off: (bi, 0, off // LANES + j))

    def fixed(off):
        return pl.BlockSpec((1, s, LANES), lambda bi, j, off=off: (bi, 0, off // LANES))

    tab = pl.BlockSpec((s, LANES), lambda bi, j: (0, 0))
    gain = pl.BlockSpec((1, LANES), lambda bi, j: (0, 0))
    return pl.pallas_call(
        functools.partial(_mixer_c_kernel, n_blk=n_blk, s=s),
        grid=(b, n_pairs),
        in_specs=[col(C_Q), fixed(C_K), fixed(C_V), col(C_G), tab, tab, gain, gain],
        out_specs=pl.BlockSpec((1, s, LANES), lambda bi, j: (bi, 0, j)),
        out_shape=jax.ShapeDtypeStruct((b, s, n_pairs * LANES), F32),
        scratch_shapes=_logit_bufs(s) + [pltpu.VMEM((s, LANES), F32)],
        compiler_params=_params("parallel", "arbitrary"),
        name="mixer_c",
    )(p, p, p, p, cos, sin, jnp.tile(qn_g, 2).reshape(1, LANES), jnp.tile(kn_g, 2).reshape(1, LANES))


def _mixer_m_kernel(q_ref, g_ref, k_ref, v_ref, o_ref, s_a, s_b, *, n_blk):
    k_bf = k_ref[0].astype(BF16)
    v_ext = _with_ones(v_ref[0].astype(BF16))
    lo = _lane_lo(TQ)

    def qk_store(buf, qi):
        r0 = pl.multiple_of(qi * TQ, TQ)
        q = q_ref[0, pl.ds(r0, TQ), :] * (HEAD ** -0.5 * LOG2E)
        for hh in range(2):
            qm = jnp.where(lo if hh == 0 else jnp.logical_not(lo), q, 0.0).astype(BF16)
            _store_logits(buf, hh, _dot_nt(qm, k_bf))

    def finish(buf, qi):
        r0 = pl.multiple_of(qi * TQ, TQ)
        o = jnp.where(lo, _softmax_pv(buf, 0, v_ext), _softmax_pv(buf, 1, v_ext))
        o_ref[0, pl.ds(r0, TQ), :] = o * _silu(g_ref[0, pl.ds(r0, TQ), :])

    _attn_loop(n_blk, qk_store, finish, s_a, s_b)


def _mixer_m(p, kv):
    b, s, _ = p.shape
    n_mem = kv.shape[1]
    n_blk = s // TQ
    n_pairs = 2

    def col(off):
        return pl.BlockSpec((1, s, LANES), lambda bi, j, off=off: (bi, 0, off // LANES + j))

    return pl.pallas_call(
        functools.partial(_mixer_m_kernel, n_blk=n_blk),
        grid=(b, n_pairs),
        in_specs=[col(M_Q), col(M_G),
                  pl.BlockSpec((1, n_mem, LANES), lambda bi, j: (bi, 0, j)),
                  pl.BlockSpec((1, n_mem, LANES), lambda bi, j: (bi, 0, n_pairs + j))],
        out_specs=pl.BlockSpec((1, s, LANES), lambda bi, j: (bi, 0, j)),
        out_shape=jax.ShapeDtypeStruct((b, s, n_pairs * LANES), F32),
        scratch_shapes=_logit_bufs(n_mem),
        compiler_params=_params("parallel", "parallel"),
        name="mixer_m",
    )(p, p, kv, kv)


def _split3_dot(tri_bf, x):
    h1 = x.astype(BF16)
    r1 = x - h1.astype(F32)
    h2 = r1.astype(BF16)
    h3 = (r1 - h2.astype(F32)).astype(BF16)
    return _dot(tri_bf, h1) + _dot(tri_bf, h2) + _dot(tri_bf, h3)


def _rwkv_kernel(r_ref, k_ref, v_ref, wa_ref, mu_ref, w0_ref, wup_ref, a0_ref, aup_ref,
                 kk_ref, ka_ref, rk_ref, yf_ref, yb_ref, bvf_ref, bvb_ref, state_ref, *, nb, s, nch):
    C = CHUNK
    R = nch * C
    c = pl.program_id(1)

    @pl.when(c == 0)
    def _():
        state_ref[...] = jnp.zeros_like(state_ref)

    n_pairs = r_ref.shape[2] // LANES
    width = r_ref.shape[2]
    tt = lax.broadcasted_iota(jnp.int32, (C, LANES), 0)
    ss = lax.broadcasted_iota(jnp.int32, (C, LANES), 1) % C
    eye_bf = jnp.where(tt == ss, 1.0, 0.0).astype(BF16)
    rr = lax.broadcasted_iota(jnp.int32, (LANES, LANES), 0)
    cc = lax.broadcasted_iota(jnp.int32, (LANES, LANES), 1)
    eye = rr == cc
    tr = lax.broadcasted_iota(jnp.int32, (R, R), 0)
    tc = lax.broadcasted_iota(jnp.int32, (R, R), 1)
    same_chunk = (tr // C) == (tc // C)
    lo = _lane_lo(C)
    hi = jnp.logical_not(lo)
    lo_bf = jnp.where(lo, 1.0, 0.0).astype(BF16)
    hi_bf = jnp.where(hi, 1.0, 0.0).astype(BF16)
    lo_r = _lane_lo(R)

    def stack(x):
        return jnp.concatenate([jnp.where(lo, x, 0.0), jnp.where(hi, x, 0.0)], axis=0)

    def stack_bf(x):
        return jnp.concatenate([x * lo_bf, x * hi_bf], axis=0)

    units = []
    for e in range(2):
        rev = e == 1
        blk = (nb - 1 - c) if rev else c
        r0 = pl.multiple_of(blk * R, R)
        bv_ref = bvb_ref if rev else bvf_ref
        strict = (tt < ss) if rev else (tt > ss)
        incl = (tt <= ss) if rev else (tt >= ss)
        tri_bf = jnp.where(jnp.logical_and(same_chunk, (tr <= tc) if rev else (tr >= tc)), 1.0, 0.0).astype(BF16)
        last = 0 if rev else C - 1
        up, dn = (ss, tt) if rev else (tt, ss)
        level_masks = []
        m = 1
        while m < C:
            same_blk = ((tt ^ ss) >> (int(math.log2(m)) + 1)) == 0
            lvl_mask = jnp.logical_and(same_blk, jnp.logical_and((up & m) != 0, (dn & m) == 0))
            level_masks.append(jnp.where(lvl_mask, 1.0, 0.0).astype(BF16))
            m *= 2

        def shifted(ref, col0, w):
            x = ref[0, pl.ds(r0, R), :]
            prev_row = ref[0, pl.ds(jnp.maximum(r0 - 1, 0), 1), :]
            prev_row = jnp.where(blk == 0, 0.0, prev_row)
            next_row = ref[0, pl.ds(jnp.minimum(r0 + R, s - 1), 1), :]
            next_row = jnp.where(blk == nb - 1, 0.0, next_row)
            sub = lax.broadcasted_iota(jnp.int32, (8, w), 0)
            down = pltpu.roll(x, 1, axis=0)
            up_ = pltpu.roll(x, R - 1, axis=0)
            prev = jnp.concatenate([jnp.where(sub == 0, prev_row, down[:8]), down[8:]], axis=0)
            nxt = jnp.concatenate([up_[:R - 8], jnp.where(sub == 7, next_row, up_[R - 8:])], axis=0)
            mu0 = mu_ref[0:1, col0:col0 + w]
            mu1 = mu_ref[1:2, col0:col0 + w]
            return x * (1.0 - mu0 - mu1) + mu0 * prev + mu1 * nxt

        rs = shifted(r_ref, 0, width)
        ks = shifted(k_ref, width, width)
        vs = shifted(v_ref, 2 * width, width)
        wa = shifted(wa_ref, 3 * width, 2 * LANES)
        sel = lo_r if e == 0 else jnp.logical_not(lo_r)
        wd = jnp.where(sel, jnp.tanh(wa[:, :LANES]), 0.0).astype(BF16)
        ad = jnp.where(sel, wa[:, LANES:], 0.0).astype(BF16)
        zw = w0_ref[e:e + 1, :] + _dot(wd, wup_ref[...])
        lw = -math.exp(-0.5) / (1.0 + jnp.exp(-zw))
        za = a0_ref[e:e + 1, :] + _dot(ad, aup_ref[...])
        a_sig = 1.0 / (1.0 + jnp.exp(-za))
        kk = ks * kk_ref[...]
        kk2 = kk * kk
        nrm2 = jnp.concatenate([_seg_sum(kk2[:, i * LANES:(i + 1) * LANES], lo_r) for i in range(n_pairs)], axis=1)
        kkn = kk / jnp.maximum(jnp.sqrt(nrm2), 1e-12)
        ka = ka_ref[...]
        ke = ks * ((1.0 - ka) + ka * a_sig)
        be = kkn * a_sig
        rkr = rs * ke * rk_ref[...]
        bonus = jnp.concatenate([_seg_sum(rkr[:, i * LANES:(i + 1) * LANES], lo_r) for i in range(n_pairs)], axis=1)
        bv_ref[0] = bonus * vs

        l_incl = _split3_dot(tri_bf, lw)
        l_tot = jnp.concatenate(
            [jnp.broadcast_to(l_incl[j * C + last:j * C + last + 1, :], (C, width)) for j in range(nch)], axis=0)
        w_inv = jnp.exp(-l_incl)
        d_end = jnp.exp(l_tot)
        w_end = d_end * w_inv
        at = -kkn * jnp.exp(l_incl - lw)
        rt = rs * jnp.exp(l_incl)
        bt = be * w_inv
        kt = ke * w_inv
        bh = be * w_end
        kh = ke * w_end
        for j in range(nch):
            rows = slice(j * C, (j + 1) * C)
            for pr in range(n_pairs):
                sl = slice(pr * LANES, (pr + 1) * LANES)
                units.append(dict(
                    e=e, j=j, pr=pr, strict=strict, incl=incl, levels=level_masks,
                    at=at[rows, sl], rt=rt[rows, sl], bt=bt[rows, sl], kt=kt[rows, sl],
                    bh=bh[rows, sl], kh=kh[rows, sl], v=vs[rows, sl], d_end=d_end[j * C:j * C + 1, sl]))

    for u in units:
        u["at_bf"] = u["at"].astype(BF16)
        u["v_st"] = stack_bf(u["v"].astype(BF16))
        lhs = jnp.concatenate([u["at_bf"], u["rt"].astype(BF16)], axis=0)
        rhs = jnp.concatenate([stack_bf(u["bt"].astype(BF16)), stack_bf(u["kt"].astype(BF16))], axis=0)
        g = _dot_nt(lhs, rhs)
        u["a_ab"] = jnp.where(u["strict"], g[:C, :LANES], 0.0).astype(BF16)
        u["a_ak"] = jnp.where(u["strict"], g[:C, LANES:], 0.0).astype(BF16)
        u["ly"] = jnp.concatenate([jnp.where(u["incl"], g[C:, :LANES], 0.0),
                                   jnp.where(u["incl"], g[C:, LANES:], 0.0)], axis=1).astype(BF16)
        u["t"] = eye_bf + u["a_ab"] * u["levels"][0]
    for lvl in range(1, len(units[0]["levels"])):
        for u in units:
            u["tmp"] = _dot(u["a_ab"] * u["levels"][lvl], stack_bf(u["t"])).astype(BF16)
        for u in units:
            u["t"] = u["t"] + _dot(u["t"], stack_bf(u["tmp"])).astype(BF16)
    for u in units:
        u["akv"] = _dot(u["a_ak"], u["v_st"]).astype(BF16)
    for u in units:
        x = jnp.concatenate([stack_bf(u["at_bf"]), stack_bf(u["akv"])], axis=1)
        pq = _dot(u["t"], x).astype(BF16)
        u["ry"] = jnp.concatenate(
            [jnp.concatenate([stack_bf(pq[:, :LANES]), stack_bf(pq[:, LANES:])], axis=1),
             jnp.concatenate([jnp.zeros((LANES, LANES), BF16), u["v_st"]], axis=1)], axis=0)
        ls = jnp.concatenate([stack(u["bh"]), stack(u["kh"])], axis=0)
        u["lyz"] = jnp.concatenate([u["ly"], ls.T.astype(BF16)], axis=0)
    for u in units:
        yz = _dot(u["lyz"], u["ry"])
        yy = yz[:C]
        zz = yz[C:]
        u["rpm"] = jnp.concatenate([u["rt"] + yy[:, :LANES], zz[:, :LANES]], axis=0).astype(BF16)
        u["y0"] = yy[:, LANES:]
        u["z"] = zz[:, LANES:]
        d_diag = jnp.where(eye, jnp.broadcast_to(u["d_end"], (LANES, LANES)), 0.0)
        u["d_col"] = jnp.sum(d_diag, axis=1, keepdims=True)

    by_key = {(u["e"], u["j"], u["pr"]): u for u in units}
    for e in range(2):
        y_ref = yb_ref if e == 1 else yf_ref
        order = list(range(nch))[::-1] if e == 1 else list(range(nch))
        ys = {}
        states = [state_ref[e, pr] for pr in range(n_pairs)]
        for j in order:
            for pr in range(n_pairs):
                u = by_key[(e, j, pr)]
                ym = _dot(u["rpm"], states[pr].astype(BF16))
                ys[(j, pr)] = ym[:C] + u["y0"]
                states[pr] = u["d_col"] * states[pr] + ym[C:] + u["z"]
        for pr in range(n_pairs):
            state_ref[e, pr] = states[pr]
        y_ref[0] = jnp.concatenate(
            [jnp.concatenate([ys[(j, pr)] for pr in range(n_pairs)], axis=1) for j in range(nch)], axis=0)


RWKV_CHUNKS_PER_STEP = 2


def _mixer_b_scan(p, mu, w0, w_up, a0, a_up, k_k, k_a, r_k):
    b, s, _ = p.shape
    width = 512
    nch = RWKV_CHUNKS_PER_STEP
    rows = nch * CHUNK
    nb = s // rows
    n_pairs = width // LANES

    def seq(off, w):
        return pl.BlockSpec((1, s, w), lambda bi, c, off=off, w=w: (bi, 0, off // w))

    def full(shape):
        return pl.BlockSpec(shape, lambda bi, c: tuple(0 for _ in shape))

    out_f = pl.BlockSpec((1, rows, width), lambda bi, c: (bi, c, 0))
    out_b = pl.BlockSpec((1, rows, width), lambda bi, c: (bi, nb - 1 - c, 0))
    sds = jax.ShapeDtypeStruct((b, s, width), F32)
    lora = 2 * HEAD
    return pl.pallas_call(
        functools.partial(_rwkv_kernel, nb=nb, s=s, nch=nch),
        grid=(b, nb),
        in_specs=[seq(B_R, width), seq(B_K, width), seq(B_V, width), seq(B_WA, 2 * LANES),
                  full((2, 3 * width + 2 * LANES)), full((2, width)), full((lora, width)),
                  full((2, width)), full((lora, width)), full((1, width)), full((1, width)), full((1, width))],
        out_specs=[out_f, out_b, out_f, out_b],
        out_shape=[sds, sds, sds, sds],
        scratch_shapes=[pltpu.VMEM((2, n_pairs, LANES, LANES), F32)],
        compiler_params=_params("parallel", "arbitrary"),
        name="mixer_b",
    )(p, p, p, p, mu, w0, w_up.reshape(lora, width).astype(BF16), a0, a_up.reshape(lora, width).astype(BF16),
      k_k.reshape(1, width), k_a.reshape(1, width), r_k.reshape(1, width))


def _merge_kernel(h_ref, hb_ref, oa_ref, yf_ref, yb_ref, bvf_ref, bvb_ref, gb0_ref, gb1_ref, oc_ref, od_ref,
                  om_ref, wg_ref, bg_ref, wb_ref, wo_ref, lng_ref, lnb_ref, bg_g_ref, bg_b_ref,
                  hn_ref, hnb_ref, *, alpha):
    d = h_ref.shape[1]
    tm = h_ref.shape[0]
    lo = _lane_lo(tm)
    y = yf_ref[...] + yb_ref[...]
    n_pairs = y.shape[1] // LANES
    gn = []
    for i in range(n_pairs):
        yp = y[:, i * LANES:(i + 1) * LANES]
        mu = _seg_sum(yp, lo) * (1.0 / HEAD)
        dy = yp - mu
        var = _seg_sum(dy * dy, lo) * (1.0 / HEAD)
        gn.append(dy * lax.rsqrt(var + B_GN_EPS))
    gn = jnp.concatenate(gn, axis=1) * bg_g_ref[...] + bg_b_ref[...]
    gb = jnp.concatenate([gb0_ref[...], gb1_ref[...]], axis=1)
    ob = (gn + bvf_ref[...] + bvb_ref[...]) * _silu(gb)

    hb = hb_ref[...]
    branches = (oa_ref[...], ob, oc_ref[...], od_ref[...], om_ref[...])
    acc = None
    row = 0
    for i, o in enumerate(branches):
        wdt = o.shape[1]
        gate = 1.0 / (1.0 + jnp.exp(-(_dot(hb, wg_ref[:, i * d:(i + 1) * d]) + bg_ref[:, i * d:(i + 1) * d])))
        proj = _dot(o.astype(BF16), wb_ref[row:row + wdt, :])
        term = gate * proj
        acc = term if acc is None else acc + term
        row += wdt
    out = _dot(acc.astype(BF16), wo_ref[...])
    z = alpha * h_ref[...] + out
    mu = jnp.mean(z, -1, keepdims=True)
    dz = z - mu
    var = jnp.mean(dz * dz, -1, keepdims=True)
    hn = dz * lax.rsqrt(var + 1e-5) * lng_ref[...] + lnb_ref[...]
    hn_ref[...] = hn
    hnb_ref[...] = hn.astype(BF16)


def _merge(h, hb, o_a, yf, yb, bvf, bvb, p2, o_c, o_d, o_m, layer, w_gate, b_gate, w_branch, w_out, ln_g, ln_b,
           bln_g, bln_b):
    n, d = h.shape
    tm = 256
    alpha = (2 * DEPTH) ** 0.25

    def rows(w):
        return pl.BlockSpec((tm, w), lambda i: (i, 0))

    def full(shape):
        return pl.BlockSpec(shape, lambda i: tuple(0 for _ in shape))

    def of_layer(w):
        return pl.BlockSpec((None,) + w.shape[1:], lambda i: (layer, 0, 0))

    gb_specs = pl.BlockSpec((tm, 256), lambda i: (i, B_G // 256))
    gb_specs2 = pl.BlockSpec((tm, 256), lambda i: (i, B_G // 256 + 1))
    return pl.pallas_call(
        functools.partial(_merge_kernel, alpha=alpha),
        grid=(n // tm,),
        in_specs=[rows(d), rows(d), rows(512), rows(512), rows(512), rows(512), rows(512),
                  gb_specs, gb_specs2, rows(512), rows(512), rows(256),
                  of_layer(w_gate), full((1, b_gate.shape[0])), of_layer(w_branch), of_layer(w_out),
                  full((1, d)), full((1, d)), full((1, 512)), full((1, 512))],
        out_specs=[rows(d), rows(d)],
        out_shape=[jax.ShapeDtypeStruct((n, d), F32), jax.ShapeDtypeStruct((n, d), BF16)],
        compiler_params=_params("parallel"),
        name="merge",
    )(h, hb, o_a, yf, yb, bvf, bvb, p2, p2, o_c, o_d, o_m,
      w_gate, b_gate.reshape(1, -1), w_branch, w_out, ln_g.reshape(1, d), ln_b.reshape(1, d),
      bln_g.reshape(1, 512), bln_b.reshape(1, 512))


def kernel(x, mem, ln_in_g, ln_in_b, rel_bias, w_in, shift_mu, rwkv_w0, rwkv_w_up, rwkv_a0, rwkv_a_up,
           rwkv_k_k, rwkv_k_a, rwkv_r_k, rwkv_ln_g, rwkv_ln_b, c_qnorm_g, c_knorm_g, d_lambda, d_subln_g,
           w_mem_kv, w_branch, w_gate, b_gate, w_out, ln_g, ln_b):
    b, s, d = x.shape
    n = b * s
    bias = _bias_tiles(rel_bias, s // TB)
    cos, sin = _rope_tables(s)
    mem_bf = mem.reshape(b * mem.shape[1], d).astype(BF16)
    h, hb = _ln_in(x.reshape(n, d), ln_in_g, ln_in_b)
    w_gate, w_branch, w_out = (w.astype(BF16) for w in (w_gate, w_branch, w_out))
    for l in range(DEPTH):
        p2 = _matmul(hb, w_in, l, 1024, 2048, "proj_in")
        p = p2.reshape(b, s, IN_COLS)
        kv = _matmul(mem_bf, w_mem_kv, l, 512, 512, "proj_mem").reshape(b, mem.shape[1], -1)
        o_a = _mixer_a(p, bias)
        yf, yb, bvf, bvb = _mixer_b_scan(p, shift_mu[l], rwkv_w0[l], rwkv_w_up[l], rwkv_a0[l], rwkv_a_up[l],
                                         rwkv_k_k[l], rwkv_k_a[l], rwkv_r_k[l])
        o_c = _mixer_c(p, cos, sin, c_qnorm_g[l], c_knorm_g[l])
        o_d = _mixer_d(p, bias, d_lambda[l], d_subln_g[l], l)
        o_m = _mixer_m(p, kv)
        flat = lambda t: t.reshape(n, t.shape[-1])
        h, hb = _merge(h, hb, flat(o_a), flat(yf), flat(yb), flat(bvf), flat(bvb), p2, flat(o_c), flat(o_d),
                       flat(o_m), l, w_gate, b_gate[l], w_branch, w_out, ln_g[l], ln_b[l],
                       rwkv_ln_g[l], rwkv_ln_b[l])
    return h.reshape(b, s, d)
```

```python
import functools
import math

import numpy as np
import jax
import jax.numpy as jnp
from jax import lax
from jax.experimental import pallas as pl
from jax.experimental.pallas import tpu as pltpu

F32 = jnp.float32
BF16 = jnp.bfloat16

LANES = 128
HEAD = 64
VMEM_LIMIT = 56 * 1024 * 1024

D_MODEL = 1024
DEPTH = 2
GRID_W = 64
ROPE_THETA = 10000.0
NUM_BUCKETS = 32
REL_MAX_DISTANCE = 1024
A_HEADS = 8
D_HEADS = 4
A_GROUPS = ((128, 1), (512, 4), (2048, 16))
B_GN_EPS = 64e-5
NEG_INF = -1e30

A_Q, A_K, A_V, A_G = 0, 512, 1024, 1536
B_R, B_K, B_V, B_WA, B_G = 2048, 2560, 3072, 3584, 3840
C_Q, C_K, C_V, C_G = 4352, 4864, 4992, 5120
D_Q, D_K, D_V, D_G = 5632, 6144, 6656, 7168
M_Q, M_G = 7680, 7936
IN_COLS = 8192

TB = 128
TQ = 256
LOG2E = math.log2(math.e)
CHUNK = 64


def _dot(a, b):
    return jnp.dot(a, b, preferred_element_type=F32)


def _dot_nt(a, b):
    return lax.dot_general(a, b, (((1,), (1,)), ((), ())), preferred_element_type=F32)


def _silu(g):
    return g / (1.0 + jnp.exp(-g))


def _params(*sem):
    return pltpu.CompilerParams(dimension_semantics=sem, vmem_limit_bytes=VMEM_LIMIT)


def _ln_in_kernel(x_ref, g_ref, b_ref, h_ref, hb_ref):
    x = x_ref[...]
    mu = jnp.mean(x, -1, keepdims=True)
    d = x - mu
    var = jnp.mean(d * d, -1, keepdims=True)
    h = d * lax.rsqrt(var + 1e-5) * g_ref[...] + b_ref[...]
    h_ref[...] = h
    hb_ref[...] = h.astype(BF16)


def _ln_in(x2, g, b):
    n, d = x2.shape
    tm = 512
    return pl.pallas_call(
        _ln_in_kernel,
        grid=(n // tm,),
        in_specs=[pl.BlockSpec((tm, d), lambda i: (i, 0)),
                  pl.BlockSpec((1, d), lambda i: (0, 0)),
                  pl.BlockSpec((1, d), lambda i: (0, 0))],
        out_specs=[pl.BlockSpec((tm, d), lambda i: (i, 0)),
                   pl.BlockSpec((tm, d), lambda i: (i, 0))],
        out_shape=[jax.ShapeDtypeStruct((n, d), F32), jax.ShapeDtypeStruct((n, d), BF16)],
        compiler_params=_params("parallel"),
        name="ln_in",
    )(x2, g.reshape(1, d), b.reshape(1, d))


def _matmul_kernel(a_ref, w_ref, o_ref, wbf_ref):
    @pl.when(pl.program_id(1) == 0)
    def _():
        wbf_ref[...] = w_ref[...].astype(BF16)

    o_ref[...] = _dot(a_ref[...], wbf_ref[...])


def _matmul(a, w, layer, tm, tn, name):
    m, k = a.shape
    n = w.shape[2]
    return pl.pallas_call(
        _matmul_kernel,
        grid=(n // tn, m // tm),
        in_specs=[pl.BlockSpec((tm, k), lambda j, i: (i, 0)),
                  pl.BlockSpec((None, k, tn), lambda j, i: (layer, 0, j))],
        out_specs=pl.BlockSpec((tm, tn), lambda j, i: (i, j)),
        out_shape=jax.ShapeDtypeStruct((m, n), F32),
        scratch_shapes=[pltpu.VMEM((k, tn), BF16)],
        compiler_params=_params("parallel", "arbitrary"),
        name=name,
    )(a, w)


def _proj_in_kernel(a_ref, halo_ref, w_ref, mu_ref, o_ref, wbf_ref, sh_ref, *, shift_block, n_shift, sub):
    j = pl.program_id(0)
    i = pl.program_id(1)

    @pl.when(i == 0)
    def _():
        wbf_ref[...] = w_ref[...].astype(BF16)

    @pl.when(j != shift_block)
    def _():
        o_ref[...] = _dot(a_ref[...], wbf_ref[...])

    @pl.when(j == shift_block)
    def _():
        tm, tn = o_ref.shape
        first_half = (i % 2) == 0
        for cb in range(tn // sub):
            cols = slice(cb * sub, (cb + 1) * sub)
            x = _dot(a_ref[...], wbf_ref[:, cols])
            if cols.start >= n_shift:
                o_ref[:, cols] = x
                continue
            halo = _dot(halo_ref[...], wbf_ref[:, cols])
            prev_row = jnp.where(first_half, 0.0, halo[7:8])
            next_row = jnp.where(first_half, halo[0:1], 0.0)
            sh_ref[7:8, :] = prev_row
            sh_ref[8:tm + 8, :] = x
            sh_ref[tm + 8:tm + 9, :] = next_row
            prev = sh_ref[7:tm + 7, :]
            nxt = sh_ref[9:tm + 9, :]
            mu0 = mu_ref[0:1, cols]
            mu1 = mu_ref[1:2, cols]
            o_ref[:, cols] = x * (1.0 - mu0 - mu1) + mu0 * prev + mu1 * nxt


def _proj_in(a, w, layer, mu, seq):
    m, k = a.shape
    n = w.shape[2]
    tn = 2048
    tm = seq // 2
    assert B_R % tn == 0 and B_G < B_R + tn, "the RWKV mix columns must sit inside one column block"
    mu_pad = jnp.zeros((2, tn), F32).at[:, :mu.shape[1]].set(mu)
    halo_rows = tm // 8

    def halo_index(j, i):
        return ((i + 1 - i % 2) * halo_rows - i % 2, 0)

    return pl.pallas_call(
        functools.partial(_proj_in_kernel, shift_block=B_R // tn, n_shift=mu.shape[1], sub=256),
        grid=(n // tn, m // tm),
        in_specs=[pl.BlockSpec((tm, k), lambda j, i: (i, 0)),
                  pl.BlockSpec((8, k), halo_index),
                  pl.BlockSpec((None, k, tn), lambda j, i: (layer, 0, j)),
                  pl.BlockSpec((2, tn), lambda j, i: (0, 0))],
        out_specs=pl.BlockSpec((tm, tn), lambda j, i: (i, j)),
        out_shape=jax.ShapeDtypeStruct((m, n), F32),
        scratch_shapes=[pltpu.VMEM((k, tn), BF16), pltpu.VMEM((tm + 16, 256), F32)],
        compiler_params=_params("parallel", "arbitrary"),
        name="proj_in",
    )(a, a, w, mu_pad)


def _rel_bucket_np(rel):
    nb = NUM_BUCKETS // 2
    max_exact = nb // 2
    n = np.abs(rel)
    nf = np.maximum(n, 1).astype(np.float32)
    large = max_exact + (np.log(nf / np.float32(max_exact)) / np.float32(math.log(REL_MAX_DISTANCE / max_exact))
                         * np.float32(nb - max_exact)).astype(np.int32)
    large = np.minimum(large, nb - 1)
    return (np.where(rel > 0, nb, 0) + np.where(n < max_exact, n, large)).astype(np.int32)


def _tile_deltas(n_blk):
    d = np.arange(2 * n_blk - 1)[:, None, None] - (n_blk - 1)
    r = np.arange(TB)[None, :, None]
    c = np.arange(TB)[None, None, :]
    return d * TB + c - r


def _dilated_log_multiplicity(delta):
    mult = np.zeros(delta.shape, np.float32)
    for window, dil in A_GROUPS:
        mult += ((delta % dil == 0) & (np.abs(delta) <= window // 2)).astype(np.float32)
    with np.errstate(divide="ignore"):
        return np.where(mult > 0, np.log(np.maximum(mult, 1.0)), NEG_INF).astype(np.float32)


def _bias_kernel(table_ref, bucket_ref, base_ref, o_ref, *, n_a, tile_bucket):
    h = pl.program_id(0)
    for d, const in enumerate(tile_bucket):
        acc = jnp.where(h < n_a, base_ref[d], 0.0)
        if const is not None:
            acc = acc + table_ref[const, h]
        else:
            bucket = bucket_ref[d]
            for b in range(NUM_BUCKETS):
                acc = acc + jnp.where(bucket == b, table_ref[b, h], 0.0)
        o_ref[0, d] = acc * LOG2E


def _bias_tiles(rel_bias, n_blk):
    delta = _tile_deltas(n_blk)
    bucket_np = _rel_bucket_np(delta)
    tile_bucket = tuple(int(t.flat[0]) if (t == t.flat[0]).all() else None for t in bucket_np)
    bucket = jnp.asarray(bucket_np)
    base = jnp.asarray(_dilated_log_multiplicity(delta))
    n_heads = rel_bias.shape[1]
    nd = 2 * n_blk - 1
    return pl.pallas_call(
        functools.partial(_bias_kernel, n_a=A_HEADS, tile_bucket=tile_bucket),
        grid=(n_heads,),
        in_specs=[pl.BlockSpec(memory_space=pltpu.SMEM),
                  pl.BlockSpec((nd, TB, TB), lambda h: (0, 0, 0)),
                  pl.BlockSpec((nd, TB, TB), lambda h: (0, 0, 0))],
        out_specs=pl.BlockSpec((1, nd, TB, TB), lambda h: (h, 0, 0, 0)),
        out_shape=jax.ShapeDtypeStruct((n_heads, nd, TB, TB), F32),
        compiler_params=_params("arbitrary"),
        name="bias_tiles",
    )(rel_bias, bucket, base)


def _lane_lo(rows):
    return lax.broadcasted_iota(jnp.int32, (rows, LANES), 1) < HEAD


def _with_ones(v_bf):
    return jnp.concatenate([v_bf, jnp.ones(v_bf.shape, BF16)], axis=1)


def _store_logits(buf, hh, s, bias_fn=None):
    rows, keys = s.shape
    for rb in range(rows // TB):
        for j in range(keys // LANES):
            c = s[rb * TB:(rb + 1) * TB, j * LANES:(j + 1) * LANES]
            if bias_fn is not None:
                c = c + bias_fn(rb, j)
            buf[hh, rb * TB:(rb + 1) * TB, j * LANES:(j + 1) * LANES] = c


def _softmax_pv(buf, hh, v_ext):
    _, rows, keys = buf.shape
    es = []
    for rb in range(rows // TB):
        def chunk(j):
            return buf[hh, rb * TB:(rb + 1) * TB, j * LANES:(j + 1) * LANES]
        m = chunk(0)
        for j in range(1, keys // LANES):
            m = jnp.maximum(m, chunk(j))
        m = jnp.max(m, axis=1, keepdims=True)
        es.append(jnp.concatenate([jnp.exp2(chunk(j) - m).astype(BF16) for j in range(keys // LANES)], axis=1))
    o = _dot(jnp.concatenate(es, axis=0), v_ext)
    return o[:, :LANES] / o[:, LANES:]


def _attn_loop(n_q, qk_store, finish, s_a, s_b):
    qk_store(s_a, 0)

    def body(i, carry):
        q0 = 2 * i
        qk_store(s_b, q0 + 1)
        finish(s_a, q0)
        qk_store(s_a, q0 + 2)
        finish(s_b, q0 + 1)
        return carry

    lax.fori_loop(0, n_q // 2 - 1, body, 0)
    qk_store(s_b, n_q - 1)
    finish(s_a, n_q - 2)
    finish(s_b, n_q - 1)


def _logit_bufs(keys):
    return [pltpu.VMEM((2, TQ, keys), F32), pltpu.VMEM((2, TQ, keys), F32)]


def _seg_sum(x, lo):
    s0 = jnp.sum(jnp.where(lo, x, 0.0), axis=1, keepdims=True)
    s1 = jnp.sum(jnp.where(lo, 0.0, x), axis=1, keepdims=True)
    return jnp.where(lo, s0, s1)


def _mixer_a_kernel(q_ref, k_ref, v_ref, g_ref, bias_ref, o_ref, s_a, s_b, *, n_blk):
    k_bf = k_ref[0].astype(BF16)
    v_ext = _with_ones(v_ref[0].astype(BF16))
    lo = _lane_lo(TQ)
    rpb = TQ // TB

    def qk_store(buf, qi):
        r0 = pl.multiple_of(qi * TQ, TQ)
        q = q_ref[0, pl.ds(r0, TQ), :] * (HEAD ** -0.5 * LOG2E)
        for hh in range(2):
            qm = jnp.where(lo if hh == 0 else jnp.logical_not(lo), q, 0.0).astype(BF16)
            _store_logits(buf, hh, _dot_nt(qm, k_bf),
                          lambda rb, j, hh=hh: bias_ref[hh, n_blk - 1 - (qi * rpb + rb) + j])

    def finish(buf, qi):
        r0 = pl.multiple_of(qi * TQ, TQ)
        o = jnp.where(lo, _softmax_pv(buf, 0, v_ext), _softmax_pv(buf, 1, v_ext))
        o_ref[0, pl.ds(r0, TQ), :] = o * _silu(g_ref[0, pl.ds(r0, TQ), :])

    _attn_loop(n_blk // rpb, qk_store, finish, s_a, s_b)


def _mixer_a(p, bias):
    b, s, _ = p.shape
    n_blk = s // TB
    nd = 2 * n_blk - 1
    n_pairs = A_HEADS // 2

    def col(off):
        return pl.BlockSpec((1, s, LANES), lambda bi, j, off=off: (bi, 0, off // LANES + j))

    return pl.pallas_call(
        functools.partial(_mixer_a_kernel, n_blk=n_blk),
        grid=(b, n_pairs),
        in_specs=[col(A_Q), col(A_K), col(A_V), col(A_G),
                  pl.BlockSpec((2, nd, TB, TB), lambda bi, j: (j, 0, 0, 0))],
        out_specs=pl.BlockSpec((1, s, LANES), lambda bi, j: (bi, 0, j)),
        out_shape=jax.ShapeDtypeStruct((b, s, n_pairs * LANES), F32),
        scratch_shapes=_logit_bufs(s),
        compiler_params=_params("parallel", "parallel"),
        name="mixer_a",
    )(p, p, p, p, bias)


def _mixer_d_kernel(q_ref, k_ref, v_ref, g_ref, bias_ref, lam_ref, sg_ref, o_ref, s_a, s_b, *, n_blk, lam_init):
    k_bf = k_ref[0].astype(BF16)
    v_ext = _with_ones(v_ref[0].astype(BF16))
    lo = _lane_lo(TQ)
    rpb = TQ // TB
    dl = lam_ref[...]
    lam = (jnp.exp(jnp.sum(dl[0:1] * dl[1:2], axis=1, keepdims=True))
           - jnp.exp(jnp.sum(dl[2:3] * dl[3:4], axis=1, keepdims=True)) + lam_init)
    sg = sg_ref[...] * (1.0 - lam_init)

    def qk_store(buf, qi):
        r0 = pl.multiple_of(qi * TQ, TQ)
        q = q_ref[0, pl.ds(r0, TQ), :] * (HEAD ** -0.5 * LOG2E)
        for hh in range(2):
            qm = jnp.where(lo if hh == 0 else jnp.logical_not(lo), q, 0.0).astype(BF16)
            _store_logits(buf, hh, _dot_nt(qm, k_bf),
                          lambda rb, j: bias_ref[0, n_blk - 1 - (qi * rpb + rb) + j])

    def finish(buf, qi):
        r0 = pl.multiple_of(qi * TQ, TQ)
        o = _softmax_pv(buf, 0, v_ext) - lam * _softmax_pv(buf, 1, v_ext)
        o = o * lax.rsqrt(jnp.mean(o * o, axis=1, keepdims=True) + 1e-5) * sg
        o_ref[0, pl.ds(r0, TQ), :] = o * _silu(g_ref[0, pl.ds(r0, TQ), :])

    _attn_loop(n_blk // rpb, qk_store, finish, s_a, s_b)


def _mixer_d(p, bias, d_lambda, subln_g, layer_idx):
    b, s, _ = p.shape
    n_blk = s // TB
    nd = 2 * n_blk - 1
    lam_init = 0.8 - 0.6 * math.exp(-0.3 * layer_idx)

    def col(off):
        return pl.BlockSpec((1, s, LANES), lambda bi, j, off=off: (bi, 0, off // LANES + j))

    return pl.pallas_call(
        functools.partial(_mixer_d_kernel, n_blk=n_blk, lam_init=lam_init),
        grid=(b, D_HEADS),
        in_specs=[col(D_Q), col(D_K), col(D_V), col(D_G),
                  pl.BlockSpec((1, nd, TB, TB), lambda bi, j: (A_HEADS + j, 0, 0, 0)),
                  pl.BlockSpec((4, HEAD), lambda bi, j: (0, 0)),
                  pl.BlockSpec((1, LANES), lambda bi, j: (0, 0))],
        out_specs=pl.BlockSpec((1, s, LANES), lambda bi, j: (bi, 0, j)),
        out_shape=jax.ShapeDtypeStruct((b, s, D_HEADS * LANES), F32),
        scratch_shapes=_logit_bufs(s),
        compiler_params=_params("parallel", "parallel"),
        name="mixer_d",
    )(p, p, p, p, bias, d_lambda, subln_g.reshape(1, LANES))


def _rope_tables(s):
    t = np.arange(s)
    row, colp = t // GRID_W, t % GRID_W
    qtr = HEAD // 4
    freqs = np.float32(ROPE_THETA) ** (-(np.arange(qtr, dtype=np.float32) / np.float32(qtr)))
    lane = np.arange(LANES) % HEAD
    use_col = (lane // (HEAD // 2)) == 1
    second = (lane % (HEAD // 2)) >= qtr
    pos = np.where(use_col[None, :], colp[:, None], row[:, None]).astype(np.float32)
    ang = (pos * freqs[lane % qtr][None, :]).astype(np.float32)
    cos = np.cos(ang).astype(np.float32)
    sin = np.where(second[None, :], np.sin(ang), -np.sin(ang)).astype(np.float32)
    return jnp.asarray(cos), jnp.asarray(sin)


def _norm_rope(x, gain, cos, sin, lo, first):
    ms = _seg_sum(x * x, lo) * (1.0 / HEAD)
    x = x * lax.rsqrt(ms + 1e-6) * gain
    qtr = HEAD // 4
    partner = jnp.where(first, pltpu.roll(x, LANES - qtr, axis=1), pltpu.roll(x, qtr, axis=1))
    return x * cos + partner * sin


def _mixer_c_kernel(q_ref, k_ref, v_ref, g_ref, cos_ref, sin_ref, qg_ref, kg_ref, o_ref, s_a, s_b, krot_ref,
                    *, n_blk, s):
    pair = pl.program_id(1)
    grp = pair // 2
    lo_s = _lane_lo(s)

    @pl.when(pair == 0)
    def _():
        lane_s = lax.broadcasted_iota(jnp.int32, (s, LANES), 1)
        first_s = (lane_s % (HEAD // 2)) < (HEAD // 4)
        krot_ref[...] = _norm_rope(k_ref[0], kg_ref[...], cos_ref[...], sin_ref[...], lo_s, first_s)

    k = krot_ref[...]
    v = v_ref[0]
    k_sw = pltpu.roll(k, HEAD, axis=1)
    v_sw = pltpu.roll(v, HEAD, axis=1)
    keep = jnp.logical_xor(lo_s, grp == 1)
    k_bf = jnp.where(keep, k, k_sw).astype(BF16)
    v_ext = _with_ones(jnp.where(keep, v, v_sw).astype(BF16))
    lo = _lane_lo(TQ)
    lane = lax.broadcasted_iota(jnp.int32, (TQ, LANES), 1)
    first = (lane % (HEAD // 2)) < (HEAD // 4)

    def qk_store(buf, qi):
        r0 = pl.multiple_of(qi * TQ, TQ)
        q = _norm_rope(q_ref[0, pl.ds(r0, TQ), :], qg_ref[...], cos_ref[pl.ds(r0, TQ), :],
                       sin_ref[pl.ds(r0, TQ), :], lo, first) * (HEAD ** -0.5 * LOG2E)
        for hh in range(2):
            qm = jnp.where(lo if hh == 0 else jnp.logical_not(lo), q, 0.0).astype(BF16)
            _store_logits(buf, hh, _dot_nt(qm, k_bf))

    def finish(buf, qi):
        r0 = pl.multiple_of(qi * TQ, TQ)
        o = jnp.where(lo, _softmax_pv(buf, 0, v_ext), _softmax_pv(buf, 1, v_ext))
        o_ref[0, pl.ds(r0, TQ), :] = o * _silu(g_ref[0, pl.ds(r0, TQ), :])

    _attn_loop(n_blk, qk_store, finish, s_a, s_b)


def _mixer_c(p, cos, sin, qn_g, kn_g):
    b, s, _ = p.shape
    n_blk = s // TQ
    n_pairs = 4

    def col(off):
        return pl.BlockSpec((1, s, LANES), lambda bi, j, off=off: (bi, 0, off // LANES + j))

    def fixed(off):
        return pl.BlockSpec((1, s, LANES), lambda bi, j, off=off: (bi, 0, off // LANES))

    tab = pl.BlockSpec((s, LANES), lambda bi, j: (0, 0))
    gain = pl.BlockSpec((1, LANES), lambda bi, j: (0, 0))
    return pl.pallas_call(
        functools.partial(_mixer_c_kernel, n_blk=n_blk, s=s),
        grid=(b, n_pairs),
        in_specs=[col(C_Q), fixed(C_K), fixed(C_V), col(C_G), tab, tab, gain, gain],
        out_specs=pl.BlockSpec((1, s, LANES), lambda bi, j: (bi, 0, j)),
        out_shape=jax.ShapeDtypeStruct((b, s, n_pairs * LANES), F32),
        scratch_shapes=_logit_bufs(s) + [pltpu.VMEM((s, LANES), F32)],
        compiler_params=_params("parallel", "arbitrary"),
        name="mixer_c",
    )(p, p, p, p, cos, sin, jnp.tile(qn_g, 2).reshape(1, LANES), jnp.tile(kn_g, 2).reshape(1, LANES))


def _mixer_m_kernel(q_ref, g_ref, k_ref, v_ref, o_ref, s_a, s_b, *, n_blk):
    k_bf = k_ref[0].astype(BF16)
    v_ext = _with_ones(v_ref[0].astype(BF16))
    lo = _lane_lo(TQ)

    def qk_store(buf, qi):
        r0 = pl.multiple_of(qi * TQ, TQ)
        q = q_ref[0, pl.ds(r0, TQ), :] * (HEAD ** -0.5 * LOG2E)
        for hh in range(2):
            qm = jnp.where(lo if hh == 0 else jnp.logical_not(lo), q, 0.0).astype(BF16)
            _store_logits(buf, hh, _dot_nt(qm, k_bf))

    def finish(buf, qi):
        r0 = pl.multiple_of(qi * TQ, TQ)
        o = jnp.where(lo, _softmax_pv(buf, 0, v_ext), _softmax_pv(buf, 1, v_ext))
        o_ref[0, pl.ds(r0, TQ), :] = o * _silu(g_ref[0, pl.ds(r0, TQ), :])

    _attn_loop(n_blk, qk_store, finish, s_a, s_b)


def _mixer_m(p, kv):
    b, s, _ = p.shape
    n_mem = kv.shape[1]
    n_blk = s // TQ
    n_pairs = 2

    def col(off):
        return pl.BlockSpec((1, s, LANES), lambda bi, j, off=off: (bi, 0, off // LANES + j))

    return pl.pallas_call(
        functools.partial(_mixer_m_kernel, n_blk=n_blk),
        grid=(b, n_pairs),
        in_specs=[col(M_Q), col(M_G),
                  pl.BlockSpec((1, n_mem, LANES), lambda bi, j: (bi, 0, j)),
                  pl.BlockSpec((1, n_mem, LANES), lambda bi, j: (bi, 0, n_pairs + j))],
        out_specs=pl.BlockSpec((1, s, LANES), lambda bi, j: (bi, 0, j)),
        out_shape=jax.ShapeDtypeStruct((b, s, n_pairs * LANES), F32),
        scratch_shapes=_logit_bufs(n_mem),
        compiler_params=_params("parallel", "parallel"),
        name="mixer_m",
    )(p, p, kv, kv)


def _split3_dot(tri_bf, x):
    h1 = x.astype(BF16)
    r1 = x - h1.astype(F32)
    h2 = r1.astype(BF16)
    h3 = (r1 - h2.astype(F32)).astype(BF16)
    return _dot(tri_bf, h1) + _dot(tri_bf, h2) + _dot(tri_bf, h3)


def _rwkv_kernel(r_ref, k_ref, v_ref, wa_ref, w0_ref, wup_ref, a0_ref, aup_ref,
                 kk_ref, ka_ref, rk_ref, yf_ref, yb_ref, bvf_ref, bvb_ref, state_ref, *, nb, nch):
    C = CHUNK
    R = nch * C
    c = pl.program_id(1)

    @pl.when(c == 0)
    def _():
        state_ref[...] = jnp.zeros_like(state_ref)

    n_pairs = r_ref.shape[2] // LANES
    width = r_ref.shape[2]
    tt = lax.broadcasted_iota(jnp.int32, (C, LANES), 0)
    ss = lax.broadcasted_iota(jnp.int32, (C, LANES), 1) % C
    eye_bf = jnp.where(tt == ss, 1.0, 0.0).astype(BF16)
    rr = lax.broadcasted_iota(jnp.int32, (LANES, LANES), 0)
    cc = lax.broadcasted_iota(jnp.int32, (LANES, LANES), 1)
    eye = rr == cc
    tr = lax.broadcasted_iota(jnp.int32, (R, R), 0)
    tc = lax.broadcasted_iota(jnp.int32, (R, R), 1)
    same_chunk = (tr // C) == (tc // C)
    lo = _lane_lo(C)
    hi = jnp.logical_not(lo)
    lo_bf = jnp.where(lo, 1.0, 0.0).astype(BF16)
    hi_bf = jnp.where(hi, 1.0, 0.0).astype(BF16)
    lo_r = _lane_lo(R)

    def stack(x):
        return jnp.concatenate([jnp.where(lo, x, 0.0), jnp.where(hi, x, 0.0)], axis=0)

    def stack_bf(x):
        return jnp.concatenate([x * lo_bf, x * hi_bf], axis=0)

    units = []
    for e in range(2):
        rev = e == 1
        blk = (nb - 1 - c) if rev else c
        r0 = pl.multiple_of(blk * R, R)
        bv_ref = bvb_ref if rev else bvf_ref
        strict = (tt < ss) if rev else (tt > ss)
        incl = (tt <= ss) if rev else (tt >= ss)
        tri_bf = jnp.where(jnp.logical_and(same_chunk, (tr <= tc) if rev else (tr >= tc)), 1.0, 0.0).astype(BF16)
        last = 0 if rev else C - 1
        up, dn = (ss, tt) if rev else (tt, ss)
        level_masks = []
        m = 1
        while m < C:
            same_blk = ((tt ^ ss) >> (int(math.log2(m)) + 1)) == 0
            lvl_mask = jnp.logical_and(same_blk, jnp.logical_and((up & m) != 0, (dn & m) == 0))
            level_masks.append(jnp.where(lvl_mask, 1.0, 0.0).astype(BF16))
            m *= 2

        rs = r_ref[0, pl.ds(r0, R), :]
        ks = k_ref[0, pl.ds(r0, R), :]
        vs = v_ref[0, pl.ds(r0, R), :]
        wa = wa_ref[0, pl.ds(r0, R), :]
        sel = lo_r if e == 0 else jnp.logical_not(lo_r)
        wd = jnp.where(sel, jnp.tanh(wa[:, :LANES]), 0.0).astype(BF16)
        ad = jnp.where(sel, wa[:, LANES:], 0.0).astype(BF16)
        zw = w0_ref[e:e + 1, :] + _dot(wd, wup_ref[...])
        lw = -math.exp(-0.5) / (1.0 + jnp.exp(-zw))
        za = a0_ref[e:e + 1, :] + _dot(ad, aup_ref[...])
        a_sig = 1.0 / (1.0 + jnp.exp(-za))
        kk = ks * kk_ref[...]
        kk2 = kk * kk
        nrm2 = jnp.concatenate([_seg_sum(kk2[:, i * LANES:(i + 1) * LANES], lo_r) for i in range(n_pairs)], axis=1)
        kkn = kk / jnp.maximum(jnp.sqrt(nrm2), 1e-12)
        ka = ka_ref[...]
        ke = ks * ((1.0 - ka) + ka * a_sig)
        be = kkn * a_sig
        rkr = rs * ke * rk_ref[...]
        bonus = jnp.concatenate([_seg_sum(rkr[:, i * LANES:(i + 1) * LANES], lo_r) for i in range(n_pairs)], axis=1)
        bv_ref[0] = bonus * vs

        l_incl = _split3_dot(tri_bf, lw)
        l_tot = jnp.concatenate(
            [jnp.broadcast_to(l_incl[j * C + last:j * C + last + 1, :], (C, width)) for j in range(nch)], axis=0)
        w_inv = jnp.exp(-l_incl)
        d_end = jnp.exp(l_tot)
        w_end = d_end * w_inv
        at = -kkn * jnp.exp(l_incl - lw)
        rt = rs * jnp.exp(l_incl)
        bt = be * w_inv
        kt = ke * w_inv
        bh = be * w_end
        kh = ke * w_end
        for j in range(nch):
            rows = slice(j * C, (j + 1) * C)
            for pr in range(n_pairs):
                sl = slice(pr * LANES, (pr + 1) * LANES)
                units.append(dict(
                    e=e, j=j, pr=pr, strict=strict, incl=incl, levels=level_masks,
                    at=at[rows, sl], rt=rt[rows, sl], bt=bt[rows, sl], kt=kt[rows, sl],
                    bh=bh[rows, sl], kh=kh[rows, sl], v=vs[rows, sl], d_end=d_end[j * C:j * C + 1, sl]))

    for u in units:
        u["at_bf"] = u["at"].astype(BF16)
        u["v_st"] = stack_bf(u["v"].astype(BF16))
        lhs = jnp.concatenate([u["at_bf"], u["rt"].astype(BF16)], axis=0)
        rhs = jnp.concatenate([stack_bf(u["bt"].astype(BF16)), stack_bf(u["kt"].astype(BF16))], axis=0)
        g = _dot_nt(lhs, rhs)
        u["a_ab"] = jnp.where(u["strict"], g[:C, :LANES], 0.0).astype(BF16)
        u["a_ak"] = jnp.where(u["strict"], g[:C, LANES:], 0.0).astype(BF16)
        u["ly"] = jnp.concatenate([jnp.where(u["incl"], g[C:, :LANES], 0.0),
                                   jnp.where(u["incl"], g[C:, LANES:], 0.0)], axis=1).astype(BF16)
        u["t"] = eye_bf + u["a_ab"] * u["levels"][0]
    for lvl in range(1, len(units[0]["levels"])):
        for u in units:
            u["tmp"] = _dot(u["a_ab"] * u["levels"][lvl], stack_bf(u["t"])).astype(BF16)
        for u in units:
            u["t"] = u["t"] + _dot(u["t"], stack_bf(u["tmp"])).astype(BF16)
    for u in units:
        u["akv"] = _dot(u["a_ak"], u["v_st"]).astype(BF16)
    for u in units:
        x = jnp.concatenate([stack_bf(u["at_bf"]), stack_bf(u["akv"])], axis=1)
        pq = _dot(u["t"], x).astype(BF16)
        u["ry"] = jnp.concatenate(
            [jnp.concatenate([stack_bf(pq[:, :LANES]), stack_bf(pq[:, LANES:])], axis=1),
             jnp.concatenate([jnp.zeros((LANES, LANES), BF16), u["v_st"]], axis=1)], axis=0)
        ls = jnp.concatenate([stack(u["bh"]), stack(u["kh"])], axis=0)
        u["lyz"] = jnp.concatenate([u["ly"], ls.T.astype(BF16)], axis=0)
    for u in units:
        yz = _dot(u["lyz"], u["ry"])
        yy = yz[:C]
        zz = yz[C:]
        u["rpm"] = jnp.concatenate([u["rt"] + yy[:, :LANES], zz[:, :LANES]], axis=0).astype(BF16)
        u["y0"] = yy[:, LANES:]
        u["z"] = zz[:, LANES:]
        d_diag = jnp.where(eye, jnp.broadcast_to(u["d_end"], (LANES, LANES)), 0.0)
        u["d_col"] = jnp.sum(d_diag, axis=1, keepdims=True)

    by_key = {(u["e"], u["j"], u["pr"]): u for u in units}
    for e in range(2):
        y_ref = yb_ref if e == 1 else yf_ref
        order = list(range(nch))[::-1] if e == 1 else list(range(nch))
        ys = {}
        states = [state_ref[e, pr] for pr in range(n_pairs)]
        for j in order:
            for pr in range(n_pairs):
                u = by_key[(e, j, pr)]
                ym = _dot(u["rpm"], states[pr].astype(BF16))
                ys[(j, pr)] = ym[:C] + u["y0"]
                states[pr] = u["d_col"] * states[pr] + ym[C:] + u["z"]
        for pr in range(n_pairs):
            state_ref[e, pr] = states[pr]
        y_ref[0] = jnp.concatenate(
            [jnp.concatenate([ys[(j, pr)] for pr in range(n_pairs)], axis=1) for j in range(nch)], axis=0)


RWKV_CHUNKS_PER_STEP = 2


def _mixer_b_scan(p, w0, w_up, a0, a_up, k_k, k_a, r_k):
    b, s, _ = p.shape
    width = 512
    nch = RWKV_CHUNKS_PER_STEP
    rows = nch * CHUNK
    nb = s // rows
    n_pairs = width // LANES

    def seq(off, w):
        return pl.BlockSpec((1, s, w), lambda bi, c, off=off, w=w: (bi, 0, off // w))

    def full(shape):
        return pl.BlockSpec(shape, lambda bi, c: tuple(0 for _ in shape))

    out_f = pl.BlockSpec((1, rows, width), lambda bi, c: (bi, c, 0))
    out_b = pl.BlockSpec((1, rows, width), lambda bi, c: (bi, nb - 1 - c, 0))
    sds = jax.ShapeDtypeStruct((b, s, width), F32)
    lora = 2 * HEAD
    return pl.pallas_call(
        functools.partial(_rwkv_kernel, nb=nb, nch=nch),
        grid=(b, nb),
        in_specs=[seq(B_R, width), seq(B_K, width), seq(B_V, width), seq(B_WA, 2 * LANES),
                  full((2, width)), full((lora, width)),
                  full((2, width)), full((lora, width)), full((1, width)), full((1, width)), full((1, width))],
        out_specs=[out_f, out_b, out_f, out_b],
        out_shape=[sds, sds, sds, sds],
        scratch_shapes=[pltpu.VMEM((2, n_pairs, LANES, LANES), F32)],
        compiler_params=_params("parallel", "arbitrary"),
        name="mixer_b",
    )(p, p, p, p, w0, w_up.reshape(lora, width).astype(BF16), a0, a_up.reshape(lora, width).astype(BF16),
      k_k.reshape(1, width), k_a.reshape(1, width), r_k.reshape(1, width))


def _merge_kernel(h_ref, hb_ref, oa_ref, yf_ref, yb_ref, bvf_ref, bvb_ref, gb0_ref, gb1_ref, oc_ref, od_ref,
                  om_ref, wg_ref, bg_ref, wb_ref, wo_ref, lng_ref, lnb_ref, bg_g_ref, bg_b_ref,
                  hn_ref, hnb_ref, *, alpha):
    d = h_ref.shape[1]
    tm = h_ref.shape[0]
    lo = _lane_lo(tm)
    y = yf_ref[...] + yb_ref[...]
    n_pairs = y.shape[1] // LANES
    gn = []
    for i in range(n_pairs):
        yp = y[:, i * LANES:(i + 1) * LANES]
        mu = _seg_sum(yp, lo) * (1.0 / HEAD)
        dy = yp - mu
        var = _seg_sum(dy * dy, lo) * (1.0 / HEAD)
        gn.append(dy * lax.rsqrt(var + B_GN_EPS))
    gn = jnp.concatenate(gn, axis=1) * bg_g_ref[...] + bg_b_ref[...]
    gb = jnp.concatenate([gb0_ref[...], gb1_ref[...]], axis=1)
    ob = (gn + bvf_ref[...] + bvb_ref[...]) * _silu(gb)

    hb = hb_ref[...]
    branches = (oa_ref[...], ob, oc_ref[...], od_ref[...], om_ref[...])
    acc = None
    row = 0
    for i, o in enumerate(branches):
        wdt = o.shape[1]
        gate = 1.0 / (1.0 + jnp.exp(-(_dot(hb, wg_ref[:, i * d:(i + 1) * d]) + bg_ref[:, i * d:(i + 1) * d])))
        proj = _dot(o.astype(BF16), wb_ref[row:row + wdt, :])
        term = gate * proj
        acc = term if acc is None else acc + term
        row += wdt
    out = _dot(acc.astype(BF16), wo_ref[...])
    z = alpha * h_ref[...] + out
    mu = jnp.mean(z, -1, keepdims=True)
    dz = z - mu
    var = jnp.mean(dz * dz, -1, keepdims=True)
    hn = dz * lax.rsqrt(var + 1e-5) * lng_ref[...] + lnb_ref[...]
    hn_ref[...] = hn
    hnb_ref[...] = hn.astype(BF16)


def _merge(h, hb, o_a, yf, yb, bvf, bvb, p2, o_c, o_d, o_m, layer, w_gate, b_gate, w_branch, w_out, ln_g, ln_b,
           bln_g, bln_b):
    n, d = h.shape
    tm = 256
    alpha = (2 * DEPTH) ** 0.25

    def rows(w):
        return pl.BlockSpec((tm, w), lambda i: (i, 0))

    def full(shape):
        return pl.BlockSpec(shape, lambda i: tuple(0 for _ in shape))

    def of_layer(w):
        return pl.BlockSpec((None,) + w.shape[1:], lambda i: (layer, 0, 0))

    gb_specs = pl.BlockSpec((tm, 256), lambda i: (i, B_G // 256))
    gb_specs2 = pl.BlockSpec((tm, 256), lambda i: (i, B_G // 256 + 1))
    return pl.pallas_call(
        functools.partial(_merge_kernel, alpha=alpha),
        grid=(n // tm,),
        in_specs=[rows(d), rows(d), rows(512), rows(512), rows(512), rows(512), rows(512),
                  gb_specs, gb_specs2, rows(512), rows(512), rows(256),
                  of_layer(w_gate), full((1, b_gate.shape[0])), of_layer(w_branch), of_layer(w_out),
                  full((1, d)), full((1, d)), full((1, 512)), full((1, 512))],
        out_specs=[rows(d), rows(d)],
        out_shape=[jax.ShapeDtypeStruct((n, d), F32), jax.ShapeDtypeStruct((n, d), BF16)],
        compiler_params=_params("parallel"),
        name="merge",
    )(h, hb, o_a, yf, yb, bvf, bvb, p2, p2, o_c, o_d, o_m,
      w_gate, b_gate.reshape(1, -1), w_branch, w_out, ln_g.reshape(1, d), ln_b.reshape(1, d),
      bln_g.reshape(1, 512), bln_b.reshape(1, 512))


def kernel(x, mem, ln_in_g, ln_in_b, rel_bias, w_in, shift_mu, rwkv_w0, rwkv_w_up, rwkv_a0, rwkv_a_up,
           rwkv_k_k, rwkv_k_a, rwkv_r_k, rwkv_ln_g, rwkv_ln_b, c_qnorm_g, c_knorm_g, d_lambda, d_subln_g,
           w_mem_kv, w_branch, w_gate, b_gate, w_out, ln_g, ln_b):
    b, s, d = x.shape
    n = b * s
    bias = _bias_tiles(rel_bias, s // TB)
    cos, sin = _rope_tables(s)
    mem_bf = mem.reshape(b * mem.shape[1], d).astype(BF16)
    h, hb = _ln_in(x.reshape(n, d), ln_in_g, ln_in_b)
    w_gate, w_branch, w_out = (w.astype(BF16) for w in (w_gate, w_branch, w_out))
    for l in range(DEPTH):
        p2 = _proj_in(hb, w_in, l, shift_mu[l], s)
        p = p2.reshape(b, s, IN_COLS)
        kv = _matmul(mem_bf, w_mem_kv, l, 512, 512, "proj_mem").reshape(b, mem.shape[1], -1)
        o_a = _mixer_a(p, bias)
        yf, yb, bvf, bvb = _mixer_b_scan(p, rwkv_w0[l], rwkv_w_up[l], rwkv_a0[l], rwkv_a_up[l],
                                         rwkv_k_k[l], rwkv_k_a[l], rwkv_r_k[l])
        o_c = _mixer_c(p, cos, sin, c_qnorm_g[l], c_knorm_g[l])
        o_d = _mixer_d(p, bias, d_lambda[l], d_subln_g[l], l)
        o_m = _mixer_m(p, kv)
        flat = lambda t: t.reshape(n, t.shape[-1])
        h, hb = _merge(h, hb, flat(o_a), flat(yf), flat(yb), flat(bvf), flat(bvb), p2, flat(o_c), flat(o_d),
                       flat(o_m), l, w_gate, b_gate[l], w_branch, w_out, ln_g[l], ln_b[l],
                       rwkv_ln_g[l], rwkv_ln_b[l])
    return h.reshape(b, s, d)
```

```python
import functools
import math

import numpy as np
import jax
import jax.numpy as jnp
from jax import lax
from jax.experimental import pallas as pl
from jax.experimental.pallas import tpu as pltpu

F32 = jnp.float32
BF16 = jnp.bfloat16

LANES = 128
HEAD = 64
VMEM_LIMIT = 56 * 1024 * 1024

D_MODEL = 1024
DEPTH = 2
GRID_W = 64
ROPE_THETA = 10000.0
NUM_BUCKETS = 32
REL_MAX_DISTANCE = 1024
A_HEADS = 8
D_HEADS = 4
A_GROUPS = ((128, 1), (512, 4), (2048, 16))
B_GN_EPS = 64e-5
NEG_INF = -1e30

A_Q, A_K, A_V, A_G = 0, 512, 1024, 1536
B_R, B_K, B_V, B_WA, B_G = 2048, 2560, 3072, 3584, 3840
C_Q, C_K, C_V, C_G = 4352, 4864, 4992, 5120
D_Q, D_K, D_V, D_G = 5632, 6144, 6656, 7168
M_Q, M_G = 7680, 7936
IN_COLS = 8192

TB = 128
TQ = 256
LOG2E = math.log2(math.e)
CHUNK = 64


def _dot(a, b):
    return jnp.dot(a, b, preferred_element_type=F32)


def _dot_nt(a, b):
    return lax.dot_general(a, b, (((1,), (1,)), ((), ())), preferred_element_type=F32)


def _silu(g):
    return g / (1.0 + jnp.exp(-g))


def _params(*sem):
    return pltpu.CompilerParams(dimension_semantics=sem, vmem_limit_bytes=VMEM_LIMIT)


def _ln_in_kernel(x_ref, g_ref, b_ref, h_ref, hb_ref):
    x = x_ref[...]
    mu = jnp.mean(x, -1, keepdims=True)
    d = x - mu
    var = jnp.mean(d * d, -1, keepdims=True)
    h = d * lax.rsqrt(var + 1e-5) * g_ref[...] + b_ref[...]
    h_ref[...] = h
    hb_ref[...] = h.astype(BF16)


def _ln_in(x2, g, b):
    n, d = x2.shape
    tm = 512
    return pl.pallas_call(
        _ln_in_kernel,
        grid=(n // tm,),
        in_specs=[pl.BlockSpec((tm, d), lambda i: (i, 0)),
                  pl.BlockSpec((1, d), lambda i: (0, 0)),
                  pl.BlockSpec((1, d), lambda i: (0, 0))],
        out_specs=[pl.BlockSpec((tm, d), lambda i: (i, 0)),
                   pl.BlockSpec((tm, d), lambda i: (i, 0))],
        out_shape=[jax.ShapeDtypeStruct((n, d), F32), jax.ShapeDtypeStruct((n, d), BF16)],
        compiler_params=_params("parallel"),
        name="ln_in",
    )(x2, g.reshape(1, d), b.reshape(1, d))


def _matmul_kernel(a_ref, w_ref, o_ref, wbf_ref):
    @pl.when(pl.program_id(1) == 0)
    def _():
        wbf_ref[...] = w_ref[...].astype(BF16)

    o_ref[...] = _dot(a_ref[...], wbf_ref[...])


def _matmul(a, w, layer, tm, tn, name):
    m, k = a.shape
    n = w.shape[2]
    return pl.pallas_call(
        _matmul_kernel,
        grid=(n // tn, m // tm),
        in_specs=[pl.BlockSpec((tm, k), lambda j, i: (i, 0)),
                  pl.BlockSpec((None, k, tn), lambda j, i: (layer, 0, j))],
        out_specs=pl.BlockSpec((tm, tn), lambda j, i: (i, j)),
        out_shape=jax.ShapeDtypeStruct((m, n), F32),
        scratch_shapes=[pltpu.VMEM((k, tn), BF16)],
        compiler_params=_params("parallel", "arbitrary"),
        name=name,
    )(a, w)


def _proj_in_kernel(a_ref, halo_ref, w_ref, mu_ref, o_ref, wbf_ref, sh_ref, *, shift_block, n_shift, sub):
    j = pl.program_id(0)
    i = pl.program_id(1)

    @pl.when(i == 0)
    def _():
        wbf_ref[...] = w_ref[...].astype(BF16)

    @pl.when(j != shift_block)
    def _():
        o_ref[...] = _dot(a_ref[...], wbf_ref[...])

    @pl.when(j == shift_block)
    def _():
        tm, tn = o_ref.shape
        first_half = (i % 2) == 0
        for cb in range(tn // sub):
            cols = slice(cb * sub, (cb + 1) * sub)
            x = _dot(a_ref[...], wbf_ref[:, cols])
            if cols.start >= n_shift:
                o_ref[:, cols] = x
                continue
            halo = _dot(halo_ref[...], wbf_ref[:, cols])
            prev_row = jnp.where(first_half, 0.0, halo[7:8])
            next_row = jnp.where(first_half, halo[0:1], 0.0)
            sh_ref[7:8, :] = prev_row
            sh_ref[8:tm + 8, :] = x
            sh_ref[tm + 8:tm + 9, :] = next_row
            prev = sh_ref[7:tm + 7, :]
            nxt = sh_ref[9:tm + 9, :]
            mu0 = mu_ref[0:1, cols]
            mu1 = mu_ref[1:2, cols]
            o_ref[:, cols] = x * (1.0 - mu0 - mu1) + mu0 * prev + mu1 * nxt


def _proj_in(a, w, layer, mu, seq):
    m, k = a.shape
    n = w.shape[2]
    tn = 2048
    tm = seq // 2
    assert B_R % tn == 0 and B_G < B_R + tn, "the RWKV mix columns must sit inside one column block"
    mu_pad = jnp.zeros((2, tn), F32).at[:, :mu.shape[1]].set(mu)
    halo_rows = tm // 8

    def halo_index(j, i):
        return ((i + 1 - i % 2) * halo_rows - i % 2, 0)

    return pl.pallas_call(
        functools.partial(_proj_in_kernel, shift_block=B_R // tn, n_shift=mu.shape[1], sub=256),
        grid=(n // tn, m // tm),
        in_specs=[pl.BlockSpec((tm, k), lambda j, i: (i, 0)),
                  pl.BlockSpec((8, k), halo_index),
                  pl.BlockSpec((None, k, tn), lambda j, i: (layer, 0, j)),
                  pl.BlockSpec((2, tn), lambda j, i: (0, 0))],
        out_specs=pl.BlockSpec((tm, tn), lambda j, i: (i, j)),
        out_shape=jax.ShapeDtypeStruct((m, n), F32),
        scratch_shapes=[pltpu.VMEM((k, tn), BF16), pltpu.VMEM((tm + 16, 256), F32)],
        compiler_params=_params("parallel", "arbitrary"),
        name="proj_in",
    )(a, a, w, mu_pad)


def _rel_bucket_np(rel):
    nb = NUM_BUCKETS // 2
    max_exact = nb // 2
    n = np.abs(rel)
    nf = np.maximum(n, 1).astype(np.float32)
    large = max_exact + (np.log(nf / np.float32(max_exact)) / np.float32(math.log(REL_MAX_DISTANCE / max_exact))
                         * np.float32(nb - max_exact)).astype(np.int32)
    large = np.minimum(large, nb - 1)
    return (np.where(rel > 0, nb, 0) + np.where(n < max_exact, n, large)).astype(np.int32)


def _tile_deltas(n_blk):
    d = np.arange(2 * n_blk - 1)[:, None, None] - (n_blk - 1)
    r = np.arange(TB)[None, :, None]
    c = np.arange(TB)[None, None, :]
    return d * TB + c - r


def _dilated_log_multiplicity(delta):
    mult = np.zeros(delta.shape, np.float32)
    for window, dil in A_GROUPS:
        mult += ((delta % dil == 0) & (np.abs(delta) <= window // 2)).astype(np.float32)
    with np.errstate(divide="ignore"):
        return np.where(mult > 0, np.log(np.maximum(mult, 1.0)), NEG_INF).astype(np.float32)


def _bias_kernel(table_ref, bucket_ref, base_ref, o_ref, *, n_a, tile_bucket):
    h = pl.program_id(0)
    for d, const in enumerate(tile_bucket):
        acc = jnp.where(h < n_a, base_ref[d], 0.0)
        if const is not None:
            acc = acc + table_ref[const, h]
        else:
            bucket = bucket_ref[d]
            for b in range(NUM_BUCKETS):
                acc = acc + jnp.where(bucket == b, table_ref[b, h], 0.0)
        o_ref[0, d] = acc * LOG2E


def _bias_tiles(rel_bias, n_blk):
    delta = _tile_deltas(n_blk)
    bucket_np = _rel_bucket_np(delta)
    tile_bucket = tuple(int(t.flat[0]) if (t == t.flat[0]).all() else None for t in bucket_np)
    bucket = jnp.asarray(bucket_np)
    base = jnp.asarray(_dilated_log_multiplicity(delta))
    n_heads = rel_bias.shape[1]
    nd = 2 * n_blk - 1
    return pl.pallas_call(
        functools.partial(_bias_kernel, n_a=A_HEADS, tile_bucket=tile_bucket),
        grid=(n_heads,),
        in_specs=[pl.BlockSpec(memory_space=pltpu.SMEM),
                  pl.BlockSpec((nd, TB, TB), lambda h: (0, 0, 0)),
                  pl.BlockSpec((nd, TB, TB), lambda h: (0, 0, 0))],
        out_specs=pl.BlockSpec((1, nd, TB, TB), lambda h: (h, 0, 0, 0)),
        out_shape=jax.ShapeDtypeStruct((n_heads, nd, TB, TB), F32),
        compiler_params=_params("arbitrary"),
        name="bias_tiles",
    )(rel_bias, bucket, base)


def _lane_lo(rows):
    return lax.broadcasted_iota(jnp.int32, (rows, LANES), 1) < HEAD


def _with_ones(v_bf):
    return jnp.concatenate([v_bf, jnp.ones(v_bf.shape, BF16)], axis=1)


def _store_logits(buf, hh, s, bias_fn=None):
    rows, keys = s.shape
    for rb in range(rows // TB):
        for j in range(keys // LANES):
            c = s[rb * TB:(rb + 1) * TB, j * LANES:(j + 1) * LANES]
            if bias_fn is not None:
                c = c + bias_fn(rb, j)
            buf[hh, rb * TB:(rb + 1) * TB, j * LANES:(j + 1) * LANES] = c


def _softmax_pv(buf, hh, v_ext):
    rows = buf.shape[1]
    keys = v_ext.shape[0]
    es = []
    for rb in range(rows // TB):
        def chunk(j):
            return buf[hh, rb * TB:(rb + 1) * TB, j * LANES:(j + 1) * LANES]
        m = chunk(0)
        for j in range(1, keys // LANES):
            m = jnp.maximum(m, chunk(j))
        m = jnp.max(m, axis=1, keepdims=True)
        es.append(jnp.concatenate([jnp.exp2(chunk(j) - m).astype(BF16) for j in range(keys // LANES)], axis=1))
    o = _dot(jnp.concatenate(es, axis=0), v_ext)
    return o[:, :LANES] / o[:, LANES:]


def _attn_loop(n_q, qk_store, finish, s_a, s_b):
    qk_store(s_a, 0)

    def body(i, carry):
        q0 = 2 * i
        qk_store(s_b, q0 + 1)
        finish(s_a, q0)
        qk_store(s_a, q0 + 2)
        finish(s_b, q0 + 1)
        return carry

    lax.fori_loop(0, n_q // 2 - 1, body, 0)
    qk_store(s_b, n_q - 1)
    finish(s_a, n_q - 2)
    finish(s_b, n_q - 1)


def _attn_windows(n_q, qk_store, finish, s_a, s_b):
    bufs = (s_a, s_b)
    qk_store(bufs[0], 0)
    for qi in range(n_q):
        if qi + 1 < n_q:
            qk_store(bufs[(qi + 1) % 2], qi + 1)
        finish(bufs[qi % 2], qi)


def _logit_bufs(keys):
    return [pltpu.VMEM((2, TQ, keys), F32), pltpu.VMEM((2, TQ, keys), F32)]


def _seg_sum(x, lo):
    s0 = jnp.sum(jnp.where(lo, x, 0.0), axis=1, keepdims=True)
    s1 = jnp.sum(jnp.where(lo, 0.0, x), axis=1, keepdims=True)
    return jnp.where(lo, s0, s1)


def _mixer_a_kernel(q_ref, k_ref, v_ref, g_ref, bias_ref, o_ref, s_a, s_b, *, n_blk):
    k_bf = k_ref[0].astype(BF16)
    v_ext = _with_ones(v_ref[0].astype(BF16))
    lo = _lane_lo(TQ)
    rpb = TQ // TB
    n_q = n_blk // rpb
    reach = -(-max(w // 2 for w, _ in A_GROUPS) // TQ)

    def window(qi):
        return max(0, qi - reach), min(n_q, qi + reach + 1)

    def qk_store(buf, qi):
        k0, k1 = window(qi)
        q = q_ref[0, qi * TQ:(qi + 1) * TQ, :] * (HEAD ** -0.5 * LOG2E)
        for hh in range(2):
            qm = jnp.where(lo if hh == 0 else jnp.logical_not(lo), q, 0.0).astype(BF16)
            _store_logits(buf, hh, _dot_nt(qm, k_bf[k0 * TQ:k1 * TQ]),
                          lambda rb, j, hh=hh: bias_ref[hh, n_blk - 1 - (qi * rpb + rb) + k0 * rpb + j])

    def finish(buf, qi):
        k0, k1 = window(qi)
        v_w = v_ext[k0 * TQ:k1 * TQ]
        o = jnp.where(lo, _softmax_pv(buf, 0, v_w), _softmax_pv(buf, 1, v_w))
        o_ref[0, qi * TQ:(qi + 1) * TQ, :] = o * _silu(g_ref[0, qi * TQ:(qi + 1) * TQ, :])

    _attn_windows(n_q, qk_store, finish, s_a, s_b)


def _mixer_a(p, bias):
    b, s, _ = p.shape
    n_blk = s // TB
    nd = 2 * n_blk - 1
    n_pairs = A_HEADS // 2

    def col(off):
        return pl.BlockSpec((1, s, LANES), lambda bi, j, off=off: (bi, 0, off // LANES + j))

    return pl.pallas_call(
        functools.partial(_mixer_a_kernel, n_blk=n_blk),
        grid=(b, n_pairs),
        in_specs=[col(A_Q), col(A_K), col(A_V), col(A_G),
                  pl.BlockSpec((2, nd, TB, TB), lambda bi, j: (j, 0, 0, 0))],
        out_specs=pl.BlockSpec((1, s, LANES), lambda bi, j: (bi, 0, j)),
        out_shape=jax.ShapeDtypeStruct((b, s, n_pairs * LANES), F32),
        scratch_shapes=_logit_bufs(s),
        compiler_params=_params("parallel", "parallel"),
        name="mixer_a",
    )(p, p, p, p, bias)


def _mixer_d_kernel(q_ref, k_ref, v_ref, g_ref, bias_ref, lam_ref, sg_ref, o_ref, s_a, s_b, *, n_blk, lam_init):
    k_bf = k_ref[0].astype(BF16)
    v_ext = _with_ones(v_ref[0].astype(BF16))
    lo = _lane_lo(TQ)
    rpb = TQ // TB
    dl = lam_ref[...]
    lam = (jnp.exp(jnp.sum(dl[0:1] * dl[1:2], axis=1, keepdims=True))
           - jnp.exp(jnp.sum(dl[2:3] * dl[3:4], axis=1, keepdims=True)) + lam_init)
    sg = sg_ref[...] * (1.0 - lam_init)

    def qk_store(buf, qi):
        r0 = pl.multiple_of(qi * TQ, TQ)
        q = q_ref[0, pl.ds(r0, TQ), :] * (HEAD ** -0.5 * LOG2E)
        for hh in range(2):
            qm = jnp.where(lo if hh == 0 else jnp.logical_not(lo), q, 0.0).astype(BF16)
            _store_logits(buf, hh, _dot_nt(qm, k_bf),
                          lambda rb, j: bias_ref[0, n_blk - 1 - (qi * rpb + rb) + j])

    def finish(buf, qi):
        r0 = pl.multiple_of(qi * TQ, TQ)
        o = _softmax_pv(buf, 0, v_ext) - lam * _softmax_pv(buf, 1, v_ext)
        o = o * lax.rsqrt(jnp.mean(o * o, axis=1, keepdims=True) + 1e-5) * sg
        o_ref[0, pl.ds(r0, TQ), :] = o * _silu(g_ref[0, pl.ds(r0, TQ), :])

    _attn_loop(n_blk // rpb, qk_store, finish, s_a, s_b)


def _mixer_d(p, bias, d_lambda, subln_g, layer_idx):
    b, s, _ = p.shape
    n_blk = s // TB
    nd = 2 * n_blk - 1
    lam_init = 0.8 - 0.6 * math.exp(-0.3 * layer_idx)

    def col(off):
        return pl.BlockSpec((1, s, LANES), lambda bi, j, off=off: (bi, 0, off // LANES + j))

    return pl.pallas_call(
        functools.partial(_mixer_d_kernel, n_blk=n_blk, lam_init=lam_init),
        grid=(b, D_HEADS),
        in_specs=[col(D_Q), col(D_K), col(D_V), col(D_G),
                  pl.BlockSpec((1, nd, TB, TB), lambda bi, j: (A_HEADS + j, 0, 0, 0)),
                  pl.BlockSpec((4, HEAD), lambda bi, j: (0, 0)),
                  pl.BlockSpec((1, LANES), lambda bi, j: (0, 0))],
        out_specs=pl.BlockSpec((1, s, LANES), lambda bi, j: (bi, 0, j)),
        out_shape=jax.ShapeDtypeStruct((b, s, D_HEADS * LANES), F32),
        scratch_shapes=_logit_bufs(s),
        compiler_params=_params("parallel", "parallel"),
        name="mixer_d",
    )(p, p, p, p, bias, d_lambda, subln_g.reshape(1, LANES))


def _rope_tables(s):
    t = np.arange(s)
    row, colp = t // GRID_W, t % GRID_W
    qtr = HEAD // 4
    freqs = np.float32(ROPE_THETA) ** (-(np.arange(qtr, dtype=np.float32) / np.float32(qtr)))
    lane = np.arange(LANES) % HEAD
    use_col = (lane // (HEAD // 2)) == 1
    second = (lane % (HEAD // 2)) >= qtr
    pos = np.where(use_col[None, :], colp[:, None], row[:, None]).astype(np.float32)
    ang = (pos * freqs[lane % qtr][None, :]).astype(np.float32)
    cos = np.cos(ang).astype(np.float32)
    sin = np.where(second[None, :], np.sin(ang), -np.sin(ang)).astype(np.float32)
    return jnp.asarray(cos), jnp.asarray(sin)


def _norm_rope(x, gain, cos, sin, lo, first):
    ms = _seg_sum(x * x, lo) * (1.0 / HEAD)
    x = x * lax.rsqrt(ms + 1e-6) * gain
    qtr = HEAD // 4
    partner = jnp.where(first, pltpu.roll(x, LANES - qtr, axis=1), pltpu.roll(x, qtr, axis=1))
    return x * cos + partner * sin


def _mixer_c_kernel(q_ref, k_ref, v_ref, g_ref, cos_ref, sin_ref, qg_ref, kg_ref, o_ref, s_a, s_b, krot_ref,
                    *, n_blk, s):
    pair = pl.program_id(1)
    grp = pair // 2
    lo_s = _lane_lo(s)

    @pl.when(pair == 0)
    def _():
        lane_s = lax.broadcasted_iota(jnp.int32, (s, LANES), 1)
        first_s = (lane_s % (HEAD // 2)) < (HEAD // 4)
        krot_ref[...] = _norm_rope(k_ref[0], kg_ref[...], cos_ref[...], sin_ref[...], lo_s, first_s)

    k = krot_ref[...]
    v = v_ref[0]
    k_sw = pltpu.roll(k, HEAD, axis=1)
    v_sw = pltpu.roll(v, HEAD, axis=1)
    keep = jnp.logical_xor(lo_s, grp == 1)
    k_bf = jnp.where(keep, k, k_sw).astype(BF16)
    v_ext = _with_ones(jnp.where(keep, v, v_sw).astype(BF16))
    lo = _lane_lo(TQ)
    lane = lax.broadcasted_iota(jnp.int32, (TQ, LANES), 1)
    first = (lane % (HEAD // 2)) < (HEAD // 4)

    def qk_store(buf, qi):
        rows = slice(qi * TQ, (qi + 1) * TQ)
        q = _norm_rope(q_ref[0, rows, :], qg_ref[...], cos_ref[rows, :], sin_ref[rows, :], lo, first)
        q = q * (HEAD ** -0.5 * LOG2E)
        for hh in range(2):
            qm = jnp.where(lo if hh == 0 else jnp.logical_not(lo), q, 0.0).astype(BF16)
            _store_logits(buf, hh, _dot_nt(qm, k_bf))

    def finish(buf, qi):
        rows = slice(qi * TQ, (qi + 1) * TQ)
        o = jnp.where(lo, _softmax_pv(buf, 0, v_ext), _softmax_pv(buf, 1, v_ext))
        o_ref[0, rows, :] = o * _silu(g_ref[0, rows, :])

    _attn_windows(n_blk, qk_store, finish, s_a, s_b)


def _mixer_c(p, cos, sin, qn_g, kn_g):
    b, s, _ = p.shape
    n_blk = s // TQ
    n_pairs = 4

    def col(off):
        return pl.BlockSpec((1, s, LANES), lambda bi, j, off=off: (bi, 0, off // LANES + j))

    def fixed(off):
        return pl.BlockSpec((1, s, LANES), lambda bi, j, off=off: (bi, 0, off // LANES))

    tab = pl.BlockSpec((s, LANES), lambda bi, j: (0, 0))
    gain = pl.BlockSpec((1, LANES), lambda bi, j: (0, 0))
    return pl.pallas_call(
        functools.partial(_mixer_c_kernel, n_blk=n_blk, s=s),
        grid=(b, n_pairs),
        in_specs=[col(C_Q), fixed(C_K), fixed(C_V), col(C_G), tab, tab, gain, gain],
        out_specs=pl.BlockSpec((1, s, LANES), lambda bi, j: (bi, 0, j)),
        out_shape=jax.ShapeDtypeStruct((b, s, n_pairs * LANES), F32),
        scratch_shapes=_logit_bufs(s) + [pltpu.VMEM((s, LANES), F32)],
        compiler_params=_params("parallel", "arbitrary"),
        name="mixer_c",
    )(p, p, p, p, cos, sin, jnp.tile(qn_g, 2).reshape(1, LANES), jnp.tile(kn_g, 2).reshape(1, LANES))


def _mixer_m_kernel(q_ref, g_ref, k_ref, v_ref, o_ref, s_a, s_b, *, n_blk):
    k_bf = k_ref[0].astype(BF16)
    v_ext = _with_ones(v_ref[0].astype(BF16))
    lo = _lane_lo(TQ)

    def qk_store(buf, qi):
        r0 = pl.multiple_of(qi * TQ, TQ)
        q = q_ref[0, pl.ds(r0, TQ), :] * (HEAD ** -0.5 * LOG2E)
        for hh in range(2):
            qm = jnp.where(lo if hh == 0 else jnp.logical_not(lo), q, 0.0).astype(BF16)
            _store_logits(buf, hh, _dot_nt(qm, k_bf))

    def finish(buf, qi):
        r0 = pl.multiple_of(qi * TQ, TQ)
        o = jnp.where(lo, _softmax_pv(buf, 0, v_ext), _softmax_pv(buf, 1, v_ext))
        o_ref[0, pl.ds(r0, TQ), :] = o * _silu(g_ref[0, pl.ds(r0, TQ), :])

    _attn_loop(n_blk, qk_store, finish, s_a, s_b)


def _mixer_m(p, kv):
    b, s, _ = p.shape
    n_mem = kv.shape[1]
    n_blk = s // TQ
    n_pairs = 2

    def col(off):
        return pl.BlockSpec((1, s, LANES), lambda bi, j, off=off: (bi, 0, off // LANES + j))

    return pl.pallas_call(
        functools.partial(_mixer_m_kernel, n_blk=n_blk),
        grid=(b, n_pairs),
        in_specs=[col(M_Q), col(M_G),
                  pl.BlockSpec((1, n_mem, LANES), lambda bi, j: (bi, 0, j)),
                  pl.BlockSpec((1, n_mem, LANES), lambda bi, j: (bi, 0, n_pairs + j))],
        out_specs=pl.BlockSpec((1, s, LANES), lambda bi, j: (bi, 0, j)),
        out_shape=jax.ShapeDtypeStruct((b, s, n_pairs * LANES), F32),
        scratch_shapes=_logit_bufs(n_mem),
        compiler_params=_params("parallel", "parallel"),
        name="mixer_m",
    )(p, p, kv, kv)


def _split3_dot(tri_bf, x):
    h1 = x.astype(BF16)
    r1 = x - h1.astype(F32)
    h2 = r1.astype(BF16)
    h3 = (r1 - h2.astype(F32)).astype(BF16)
    return _dot(tri_bf, h1) + _dot(tri_bf, h2) + _dot(tri_bf, h3)


def _rwkv_kernel(r_ref, k_ref, v_ref, wa_ref, w0_ref, wup_ref, a0_ref, aup_ref,
                 kk_ref, ka_ref, rk_ref, yf_ref, yb_ref, bvf_ref, bvb_ref, state_ref, *, nb, nch):
    C = CHUNK
    R = nch * C
    c = pl.program_id(1)

    @pl.when(c == 0)
    def _():
        state_ref[...] = jnp.zeros_like(state_ref)

    n_pairs = r_ref.shape[2] // LANES
    width = r_ref.shape[2]
    tt = lax.broadcasted_iota(jnp.int32, (C, LANES), 0)
    ss = lax.broadcasted_iota(jnp.int32, (C, LANES), 1) % C
    eye_bf = jnp.where(tt == ss, 1.0, 0.0).astype(BF16)
    rr = lax.broadcasted_iota(jnp.int32, (LANES, LANES), 0)
    cc = lax.broadcasted_iota(jnp.int32, (LANES, LANES), 1)
    eye = rr == cc
    tr = lax.broadcasted_iota(jnp.int32, (R, R), 0)
    tc = lax.broadcasted_iota(jnp.int32, (R, R), 1)
    same_chunk = (tr // C) == (tc // C)
    lo = _lane_lo(C)
    hi = jnp.logical_not(lo)
    lo_bf = jnp.where(lo, 1.0, 0.0).astype(BF16)
    hi_bf = jnp.where(hi, 1.0, 0.0).astype(BF16)
    lo_r = _lane_lo(R)

    def stack(x):
        return jnp.concatenate([jnp.where(lo, x, 0.0), jnp.where(hi, x, 0.0)], axis=0)

    def stack_bf(x):
        return jnp.concatenate([x * lo_bf, x * hi_bf], axis=0)

    units = []
    for e in range(2):
        rev = e == 1
        blk = (nb - 1 - c) if rev else c
        r0 = pl.multiple_of(blk * R, R)
        bv_ref = bvb_ref if rev else bvf_ref
        strict = (tt < ss) if rev else (tt > ss)
        incl = (tt <= ss) if rev else (tt >= ss)
        tri_bf = jnp.where(jnp.logical_and(same_chunk, (tr <= tc) if rev else (tr >= tc)), 1.0, 0.0).astype(BF16)
        last = 0 if rev else C - 1
        up, dn = (ss, tt) if rev else (tt, ss)
        level_masks = []
        m = 1
        while m < C:
            same_blk = ((tt ^ ss) >> (int(math.log2(m)) + 1)) == 0
            lvl_mask = jnp.logical_and(same_blk, jnp.logical_and((up & m) != 0, (dn & m) == 0))
            level_masks.append(jnp.where(lvl_mask, 1.0, 0.0).astype(BF16))
            m *= 2

        rs = r_ref[0, pl.ds(r0, R), :]
        ks = k_ref[0, pl.ds(r0, R), :]
        vs = v_ref[0, pl.ds(r0, R), :]
        wa = wa_ref[0, pl.ds(r0, R), :]
        sel = lo_r if e == 0 else jnp.logical_not(lo_r)
        wd = jnp.where(sel, jnp.tanh(wa[:, :LANES]), 0.0).astype(BF16)
        ad = jnp.where(sel, wa[:, LANES:], 0.0).astype(BF16)
        zw = w0_ref[e:e + 1, :] + _dot(wd, wup_ref[...])
        lw = -math.exp(-0.5) / (1.0 + jnp.exp(-zw))
        za = a0_ref[e:e + 1, :] + _dot(ad, aup_ref[...])
        a_sig = 1.0 / (1.0 + jnp.exp(-za))
        kk = ks * kk_ref[...]
        kk2 = kk * kk
        nrm2 = jnp.concatenate([_seg_sum(kk2[:, i * LANES:(i + 1) * LANES], lo_r) for i in range(n_pairs)], axis=1)
        kkn = kk / jnp.maximum(jnp.sqrt(nrm2), 1e-12)
        ka = ka_ref[...]
        ke = ks * ((1.0 - ka) + ka * a_sig)
        be = kkn * a_sig
        rkr = rs * ke * rk_ref[...]
        bonus = jnp.concatenate([_seg_sum(rkr[:, i * LANES:(i + 1) * LANES], lo_r) for i in range(n_pairs)], axis=1)
        bv_ref[0] = bonus * vs

        l_incl = _split3_dot(tri_bf, lw)
        l_tot = jnp.concatenate(
            [jnp.broadcast_to(l_incl[j * C + last:j * C + last + 1, :], (C, width)) for j in range(nch)], axis=0)
        w_inv = jnp.exp(-l_incl)
        d_end = jnp.exp(l_tot)
        w_end = d_end * w_inv
        at = -kkn * jnp.exp(l_incl - lw)
        rt = rs * jnp.exp(l_incl)
        bt = be * w_inv
        kt = ke * w_inv
        bh = be * w_end
        kh = ke * w_end
        for j in range(nch):
            rows = slice(j * C, (j + 1) * C)
            for pr in range(n_pairs):
                sl = slice(pr * LANES, (pr + 1) * LANES)
                units.append(dict(
                    e=e, j=j, pr=pr, strict=strict, incl=incl, levels=level_masks,
                    at=at[rows, sl], rt=rt[rows, sl], bt=bt[rows, sl], kt=kt[rows, sl],
                    bh=bh[rows, sl], kh=kh[rows, sl], v=vs[rows, sl], d_end=d_end[j * C:j * C + 1, sl]))

    for u in units:
        u["at_bf"] = u["at"].astype(BF16)
        u["v_st"] = stack_bf(u["v"].astype(BF16))
        lhs = jnp.concatenate([u["at_bf"], u["rt"].astype(BF16)], axis=0)
        rhs = jnp.concatenate([stack_bf(u["bt"].astype(BF16)), stack_bf(u["kt"].astype(BF16))], axis=0)
        g = _dot_nt(lhs, rhs)
        u["a_ab"] = jnp.where(u["strict"], g[:C, :LANES], 0.0).astype(BF16)
        u["a_ak"] = jnp.where(u["strict"], g[:C, LANES:], 0.0).astype(BF16)
        u["ly"] = jnp.concatenate([jnp.where(u["incl"], g[C:, :LANES], 0.0),
                                   jnp.where(u["incl"], g[C:, LANES:], 0.0)], axis=1).astype(BF16)
        u["t"] = eye_bf + u["a_ab"] * u["levels"][0]
    for lvl in range(1, len(units[0]["levels"])):
        for u in units:
            u["tmp"] = _dot(u["a_ab"] * u["levels"][lvl], stack_bf(u["t"])).astype(BF16)
        for u in units:
            u["t"] = u["t"] + _dot(u["t"], stack_bf(u["tmp"])).astype(BF16)
    for u in units:
        u["akv"] = _dot(u["a_ak"], u["v_st"]).astype(BF16)
    for u in units:
        x = jnp.concatenate([stack_bf(u["at_bf"]), stack_bf(u["akv"])], axis=1)
        pq = _dot(u["t"], x).astype(BF16)
        u["ry"] = jnp.concatenate(
            [jnp.concatenate([stack_bf(pq[:, :LANES]), stack_bf(pq[:, LANES:])], axis=1),
             jnp.concatenate([jnp.zeros((LANES, LANES), BF16), u["v_st"]], axis=1)], axis=0)
        ls = jnp.concatenate([stack(u["bh"]), stack(u["kh"])], axis=0)
        u["lyz"] = jnp.concatenate([u["ly"], ls.T.astype(BF16)], axis=0)
    for u in units:
        yz = _dot(u["lyz"], u["ry"])
        yy = yz[:C]
        zz = yz[C:]
        u["rpm"] = jnp.concatenate([u["rt"] + yy[:, :LANES], zz[:, :LANES]], axis=0).astype(BF16)
        u["y0"] = yy[:, LANES:]
        u["z"] = zz[:, LANES:]
        d_diag = jnp.where(eye, jnp.broadcast_to(u["d_end"], (LANES, LANES)), 0.0)
        u["d_col"] = jnp.sum(d_diag, axis=1, keepdims=True)

    by_key = {(u["e"], u["j"], u["pr"]): u for u in units}
    for e in range(2):
        y_ref = yb_ref if e == 1 else yf_ref
        order = list(range(nch))[::-1] if e == 1 else list(range(nch))
        ys = {}
        states = [state_ref[e, pr] for pr in range(n_pairs)]
        for j in order:
            for pr in range(n_pairs):
                u = by_key[(e, j, pr)]
                ym = _dot(u["rpm"], states[pr].astype(BF16))
                ys[(j, pr)] = ym[:C] + u["y0"]
                states[pr] = u["d_col"] * states[pr] + ym[C:] + u["z"]
        for pr in range(n_pairs):
            state_ref[e, pr] = states[pr]
        y_ref[0] = jnp.concatenate(
            [jnp.concatenate([ys[(j, pr)] for pr in range(n_pairs)], axis=1) for j in range(nch)], axis=0)


RWKV_CHUNKS_PER_STEP = 2


def _mixer_b_scan(p, w0, w_up, a0, a_up, k_k, k_a, r_k):
    b, s, _ = p.shape
    width = 512
    nch = RWKV_CHUNKS_PER_STEP
    rows = nch * CHUNK
    nb = s // rows
    n_pairs = width // LANES

    def seq(off, w):
        return pl.BlockSpec((1, s, w), lambda bi, c, off=off, w=w: (bi, 0, off // w))

    def full(shape):
        return pl.BlockSpec(shape, lambda bi, c: tuple(0 for _ in shape))

    out_f = pl.BlockSpec((1, rows, width), lambda bi, c: (bi, c, 0))
    out_b = pl.BlockSpec((1, rows, width), lambda bi, c: (bi, nb - 1 - c, 0))
    sds = jax.ShapeDtypeStruct((b, s, width), F32)
    lora = 2 * HEAD
    return pl.pallas_call(
        functools.partial(_rwkv_kernel, nb=nb, nch=nch),
        grid=(b, nb),
        in_specs=[seq(B_R, width), seq(B_K, width), seq(B_V, width), seq(B_WA, 2 * LANES),
                  full((2, width)), full((lora, width)),
                  full((2, width)), full((lora, width)), full((1, width)), full((1, width)), full((1, width))],
        out_specs=[out_f, out_b, out_f, out_b],
        out_shape=[sds, sds, sds, sds],
        scratch_shapes=[pltpu.VMEM((2, n_pairs, LANES, LANES), F32)],
        compiler_params=_params("parallel", "arbitrary"),
        name="mixer_b",
    )(p, p, p, p, w0, w_up.reshape(lora, width).astype(BF16), a0, a_up.reshape(lora, width).astype(BF16),
      k_k.reshape(1, width), k_a.reshape(1, width), r_k.reshape(1, width))


def _merge_kernel(h_ref, hb_ref, oa_ref, yf_ref, yb_ref, bvf_ref, bvb_ref, gb0_ref, gb1_ref, oc_ref, od_ref,
                  om_ref, wg_ref, bg_ref, wb_ref, wo_ref, lng_ref, lnb_ref, bg_g_ref, bg_b_ref,
                  hn_ref, hnb_ref, *, alpha):
    d = h_ref.shape[1]
    tm = h_ref.shape[0]
    lo = _lane_lo(tm)
    y = yf_ref[...] + yb_ref[...]
    n_pairs = y.shape[1] // LANES
    gn = []
    for i in range(n_pairs):
        yp = y[:, i * LANES:(i + 1) * LANES]
        mu = _seg_sum(yp, lo) * (1.0 / HEAD)
        dy = yp - mu
        var = _seg_sum(dy * dy, lo) * (1.0 / HEAD)
        gn.append(dy * lax.rsqrt(var + B_GN_EPS))
    gn = jnp.concatenate(gn, axis=1) * bg_g_ref[...] + bg_b_ref[...]
    gb = jnp.concatenate([gb0_ref[...], gb1_ref[...]], axis=1)
    ob = (gn + bvf_ref[...] + bvb_ref[...]) * _silu(gb)

    hb = hb_ref[...]
    branches = (oa_ref[...], ob, oc_ref[...], od_ref[...], om_ref[...])
    acc = None
    row = 0
    for i, o in enumerate(branches):
        wdt = o.shape[1]
        gate = 1.0 / (1.0 + jnp.exp(-(_dot(hb, wg_ref[:, i * d:(i + 1) * d]) + bg_ref[:, i * d:(i + 1) * d])))
        proj = _dot(o.astype(BF16), wb_ref[row:row + wdt, :])
        term = gate * proj
        acc = term if acc is None else acc + term
        row += wdt
    out = _dot(acc.astype(BF16), wo_ref[...])
    z = alpha * h_ref[...] + out
    mu = jnp.mean(z, -1, keepdims=True)
    dz = z - mu
    var = jnp.mean(dz * dz, -1, keepdims=True)
    hn = dz * lax.rsqrt(var + 1e-5) * lng_ref[...] + lnb_ref[...]
    hn_ref[...] = hn
    hnb_ref[...] = hn.astype(BF16)


def _merge(h, hb, o_a, yf, yb, bvf, bvb, p2, o_c, o_d, o_m, layer, w_gate, b_gate, w_branch, w_out, ln_g, ln_b,
           bln_g, bln_b):
    n, d = h.shape
    tm = 256
    alpha = (2 * DEPTH) ** 0.25

    def rows(w):
        return pl.BlockSpec((tm, w), lambda i: (i, 0))

    def full(shape):
        return pl.BlockSpec(shape, lambda i: tuple(0 for _ in shape))

    def of_layer(w):
        return pl.BlockSpec((None,) + w.shape[1:], lambda i: (layer, 0, 0))

    gb_specs = pl.BlockSpec((tm, 256), lambda i: (i, B_G // 256))
    gb_specs2 = pl.BlockSpec((tm, 256), lambda i: (i, B_G // 256 + 1))
    return pl.pallas_call(
        functools.partial(_merge_kernel, alpha=alpha),
        grid=(n // tm,),
        in_specs=[rows(d), rows(d), rows(512), rows(512), rows(512), rows(512), rows(512),
                  gb_specs, gb_specs2, rows(512), rows(512), rows(256),
                  of_layer(w_gate), full((1, b_gate.shape[0])), of_layer(w_branch), of_layer(w_out),
                  full((1, d)), full((1, d)), full((1, 512)), full((1, 512))],
        out_specs=[rows(d), rows(d)],
        out_shape=[jax.ShapeDtypeStruct((n, d), F32), jax.ShapeDtypeStruct((n, d), BF16)],
        compiler_params=_params("parallel"),
        name="merge",
    )(h, hb, o_a, yf, yb, bvf, bvb, p2, p2, o_c, o_d, o_m,
      w_gate, b_gate.reshape(1, -1), w_branch, w_out, ln_g.reshape(1, d), ln_b.reshape(1, d),
      bln_g.reshape(1, 512), bln_b.reshape(1, 512))


def kernel(x, mem, ln_in_g, ln_in_b, rel_bias, w_in, shift_mu, rwkv_w0, rwkv_w_up, rwkv_a0, rwkv_a_up,
           rwkv_k_k, rwkv_k_a, rwkv_r_k, rwkv_ln_g, rwkv_ln_b, c_qnorm_g, c_knorm_g, d_lambda, d_subln_g,
           w_mem_kv, w_branch, w_gate, b_gate, w_out, ln_g, ln_b):
    b, s, d = x.shape
    n = b * s
    bias = _bias_tiles(rel_bias, s // TB)
    cos, sin = _rope_tables(s)
    mem_bf = mem.reshape(b * mem.shape[1], d).astype(BF16)
    h, hb = _ln_in(x.reshape(n, d), ln_in_g, ln_in_b)
    w_gate, w_branch, w_out = (w.astype(BF16) for w in (w_gate, w_branch, w_out))
    for l in range(DEPTH):
        p2 = _proj_in(hb, w_in, l, shift_mu[l], s)
        p = p2.reshape(b, s, IN_COLS)
        kv = _matmul(mem_bf, w_mem_kv, l, 512, 512, "proj_mem").reshape(b, mem.shape[1], -1)
        o_a = _mixer_a(p, bias)
        yf, yb, bvf, bvb = _mixer_b_scan(p, rwkv_w0[l], rwkv_w_up[l], rwkv_a0[l], rwkv_a_up[l],
                                         rwkv_k_k[l], rwkv_k_a[l], rwkv_r_k[l])
        o_c = _mixer_c(p, cos, sin, c_qnorm_g[l], c_knorm_g[l])
        o_d = _mixer_d(p, bias, d_lambda[l], d_subln_g[l], l)
        o_m = _mixer_m(p, kv)
        flat = lambda t: t.reshape(n, t.shape[-1])
        h, hb = _merge(h, hb, flat(o_a), flat(yf), flat(yb), flat(bvf), flat(bvb), p2, flat(o_c), flat(o_d),
                       flat(o_m), l, w_gate, b_gate[l], w_branch, w_out, ln_g[l], ln_b[l],
                       rwkv_ln_g[l], rwkv_ln_b[l])
    return h.reshape(b, s, d)
```

```python
import functools
import math

import numpy as np
import jax
import jax.numpy as jnp
from jax import lax
from jax.experimental import pallas as pl
from jax.experimental.pallas import tpu as pltpu

F32 = jnp.float32
BF16 = jnp.bfloat16

LANES = 128
HEAD = 64
VMEM_LIMIT = 56 * 1024 * 1024

D_MODEL = 1024
DEPTH = 2
GRID_W = 64
ROPE_THETA = 10000.0
NUM_BUCKETS = 32
REL_MAX_DISTANCE = 1024
A_HEADS = 8
D_HEADS = 4
A_GROUPS = ((128, 1), (512, 4), (2048, 16))
B_GN_EPS = 64e-5
NEG_INF = -1e30

A_Q, A_K, A_V, A_G = 0, 512, 1024, 1536
B_R, B_K, B_V, B_WA, B_G = 2048, 2560, 3072, 3584, 3840
C_Q, C_K, C_V, C_G = 4352, 4864, 4992, 5120
D_Q, D_K, D_V, D_G = 5632, 6144, 6656, 7168
M_Q, M_G = 7680, 7936
IN_COLS = 8192

TB = 128
TQ = 256
LOG2E = math.log2(math.e)
CHUNK = 64


def _dot(a, b):
    return jnp.dot(a, b, preferred_element_type=F32)


def _dot_nt(a, b):
    return lax.dot_general(a, b, (((1,), (1,)), ((), ())), preferred_element_type=F32)


def _silu(g):
    return g / (1.0 + jnp.exp(-g))


def _params(*sem):
    return pltpu.CompilerParams(dimension_semantics=sem, vmem_limit_bytes=VMEM_LIMIT)


def _ln_in_kernel(x_ref, g_ref, b_ref, h_ref, hb_ref):
    x = x_ref[...]
    mu = jnp.mean(x, -1, keepdims=True)
    d = x - mu
    var = jnp.mean(d * d, -1, keepdims=True)
    h = d * lax.rsqrt(var + 1e-5) * g_ref[...] + b_ref[...]
    h_ref[...] = h
    hb_ref[...] = h.astype(BF16)


def _ln_in(x2, g, b):
    n, d = x2.shape
    tm = 512
    return pl.pallas_call(
        _ln_in_kernel,
        grid=(n // tm,),
        in_specs=[pl.BlockSpec((tm, d), lambda i: (i, 0)),
                  pl.BlockSpec((1, d), lambda i: (0, 0)),
                  pl.BlockSpec((1, d), lambda i: (0, 0))],
        out_specs=[pl.BlockSpec((tm, d), lambda i: (i, 0)),
                   pl.BlockSpec((tm, d), lambda i: (i, 0))],
        out_shape=[jax.ShapeDtypeStruct((n, d), F32), jax.ShapeDtypeStruct((n, d), BF16)],
        compiler_params=_params("parallel"),
        name="ln_in",
    )(x2, g.reshape(1, d), b.reshape(1, d))


def _matmul_kernel(a_ref, w_ref, o_ref, wbf_ref):
    @pl.when(pl.program_id(1) == 0)
    def _():
        wbf_ref[...] = w_ref[...].astype(BF16)

    o_ref[...] = _dot(a_ref[...], wbf_ref[...])


def _matmul(a, w, layer, tm, tn, name):
    m, k = a.shape
    n = w.shape[2]
    return pl.pallas_call(
        _matmul_kernel,
        grid=(n // tn, m // tm),
        in_specs=[pl.BlockSpec((tm, k), lambda j, i: (i, 0)),
                  pl.BlockSpec((None, k, tn), lambda j, i: (layer, 0, j))],
        out_specs=pl.BlockSpec((tm, tn), lambda j, i: (i, j)),
        out_shape=jax.ShapeDtypeStruct((m, n), F32),
        scratch_shapes=[pltpu.VMEM((k, tn), BF16)],
        compiler_params=_params("parallel", "arbitrary"),
        name=name,
    )(a, w)


HALO = 16


def _proj_in_kernel(a_ref, halo_ref, w_ref, mu_ref, o_ref, wbf_ref, sh_ref, *, shift_block, n_shift, sub):
    j = pl.program_id(0)
    i = pl.program_id(1)

    @pl.when(i == 0)
    def _():
        wbf_ref[...] = w_ref[...].astype(BF16)

    @pl.when(j != shift_block)
    def _():
        o_ref[...] = _dot(a_ref[...], wbf_ref[...])

    @pl.when(j == shift_block)
    def _():
        tm, tn = o_ref.shape
        first_half = (i % 2) == 0
        for cb in range(tn // sub):
            cols = slice(cb * sub, (cb + 1) * sub)
            if cols.start >= n_shift:
                o_ref[:, cols] = _dot(a_ref[...], wbf_ref[:, cols])
                continue
            xh = _dot(jnp.concatenate([a_ref[...], halo_ref[...]], axis=0), wbf_ref[:, cols])
            x = xh[:tm]
            prev_row = jnp.where(first_half, 0.0, xh[tm + HALO - 1:tm + HALO])
            next_row = jnp.where(first_half, xh[tm:tm + 1], 0.0)
            sh_ref[7:8, :] = prev_row
            sh_ref[8:tm + 8, :] = x
            sh_ref[tm + 8:tm + 9, :] = next_row
            prev = sh_ref[7:tm + 7, :]
            nxt = sh_ref[9:tm + 9, :]
            mu0 = mu_ref[0:1, cols]
            mu1 = mu_ref[1:2, cols]
            o_ref[:, cols] = x * (1.0 - mu0 - mu1) + mu0 * prev + mu1 * nxt


def _proj_in(a, w, layer, mu, seq):
    m, k = a.shape
    n = w.shape[2]
    tn = 2048
    tm = seq // 2
    assert B_R % tn == 0 and B_G < B_R + tn, "the RWKV mix columns must sit inside one column block"
    mu_pad = jnp.zeros((2, tn), F32).at[:, :mu.shape[1]].set(mu)
    halo_rows = tm // HALO

    def halo_index(j, i):
        return ((i + 1 - i % 2) * halo_rows - i % 2, 0)

    return pl.pallas_call(
        functools.partial(_proj_in_kernel, shift_block=B_R // tn, n_shift=mu.shape[1], sub=256),
        grid=(n // tn, m // tm),
        in_specs=[pl.BlockSpec((tm, k), lambda j, i: (i, 0)),
                  pl.BlockSpec((HALO, k), halo_index),
                  pl.BlockSpec((None, k, tn), lambda j, i: (layer, 0, j)),
                  pl.BlockSpec((2, tn), lambda j, i: (0, 0))],
        out_specs=pl.BlockSpec((tm, tn), lambda j, i: (i, j)),
        out_shape=jax.ShapeDtypeStruct((m, n), F32),
        scratch_shapes=[pltpu.VMEM((k, tn), BF16), pltpu.VMEM((tm + 16, 256), F32)],
        compiler_params=_params("parallel", "arbitrary"),
        name="proj_in",
    )(a, a, w, mu_pad)


def _rel_bucket_np(rel):
    nb = NUM_BUCKETS // 2
    max_exact = nb // 2
    n = np.abs(rel)
    nf = np.maximum(n, 1).astype(np.float32)
    large = max_exact + (np.log(nf / np.float32(max_exact)) / np.float32(math.log(REL_MAX_DISTANCE / max_exact))
                         * np.float32(nb - max_exact)).astype(np.int32)
    large = np.minimum(large, nb - 1)
    return (np.where(rel > 0, nb, 0) + np.where(n < max_exact, n, large)).astype(np.int32)


def _tile_deltas(n_blk):
    d = np.arange(2 * n_blk - 1)[:, None, None] - (n_blk - 1)
    r = np.arange(TB)[None, :, None]
    c = np.arange(TB)[None, None, :]
    return d * TB + c - r


def _dilated_log_multiplicity(delta):
    mult = np.zeros(delta.shape, np.float32)
    for window, dil in A_GROUPS:
        mult += ((delta % dil == 0) & (np.abs(delta) <= window // 2)).astype(np.float32)
    with np.errstate(divide="ignore"):
        return np.where(mult > 0, np.log(np.maximum(mult, 1.0)), NEG_INF).astype(np.float32)


def _bias_kernel(table_ref, bucket_ref, base_ref, o_ref, *, n_a, tile_bucket):
    h = pl.program_id(0)
    for d, const in enumerate(tile_bucket):
        acc = jnp.where(h < n_a, base_ref[d], 0.0)
        if const is not None:
            acc = acc + table_ref[const, h]
        else:
            bucket = bucket_ref[d]
            for b in range(NUM_BUCKETS):
                acc = acc + jnp.where(bucket == b, table_ref[b, h], 0.0)
        o_ref[0, d] = acc * LOG2E


def _bias_tiles(rel_bias, n_blk):
    delta = _tile_deltas(n_blk)
    bucket_np = _rel_bucket_np(delta)
    tile_bucket = tuple(int(t.flat[0]) if (t == t.flat[0]).all() else None for t in bucket_np)
    bucket = jnp.asarray(bucket_np)
    base = jnp.asarray(_dilated_log_multiplicity(delta))
    n_heads = rel_bias.shape[1]
    nd = 2 * n_blk - 1
    return pl.pallas_call(
        functools.partial(_bias_kernel, n_a=A_HEADS, tile_bucket=tile_bucket),
        grid=(n_heads,),
        in_specs=[pl.BlockSpec(memory_space=pltpu.SMEM),
                  pl.BlockSpec((nd, TB, TB), lambda h: (0, 0, 0)),
                  pl.BlockSpec((nd, TB, TB), lambda h: (0, 0, 0))],
        out_specs=pl.BlockSpec((1, nd, TB, TB), lambda h: (h, 0, 0, 0)),
        out_shape=jax.ShapeDtypeStruct((n_heads, nd, TB, TB), F32),
        compiler_params=_params("arbitrary"),
        name="bias_tiles",
    )(rel_bias, bucket, base)


def _lane_lo(rows):
    return lax.broadcasted_iota(jnp.int32, (rows, LANES), 1) < HEAD


def _with_ones(v_bf):
    return jnp.concatenate([v_bf, jnp.ones(v_bf.shape, BF16)], axis=1)


def _store_logits(buf, hh, s, bias_fn=None):
    rows, keys = s.shape
    for rb in range(rows // TB):
        for j in range(keys // LANES):
            c = s[rb * TB:(rb + 1) * TB, j * LANES:(j + 1) * LANES]
            if bias_fn is not None:
                c = c + bias_fn(rb, j)
            buf[hh, rb * TB:(rb + 1) * TB, j * LANES:(j + 1) * LANES] = c


def _softmax_pv(buf, hh, v_ext):
    rows = buf.shape[1]
    keys = v_ext.shape[0]
    es = []
    for rb in range(rows // TB):
        def chunk(j):
            return buf[hh, rb * TB:(rb + 1) * TB, j * LANES:(j + 1) * LANES]
        m = chunk(0)
        for j in range(1, keys // LANES):
            m = jnp.maximum(m, chunk(j))
        m = jnp.max(m, axis=1, keepdims=True)
        es.append(jnp.concatenate([jnp.exp2(chunk(j) - m).astype(BF16) for j in range(keys // LANES)], axis=1))
    o = _dot(jnp.concatenate(es, axis=0), v_ext)
    return o[:, :LANES] / o[:, LANES:]


def _attn_loop(n_q, qk_store, finish, s_a, s_b):
    qk_store(s_a, 0)

    def body(i, carry):
        q0 = 2 * i
        qk_store(s_b, q0 + 1)
        finish(s_a, q0)
        qk_store(s_a, q0 + 2)
        finish(s_b, q0 + 1)
        return carry

    lax.fori_loop(0, n_q // 2 - 1, body, 0)
    qk_store(s_b, n_q - 1)
    finish(s_a, n_q - 2)
    finish(s_b, n_q - 1)


def _attn_windows(n_q, qk_store, finish, s_a, s_b):
    bufs = (s_a, s_b)
    qk_store(bufs[0], 0)
    for qi in range(n_q):
        if qi + 1 < n_q:
            qk_store(bufs[(qi + 1) % 2], qi + 1)
        finish(bufs[qi % 2], qi)


def _logit_bufs(keys):
    return [pltpu.VMEM((2, TQ, keys), F32), pltpu.VMEM((2, TQ, keys), F32)]


def _seg_sum(x, lo):
    s0 = jnp.sum(jnp.where(lo, x, 0.0), axis=1, keepdims=True)
    s1 = jnp.sum(jnp.where(lo, 0.0, x), axis=1, keepdims=True)
    return jnp.where(lo, s0, s1)


def _mixer_a_kernel(q_ref, k_ref, v_ref, g_ref, bias_ref, o_ref, s_a, s_b, *, n_blk):
    k_bf = k_ref[0].astype(BF16)
    v_ext = _with_ones(v_ref[0].astype(BF16))
    lo = _lane_lo(TQ)
    rpb = TQ // TB
    n_q = n_blk // rpb
    reach = -(-max(w // 2 for w, _ in A_GROUPS) // TQ)

    def window(qi):
        return max(0, qi - reach), min(n_q, qi + reach + 1)

    def qk_store(buf, qi):
        k0, k1 = window(qi)
        q = q_ref[0, qi * TQ:(qi + 1) * TQ, :] * (HEAD ** -0.5 * LOG2E)
        for hh in range(2):
            qm = jnp.where(lo if hh == 0 else jnp.logical_not(lo), q, 0.0).astype(BF16)
            _store_logits(buf, hh, _dot_nt(qm, k_bf[k0 * TQ:k1 * TQ]),
                          lambda rb, j, hh=hh: bias_ref[hh, n_blk - 1 - (qi * rpb + rb) + k0 * rpb + j])

    def finish(buf, qi):
        k0, k1 = window(qi)
        v_w = v_ext[k0 * TQ:k1 * TQ]
        o = jnp.where(lo, _softmax_pv(buf, 0, v_w), _softmax_pv(buf, 1, v_w))
        o_ref[0, qi * TQ:(qi + 1) * TQ, :] = o * _silu(g_ref[0, qi * TQ:(qi + 1) * TQ, :])

    _attn_windows(n_q, qk_store, finish, s_a, s_b)


def _mixer_a(p, bias):
    b, s, _ = p.shape
    n_blk = s // TB
    nd = 2 * n_blk - 1
    n_pairs = A_HEADS // 2

    def col(off):
        return pl.BlockSpec((1, s, LANES), lambda bi, j, off=off: (bi, 0, off // LANES + j))

    return pl.pallas_call(
        functools.partial(_mixer_a_kernel, n_blk=n_blk),
        grid=(b, n_pairs),
        in_specs=[col(A_Q), col(A_K), col(A_V), col(A_G),
                  pl.BlockSpec((2, nd, TB, TB), lambda bi, j: (j, 0, 0, 0))],
        out_specs=pl.BlockSpec((1, s, LANES), lambda bi, j: (bi, 0, j)),
        out_shape=jax.ShapeDtypeStruct((b, s, n_pairs * LANES), F32),
        scratch_shapes=_logit_bufs(s),
        compiler_params=_params("parallel", "parallel"),
        name="mixer_a",
    )(p, p, p, p, bias)


def _mixer_d_kernel(q_ref, k_ref, v_ref, g_ref, bias_ref, lam_ref, sg_ref, o_ref, s_a, s_b, *, n_blk, lam_init):
    k_bf = k_ref[0].astype(BF16)
    v_ext = _with_ones(v_ref[0].astype(BF16))
    lo = _lane_lo(TQ)
    rpb = TQ // TB
    dl = lam_ref[...]
    lam = (jnp.exp(jnp.sum(dl[0:1] * dl[1:2], axis=1, keepdims=True))
           - jnp.exp(jnp.sum(dl[2:3] * dl[3:4], axis=1, keepdims=True)) + lam_init)
    sg = sg_ref[...] * (1.0 - lam_init)

    def qk_store(buf, qi):
        r0 = pl.multiple_of(qi * TQ, TQ)
        q = q_ref[0, pl.ds(r0, TQ), :] * (HEAD ** -0.5 * LOG2E)
        for hh in range(2):
            qm = jnp.where(lo if hh == 0 else jnp.logical_not(lo), q, 0.0).astype(BF16)
            _store_logits(buf, hh, _dot_nt(qm, k_bf),
                          lambda rb, j: bias_ref[0, n_blk - 1 - (qi * rpb + rb) + j])

    def finish(buf, qi):
        r0 = pl.multiple_of(qi * TQ, TQ)
        o = _softmax_pv(buf, 0, v_ext) - lam * _softmax_pv(buf, 1, v_ext)
        o = o * lax.rsqrt(jnp.mean(o * o, axis=1, keepdims=True) + 1e-5) * sg
        o_ref[0, pl.ds(r0, TQ), :] = o * _silu(g_ref[0, pl.ds(r0, TQ), :])

    _attn_loop(n_blk // rpb, qk_store, finish, s_a, s_b)


def _mixer_d(p, bias, d_lambda, subln_g, layer_idx):
    b, s, _ = p.shape
    n_blk = s // TB
    nd = 2 * n_blk - 1
    lam_init = 0.8 - 0.6 * math.exp(-0.3 * layer_idx)

    def col(off):
        return pl.BlockSpec((1, s, LANES), lambda bi, j, off=off: (bi, 0, off // LANES + j))

    return pl.pallas_call(
        functools.partial(_mixer_d_kernel, n_blk=n_blk, lam_init=lam_init),
        grid=(b, D_HEADS),
        in_specs=[col(D_Q), col(D_K), col(D_V), col(D_G),
                  pl.BlockSpec((1, nd, TB, TB), lambda bi, j: (A_HEADS + j, 0, 0, 0)),
                  pl.BlockSpec((4, HEAD), lambda bi, j: (0, 0)),
                  pl.BlockSpec((1, LANES), lambda bi, j: (0, 0))],
        out_specs=pl.BlockSpec((1, s, LANES), lambda bi, j: (bi, 0, j)),
        out_shape=jax.ShapeDtypeStruct((b, s, D_HEADS * LANES), F32),
        scratch_shapes=_logit_bufs(s),
        compiler_params=_params("parallel", "parallel"),
        name="mixer_d",
    )(p, p, p, p, bias, d_lambda, subln_g.reshape(1, LANES))


def _rope_tables(s):
    t = np.arange(s)
    row, colp = t // GRID_W, t % GRID_W
    qtr = HEAD // 4
    freqs = np.float32(ROPE_THETA) ** (-(np.arange(qtr, dtype=np.float32) / np.float32(qtr)))
    lane = np.arange(LANES) % HEAD
    use_col = (lane // (HEAD // 2)) == 1
    second = (lane % (HEAD // 2)) >= qtr
    pos = np.where(use_col[None, :], colp[:, None], row[:, None]).astype(np.float32)
    ang = (pos * freqs[lane % qtr][None, :]).astype(np.float32)
    cos = np.cos(ang).astype(np.float32)
    sin = np.where(second[None, :], np.sin(ang), -np.sin(ang)).astype(np.float32)
    return jnp.asarray(cos), jnp.asarray(sin)


def _norm_rope(x, gain, cos, sin, lo, first):
    ms = _seg_sum(x * x, lo) * (1.0 / HEAD)
    x = x * lax.rsqrt(ms + 1e-6) * gain
    qtr = HEAD // 4
    partner = jnp.where(first, pltpu.roll(x, LANES - qtr, axis=1), pltpu.roll(x, qtr, axis=1))
    return x * cos + partner * sin


def _mixer_c_kernel(q_ref, k_ref, v_ref, g_ref, cos_ref, sin_ref, qg_ref, kg_ref, o_ref, s_a, s_b, krot_ref,
                    *, n_blk, s):
    pair = pl.program_id(1)
    grp = pair // 2
    lo_s = _lane_lo(s)

    @pl.when(pair == 0)
    def _():
        lane_s = lax.broadcasted_iota(jnp.int32, (s, LANES), 1)
        first_s = (lane_s % (HEAD // 2)) < (HEAD // 4)
        krot_ref[...] = _norm_rope(k_ref[0], kg_ref[...], cos_ref[...], sin_ref[...], lo_s, first_s)

    k = krot_ref[...]
    v = v_ref[0]
    k_sw = pltpu.roll(k, HEAD, axis=1)
    v_sw = pltpu.roll(v, HEAD, axis=1)
    keep = jnp.logical_xor(lo_s, grp == 1)
    k_bf = jnp.where(keep, k, k_sw).astype(BF16)
    v_ext = _with_ones(jnp.where(keep, v, v_sw).astype(BF16))
    lo = _lane_lo(TQ)
    lane = lax.broadcasted_iota(jnp.int32, (TQ, LANES), 1)
    first = (lane % (HEAD // 2)) < (HEAD // 4)

    def qk_store(buf, qi):
        rows = slice(qi * TQ, (qi + 1) * TQ)
        q = _norm_rope(q_ref[0, rows, :], qg_ref[...], cos_ref[rows, :], sin_ref[rows, :], lo, first)
        q = q * (HEAD ** -0.5 * LOG2E)
        for hh in range(2):
            qm = jnp.where(lo if hh == 0 else jnp.logical_not(lo), q, 0.0).astype(BF16)
            _store_logits(buf, hh, _dot_nt(qm, k_bf))

    def finish(buf, qi):
        rows = slice(qi * TQ, (qi + 1) * TQ)
        o = jnp.where(lo, _softmax_pv(buf, 0, v_ext), _softmax_pv(buf, 1, v_ext))
        o_ref[0, rows, :] = o * _silu(g_ref[0, rows, :])

    _attn_windows(n_blk, qk_store, finish, s_a, s_b)


def _mixer_c(p, cos, sin, qn_g, kn_g):
    b, s, _ = p.shape
    n_blk = s // TQ
    n_pairs = 4

    def col(off):
        return pl.BlockSpec((1, s, LANES), lambda bi, j, off=off: (bi, 0, off // LANES + j))

    def fixed(off):
        return pl.BlockSpec((1, s, LANES), lambda bi, j, off=off: (bi, 0, off // LANES))

    tab = pl.BlockSpec((s, LANES), lambda bi, j: (0, 0))
    gain = pl.BlockSpec((1, LANES), lambda bi, j: (0, 0))
    return pl.pallas_call(
        functools.partial(_mixer_c_kernel, n_blk=n_blk, s=s),
        grid=(b, n_pairs),
        in_specs=[col(C_Q), fixed(C_K), fixed(C_V), col(C_G), tab, tab, gain, gain],
        out_specs=pl.BlockSpec((1, s, LANES), lambda bi, j: (bi, 0, j)),
        out_shape=jax.ShapeDtypeStruct((b, s, n_pairs * LANES), F32),
        scratch_shapes=_logit_bufs(s) + [pltpu.VMEM((s, LANES), F32)],
        compiler_params=_params("parallel", "arbitrary"),
        name="mixer_c",
    )(p, p, p, p, cos, sin, jnp.tile(qn_g, 2).reshape(1, LANES), jnp.tile(kn_g, 2).reshape(1, LANES))


def _mixer_m_kernel(q_ref, g_ref, k_ref, v_ref, o_ref, s_a, s_b, *, n_blk):
    k_bf = k_ref[0].astype(BF16)
    v_ext = _with_ones(v_ref[0].astype(BF16))
    lo = _lane_lo(TQ)

    def qk_store(buf, qi):
        rows = slice(qi * TQ, (qi + 1) * TQ)
        q = q_ref[0, rows, :] * (HEAD ** -0.5 * LOG2E)
        for hh in range(2):
            qm = jnp.where(lo if hh == 0 else jnp.logical_not(lo), q, 0.0).astype(BF16)
            _store_logits(buf, hh, _dot_nt(qm, k_bf))

    def finish(buf, qi):
        rows = slice(qi * TQ, (qi + 1) * TQ)
        o = jnp.where(lo, _softmax_pv(buf, 0, v_ext), _softmax_pv(buf, 1, v_ext))
        o_ref[0, rows, :] = o * _silu(g_ref[0, rows, :])

    _attn_windows(n_blk, qk_store, finish, s_a, s_b)


def _mixer_m(p, kv):
    b, s, _ = p.shape
    n_mem = kv.shape[1]
    n_blk = s // TQ
    n_pairs = 2

    def col(off):
        return pl.BlockSpec((1, s, LANES), lambda bi, j, off=off: (bi, 0, off // LANES + j))

    return pl.pallas_call(
        functools.partial(_mixer_m_kernel, n_blk=n_blk),
        grid=(b, n_pairs),
        in_specs=[col(M_Q), col(M_G),
                  pl.BlockSpec((1, n_mem, LANES), lambda bi, j: (bi, 0, j)),
                  pl.BlockSpec((1, n_mem, LANES), lambda bi, j: (bi, 0, n_pairs + j))],
        out_specs=pl.BlockSpec((1, s, LANES), lambda bi, j: (bi, 0, j)),
        out_shape=jax.ShapeDtypeStruct((b, s, n_pairs * LANES), F32),
        scratch_shapes=_logit_bufs(n_mem),
        compiler_params=_params("parallel", "parallel"),
        name="mixer_m",
    )(p, p, kv, kv)


def _split3_dot(tri_bf, x):
    h1 = x.astype(BF16)
    r1 = x - h1.astype(F32)
    h2 = r1.astype(BF16)
    h3 = (r1 - h2.astype(F32)).astype(BF16)
    return _dot(tri_bf, h1) + _dot(tri_bf, h2) + _dot(tri_bf, h3)


def _rwkv_kernel(r_ref, k_ref, v_ref, wa_ref, w0_ref, wup_ref, a0_ref, aup_ref,
                 kk_ref, ka_ref, rk_ref, yf_ref, yb_ref, bvf_ref, bvb_ref, state_ref, *, nb, nch):
    C = CHUNK
    R = nch * C
    c = pl.program_id(1)

    @pl.when(c == 0)
    def _():
        state_ref[...] = jnp.zeros_like(state_ref)

    n_pairs = r_ref.shape[2] // LANES
    width = r_ref.shape[2]
    tt = lax.broadcasted_iota(jnp.int32, (C, LANES), 0)
    ss = lax.broadcasted_iota(jnp.int32, (C, LANES), 1) % C
    eye_bf = jnp.where(tt == ss, 1.0, 0.0).astype(BF16)
    rr = lax.broadcasted_iota(jnp.int32, (LANES, LANES), 0)
    cc = lax.broadcasted_iota(jnp.int32, (LANES, LANES), 1)
    eye = rr == cc
    tr = lax.broadcasted_iota(jnp.int32, (R, R), 0)
    tc = lax.broadcasted_iota(jnp.int32, (R, R), 1)
    same_chunk = (tr // C) == (tc // C)
    lo = _lane_lo(C)
    hi = jnp.logical_not(lo)
    lo_bf = jnp.where(lo, 1.0, 0.0).astype(BF16)
    hi_bf = jnp.where(hi, 1.0, 0.0).astype(BF16)
    lo_r = _lane_lo(R)

    def stack(x):
        return jnp.concatenate([jnp.where(lo, x, 0.0), jnp.where(hi, x, 0.0)], axis=0)

    def stack_bf(x):
        return jnp.concatenate([x * lo_bf, x * hi_bf], axis=0)

    units = []
    for e in range(2):
        rev = e == 1
        blk = (nb - 1 - c) if rev else c
        r0 = pl.multiple_of(blk * R, R)
        bv_ref = bvb_ref if rev else bvf_ref
        strict = (tt < ss) if rev else (tt > ss)
        incl = (tt <= ss) if rev else (tt >= ss)
        tri_bf = jnp.where(jnp.logical_and(same_chunk, (tr <= tc) if rev else (tr >= tc)), 1.0, 0.0).astype(BF16)
        last = 0 if rev else C - 1
        up, dn = (ss, tt) if rev else (tt, ss)
        level_masks = []
        m = 1
        while m < C:
            same_blk = ((tt ^ ss) >> (int(math.log2(m)) + 1)) == 0
            lvl_mask = jnp.logical_and(same_blk, jnp.logical_and((up & m) != 0, (dn & m) == 0))
            level_masks.append(jnp.where(lvl_mask, 1.0, 0.0).astype(BF16))
            m *= 2

        rs = r_ref[0, pl.ds(r0, R), :]
        ks = k_ref[0, pl.ds(r0, R), :]
        vs = v_ref[0, pl.ds(r0, R), :]
        wa = wa_ref[0, pl.ds(r0, R), :]
        sel = lo_r if e == 0 else jnp.logical_not(lo_r)
        wd = jnp.where(sel, jnp.tanh(wa[:, :LANES]), 0.0).astype(BF16)
        ad = jnp.where(sel, wa[:, LANES:], 0.0).astype(BF16)
        zw = w0_ref[e:e + 1, :] + _dot(wd, wup_ref[...])
        lw = -math.exp(-0.5) / (1.0 + jnp.exp(-zw))
        za = a0_ref[e:e + 1, :] + _dot(ad, aup_ref[...])
        a_sig = 1.0 / (1.0 + jnp.exp(-za))
        kk = ks * kk_ref[...]
        kk2 = kk * kk
        nrm2 = jnp.concatenate([_seg_sum(kk2[:, i * LANES:(i + 1) * LANES], lo_r) for i in range(n_pairs)], axis=1)
        kkn = kk / jnp.maximum(jnp.sqrt(nrm2), 1e-12)
        ka = ka_ref[...]
        ke = ks * ((1.0 - ka) + ka * a_sig)
        be = kkn * a_sig
        rkr = rs * ke * rk_ref[...]
        bonus = jnp.concatenate([_seg_sum(rkr[:, i * LANES:(i + 1) * LANES], lo_r) for i in range(n_pairs)], axis=1)
        bv_ref[0] = bonus * vs

        l_incl = _split3_dot(tri_bf, lw)
        l_tot = jnp.concatenate(
            [jnp.broadcast_to(l_incl[j * C + last:j * C + last + 1, :], (C, width)) for j in range(nch)], axis=0)
        w_inv = jnp.exp(-l_incl)
        d_end = jnp.exp(l_tot)
        w_end = d_end * w_inv
        at = -kkn * jnp.exp(l_incl - lw)
        rt = rs * jnp.exp(l_incl)
        bt = be * w_inv
        kt = ke * w_inv
        bh = be * w_end
        kh = ke * w_end
        for j in range(nch):
            rows = slice(j * C, (j + 1) * C)
            for pr in range(n_pairs):
                sl = slice(pr * LANES, (pr + 1) * LANES)
                units.append(dict(
                    e=e, j=j, pr=pr, strict=strict, incl=incl, levels=level_masks,
                    at=at[rows, sl], rt=rt[rows, sl], bt=bt[rows, sl], kt=kt[rows, sl],
                    bh=bh[rows, sl], kh=kh[rows, sl], v=vs[rows, sl], d_end=d_end[j * C:j * C + 1, sl]))

    for u in units:
        u["at_bf"] = u["at"].astype(BF16)
        u["v_st"] = stack_bf(u["v"].astype(BF16))
        lhs = jnp.concatenate([u["at_bf"], u["rt"].astype(BF16)], axis=0)
        rhs = jnp.concatenate([stack_bf(u["bt"].astype(BF16)), stack_bf(u["kt"].astype(BF16))], axis=0)
        g = _dot_nt(lhs, rhs)
        u["a_ab"] = jnp.where(u["strict"], g[:C, :LANES], 0.0).astype(BF16)
        u["a_ak"] = jnp.where(u["strict"], g[:C, LANES:], 0.0).astype(BF16)
        u["ly"] = jnp.concatenate([jnp.where(u["incl"], g[C:, :LANES], 0.0),
                                   jnp.where(u["incl"], g[C:, LANES:], 0.0)], axis=1).astype(BF16)
        u["t"] = eye_bf + u["a_ab"] * u["levels"][0]
    for lvl in range(1, len(units[0]["levels"])):
        for u in units:
            u["tmp"] = _dot(u["a_ab"] * u["levels"][lvl], stack_bf(u["t"])).astype(BF16)
        for u in units:
            u["t"] = u["t"] + _dot(u["t"], stack_bf(u["tmp"])).astype(BF16)
    for u in units:
        u["akv"] = _dot(u["a_ak"], u["v_st"]).astype(BF16)
    for u in units:
        x = jnp.concatenate([stack_bf(u["at_bf"]), stack_bf(u["akv"])], axis=1)
        pq = _dot(u["t"], x).astype(BF16)
        u["ry"] = jnp.concatenate(
            [jnp.concatenate([stack_bf(pq[:, :LANES]), stack_bf(pq[:, LANES:])], axis=1),
             jnp.concatenate([jnp.zeros((LANES, LANES), BF16), u["v_st"]], axis=1)], axis=0)
        ls = jnp.concatenate([stack(u["bh"]), stack(u["kh"])], axis=0)
        u["lyz"] = jnp.concatenate([u["ly"], ls.T.astype(BF16)], axis=0)
    for u in units:
        yz = _dot(u["lyz"], u["ry"])
        yy = yz[:C]
        zz = yz[C:]
        u["rpm"] = jnp.concatenate([u["rt"] + yy[:, :LANES], zz[:, :LANES]], axis=0).astype(BF16)
        u["y0"] = yy[:, LANES:]
        u["z"] = zz[:, LANES:]
        d_diag = jnp.where(eye, jnp.broadcast_to(u["d_end"], (LANES, LANES)), 0.0)
        u["d_col"] = jnp.sum(d_diag, axis=1, keepdims=True)

    by_key = {(u["e"], u["j"], u["pr"]): u for u in units}
    for e in range(2):
        y_ref = yb_ref if e == 1 else yf_ref
        order = list(range(nch))[::-1] if e == 1 else list(range(nch))
        ys = {}
        states = [state_ref[e, pr] for pr in range(n_pairs)]
        for j in order:
            for pr in range(n_pairs):
                u = by_key[(e, j, pr)]
                ym = _dot(u["rpm"], states[pr].astype(BF16))
                ys[(j, pr)] = ym[:C] + u["y0"]
                states[pr] = u["d_col"] * states[pr] + ym[C:] + u["z"]
        for pr in range(n_pairs):
            state_ref[e, pr] = states[pr]
        y_ref[0] = jnp.concatenate(
            [jnp.concatenate([ys[(j, pr)] for pr in range(n_pairs)], axis=1) for j in range(nch)], axis=0)


RWKV_CHUNKS_PER_STEP = 4


def _mixer_b_scan(p, w0, w_up, a0, a_up, k_k, k_a, r_k):
    b, s, _ = p.shape
    width = 512
    nch = RWKV_CHUNKS_PER_STEP
    rows = nch * CHUNK
    nb = s // rows
    n_pairs = width // LANES

    def seq(off, w):
        return pl.BlockSpec((1, s, w), lambda bi, c, off=off, w=w: (bi, 0, off // w))

    def full(shape):
        return pl.BlockSpec(shape, lambda bi, c: tuple(0 for _ in shape))

    out_f = pl.BlockSpec((1, rows, width), lambda bi, c: (bi, c, 0))
    out_b = pl.BlockSpec((1, rows, width), lambda bi, c: (bi, nb - 1 - c, 0))
    sds = jax.ShapeDtypeStruct((b, s, width), F32)
    lora = 2 * HEAD
    return pl.pallas_call(
        functools.partial(_rwkv_kernel, nb=nb, nch=nch),
        grid=(b, nb),
        in_specs=[seq(B_R, width), seq(B_K, width), seq(B_V, width), seq(B_WA, 2 * LANES),
                  full((2, width)), full((lora, width)),
                  full((2, width)), full((lora, width)), full((1, width)), full((1, width)), full((1, width))],
        out_specs=[out_f, out_b, out_f, out_b],
        out_shape=[sds, sds, sds, sds],
        scratch_shapes=[pltpu.VMEM((2, n_pairs, LANES, LANES), F32)],
        compiler_params=_params("parallel", "arbitrary"),
        name="mixer_b",
    )(p, p, p, p, w0, w_up.reshape(lora, width).astype(BF16), a0, a_up.reshape(lora, width).astype(BF16),
      k_k.reshape(1, width), k_a.reshape(1, width), r_k.reshape(1, width))


def _merge_kernel(h_ref, hb_ref, oa_ref, yf_ref, yb_ref, bvf_ref, bvb_ref, gb0_ref, gb1_ref, oc_ref, od_ref,
                  om_ref, wg_ref, bg_ref, wb_ref, wo_ref, lng_ref, lnb_ref, bg_g_ref, bg_b_ref,
                  hn_ref, hnb_ref, *, alpha):
    d = h_ref.shape[1]
    tm = h_ref.shape[0]
    lo = _lane_lo(tm)
    y = yf_ref[...] + yb_ref[...]
    n_pairs = y.shape[1] // LANES
    gn = []
    for i in range(n_pairs):
        yp = y[:, i * LANES:(i + 1) * LANES]
        mu = _seg_sum(yp, lo) * (1.0 / HEAD)
        dy = yp - mu
        var = _seg_sum(dy * dy, lo) * (1.0 / HEAD)
        gn.append(dy * lax.rsqrt(var + B_GN_EPS))
    gn = jnp.concatenate(gn, axis=1) * bg_g_ref[...] + bg_b_ref[...]
    gb = jnp.concatenate([gb0_ref[...], gb1_ref[...]], axis=1)
    ob = (gn + bvf_ref[...] + bvb_ref[...]) * _silu(gb)

    hb = hb_ref[...]
    branches = (oa_ref[...], ob, oc_ref[...], od_ref[...], om_ref[...])
    acc = None
    row = 0
    for i, o in enumerate(branches):
        wdt = o.shape[1]
        gate = 1.0 / (1.0 + jnp.exp(-(_dot(hb, wg_ref[:, i * d:(i + 1) * d]) + bg_ref[:, i * d:(i + 1) * d])))
        proj = _dot(o.astype(BF16), wb_ref[row:row + wdt, :])
        term = gate * proj
        acc = term if acc is None else acc + term
        row += wdt
    out = _dot(acc.astype(BF16), wo_ref[...])
    z = alpha * h_ref[...] + out
    mu = jnp.mean(z, -1, keepdims=True)
    dz = z - mu
    var = jnp.mean(dz * dz, -1, keepdims=True)
    hn = dz * lax.rsqrt(var + 1e-5) * lng_ref[...] + lnb_ref[...]
    hn_ref[...] = hn
    hnb_ref[...] = hn.astype(BF16)


def _merge(h, hb, o_a, yf, yb, bvf, bvb, p2, o_c, o_d, o_m, layer, w_gate, b_gate, w_branch, w_out, ln_g, ln_b,
           bln_g, bln_b):
    n, d = h.shape
    tm = 256
    alpha = (2 * DEPTH) ** 0.25

    def rows(w):
        return pl.BlockSpec((tm, w), lambda i: (i, 0))

    def full(shape):
        return pl.BlockSpec(shape, lambda i: tuple(0 for _ in shape))

    def of_layer(w):
        return pl.BlockSpec((None,) + w.shape[1:], lambda i: (layer, 0, 0))

    gb_specs = pl.BlockSpec((tm, 256), lambda i: (i, B_G // 256))
    gb_specs2 = pl.BlockSpec((tm, 256), lambda i: (i, B_G // 256 + 1))
    return pl.pallas_call(
        functools.partial(_merge_kernel, alpha=alpha),
        grid=(n // tm,),
        in_specs=[rows(d), rows(d), rows(512), rows(512), rows(512), rows(512), rows(512),
                  gb_specs, gb_specs2, rows(512), rows(512), rows(256),
                  of_layer(w_gate), full((1, b_gate.shape[0])), of_layer(w_branch), of_layer(w_out),
                  full((1, d)), full((1, d)), full((1, 512)), full((1, 512))],
        out_specs=[rows(d), rows(d)],
        out_shape=[jax.ShapeDtypeStruct((n, d), F32), jax.ShapeDtypeStruct((n, d), BF16)],
        compiler_params=_params("parallel"),
        name="merge",
    )(h, hb, o_a, yf, yb, bvf, bvb, p2, p2, o_c, o_d, o_m,
      w_gate, b_gate.reshape(1, -1), w_branch, w_out, ln_g.reshape(1, d), ln_b.reshape(1, d),
      bln_g.reshape(1, 512), bln_b.reshape(1, 512))


def kernel(x, mem, ln_in_g, ln_in_b, rel_bias, w_in, shift_mu, rwkv_w0, rwkv_w_up, rwkv_a0, rwkv_a_up,
           rwkv_k_k, rwkv_k_a, rwkv_r_k, rwkv_ln_g, rwkv_ln_b, c_qnorm_g, c_knorm_g, d_lambda, d_subln_g,
           w_mem_kv, w_branch, w_gate, b_gate, w_out, ln_g, ln_b):
    b, s, d = x.shape
    n = b * s
    bias = _bias_tiles(rel_bias, s // TB)
    cos, sin = _rope_tables(s)
    mem_bf = mem.reshape(b * mem.shape[1], d).astype(BF16)
    h, hb = _ln_in(x.reshape(n, d), ln_in_g, ln_in_b)
    w_gate, w_branch, w_out = (w.astype(BF16) for w in (w_gate, w_branch, w_out))
    for l in range(DEPTH):
        p2 = _proj_in(hb, w_in, l, shift_mu[l], s)
        p = p2.reshape(b, s, IN_COLS)
        kv = _matmul(mem_bf, w_mem_kv, l, 512, 512, "proj_mem").reshape(b, mem.shape[1], -1)
        o_a = _mixer_a(p, bias)
        yf, yb, bvf, bvb = _mixer_b_scan(p, rwkv_w0[l], rwkv_w_up[l], rwkv_a0[l], rwkv_a_up[l],
                                         rwkv_k_k[l], rwkv_k_a[l], rwkv_r_k[l])
        o_c = _mixer_c(p, cos, sin, c_qnorm_g[l], c_knorm_g[l])
        o_d = _mixer_d(p, bias, d_lambda[l], d_subln_g[l], l)
        o_m = _mixer_m(p, kv)
        flat = lambda t: t.reshape(n, t.shape[-1])
        h, hb = _merge(h, hb, flat(o_a), flat(yf), flat(yb), flat(bvf), flat(bvb), p2, flat(o_c), flat(o_d),
                       flat(o_m), l, w_gate, b_gate[l], w_branch, w_out, ln_g[l], ln_b[l],
                       rwkv_ln_g[l], rwkv_ln_b[l])
    return h.reshape(b, s, d)
```

```python
import functools
import math

import numpy as np
import jax
import jax.numpy as jnp
from jax import lax
from jax.experimental import pallas as pl
from jax.experimental.pallas import tpu as pltpu

F32 = jnp.float32
BF16 = jnp.bfloat16

LANES = 128
SUBLANES = 8
HEAD = 64
VMEM_LIMIT = 56 * 1024 * 1024

DEPTH = 2
GRID_W = 64
ROPE_THETA = 10000.0
NUM_BUCKETS = 32
REL_MAX_DISTANCE = 1024
A_HEADS = 8
D_HEADS = 4
A_GROUPS = ((128, 1), (512, 4), (2048, 16))
B_GN_EPS = 64e-5
NEG_INF = -1e30

A_Q, A_K, A_V, A_G = 0, 512, 1024, 1536
B_R, B_K, B_V, B_WA, B_G = 2048, 2560, 3072, 3584, 3840
C_Q, C_K, C_V, C_G = 4352, 4864, 4992, 5120
D_Q, D_K, D_V, D_G = 5632, 6144, 6656, 7168
M_Q, M_G = 7680, 7936
IN_COLS = 8192

TB = 128
TQ = 256
LOG2E = math.log2(math.e)
CHUNK = 64
LN_ROWS = 512
MERGE_ROWS = 256
PROJ_COLS = 2048
SHIFT_COLS = 256
MEM_TILE = 512


def _dot(a, b):
    return jnp.dot(a, b, preferred_element_type=F32)


def _dot_nt(a, b):
    return lax.dot_general(a, b, (((1,), (1,)), ((), ())), preferred_element_type=F32)


def _silu(g):
    return g / (1.0 + jnp.exp(-g))


def _params(*sem):
    return pltpu.CompilerParams(dimension_semantics=sem, vmem_limit_bytes=VMEM_LIMIT)


def _ln_in_kernel(x_ref, g_ref, b_ref, h_ref, hb_ref):
    x = x_ref[...]
    mu = jnp.mean(x, -1, keepdims=True)
    d = x - mu
    var = jnp.mean(d * d, -1, keepdims=True)
    h = d * lax.rsqrt(var + 1e-5) * g_ref[...] + b_ref[...]
    h_ref[...] = h
    hb_ref[...] = h.astype(BF16)


def _ln_in(x2, g, b):
    n, d = x2.shape
    tm = LN_ROWS
    return pl.pallas_call(
        _ln_in_kernel,
        grid=(n // tm,),
        in_specs=[pl.BlockSpec((tm, d), lambda i: (i, 0)),
                  pl.BlockSpec((1, d), lambda i: (0, 0)),
                  pl.BlockSpec((1, d), lambda i: (0, 0))],
        out_specs=[pl.BlockSpec((tm, d), lambda i: (i, 0)),
                   pl.BlockSpec((tm, d), lambda i: (i, 0))],
        out_shape=[jax.ShapeDtypeStruct((n, d), F32), jax.ShapeDtypeStruct((n, d), BF16)],
        compiler_params=_params("parallel"),
        name="ln_in",
    )(x2, g.reshape(1, d), b.reshape(1, d))


def _matmul_kernel(a_ref, w_ref, o_ref, wbf_ref):
    @pl.when(pl.program_id(1) == 0)
    def _():
        wbf_ref[...] = w_ref[...].astype(BF16)

    o_ref[...] = _dot(a_ref[...], wbf_ref[...])


def _matmul(a, w, layer, tm, tn, name):
    m, k = a.shape
    n = w.shape[2]
    return pl.pallas_call(
        _matmul_kernel,
        grid=(n // tn, m // tm),
        in_specs=[pl.BlockSpec((tm, k), lambda j, i: (i, 0)),
                  pl.BlockSpec((None, k, tn), lambda j, i: (layer, 0, j))],
        out_specs=pl.BlockSpec((tm, tn), lambda j, i: (i, j)),
        out_shape=jax.ShapeDtypeStruct((m, n), F32),
        scratch_shapes=[pltpu.VMEM((k, tn), BF16)],
        compiler_params=_params("parallel", "arbitrary"),
        name=name,
    )(a, w)


HALO = 16


def _proj_in_kernel(a_ref, halo_ref, w_ref, mu_ref, o_ref, wbf_ref, sh_ref, *, shift_block, n_shift, sub):
    j = pl.program_id(0)
    i = pl.program_id(1)

    @pl.when(i == 0)
    def _():
        wbf_ref[...] = w_ref[...].astype(BF16)

    @pl.when(j != shift_block)
    def _():
        o_ref[...] = _dot(a_ref[...], wbf_ref[...])

    @pl.when(j == shift_block)
    def _():
        tm, tn = o_ref.shape
        first_half = (i % 2) == 0
        for cb in range(tn // sub):
            cols = slice(cb * sub, (cb + 1) * sub)
            if cols.start >= n_shift:
                o_ref[:, cols] = _dot(a_ref[...], wbf_ref[:, cols])
                continue
            xh = _dot(jnp.concatenate([a_ref[...], halo_ref[...]], axis=0), wbf_ref[:, cols])
            x = xh[:tm]
            prev_row = jnp.where(first_half, 0.0, xh[tm + HALO - 1:tm + HALO])
            next_row = jnp.where(first_half, xh[tm:tm + 1], 0.0)
            at = SUBLANES
            sh_ref[at - 1:at, :] = prev_row
            sh_ref[at:at + tm, :] = x
            sh_ref[at + tm:at + tm + 1, :] = next_row
            prev = sh_ref[at - 1:at - 1 + tm, :]
            nxt = sh_ref[at + 1:at + 1 + tm, :]
            mu0 = mu_ref[0:1, cols]
            mu1 = mu_ref[1:2, cols]
            o_ref[:, cols] = x * (1.0 - mu0 - mu1) + mu0 * prev + mu1 * nxt


def _proj_in(a, w, layer, mu, seq):
    m, k = a.shape
    n = w.shape[2]
    tn = PROJ_COLS
    tm = seq // 2
    assert B_R % tn == 0 and B_G < B_R + tn, "the RWKV mix columns must sit inside one column block"
    mu_pad = jnp.zeros((2, tn), F32).at[:, :mu.shape[1]].set(mu)
    halo_rows = tm // HALO

    def halo_index(j, i):
        return ((i + 1 - i % 2) * halo_rows - i % 2, 0)

    return pl.pallas_call(
        functools.partial(_proj_in_kernel, shift_block=B_R // tn, n_shift=mu.shape[1], sub=SHIFT_COLS),
        grid=(n // tn, m // tm),
        in_specs=[pl.BlockSpec((tm, k), lambda j, i: (i, 0)),
                  pl.BlockSpec((HALO, k), halo_index),
                  pl.BlockSpec((None, k, tn), lambda j, i: (layer, 0, j)),
                  pl.BlockSpec((2, tn), lambda j, i: (0, 0))],
        out_specs=pl.BlockSpec((tm, tn), lambda j, i: (i, j)),
        out_shape=jax.ShapeDtypeStruct((m, n), F32),
        scratch_shapes=[pltpu.VMEM((k, tn), BF16), pltpu.VMEM((tm + 2 * SUBLANES, SHIFT_COLS), F32)],
        compiler_params=_params("parallel", "arbitrary"),
        name="proj_in",
    )(a, a, w, mu_pad)


def _rel_bucket_np(rel):
    nb = NUM_BUCKETS // 2
    max_exact = nb // 2
    n = np.abs(rel)
    nf = np.maximum(n, 1).astype(np.float32)
    large = max_exact + (np.log(nf / np.float32(max_exact)) / np.float32(math.log(REL_MAX_DISTANCE / max_exact))
                         * np.float32(nb - max_exact)).astype(np.int32)
    large = np.minimum(large, nb - 1)
    return (np.where(rel > 0, nb, 0) + np.where(n < max_exact, n, large)).astype(np.int32)


def _tile_deltas(n_blk):
    d = np.arange(2 * n_blk - 1)[:, None, None] - (n_blk - 1)
    r = np.arange(TB)[None, :, None]
    c = np.arange(TB)[None, None, :]
    return d * TB + c - r


def _dilated_log_multiplicity(delta):
    mult = np.zeros(delta.shape, np.float32)
    for window, dil in A_GROUPS:
        mult += ((delta % dil == 0) & (np.abs(delta) <= window // 2)).astype(np.float32)
    with np.errstate(divide="ignore"):
        return np.where(mult > 0, np.log(np.maximum(mult, 1.0)), NEG_INF).astype(np.float32)


def _bias_kernel(table_ref, bucket_ref, base_ref, o_ref, *, n_a, tile_bucket):
    h = pl.program_id(0)
    for d, const in enumerate(tile_bucket):
        acc = jnp.where(h < n_a, base_ref[d], 0.0)
        if const is not None:
            acc = acc + table_ref[const, h]
        else:
            bucket = bucket_ref[d]
            for b in range(NUM_BUCKETS):
                acc = acc + jnp.where(bucket == b, table_ref[b, h], 0.0)
        o_ref[0, d] = acc * LOG2E


def _bias_tiles(rel_bias, n_blk):
    delta = _tile_deltas(n_blk)
    bucket_np = _rel_bucket_np(delta)
    tile_bucket = tuple(int(t.flat[0]) if (t == t.flat[0]).all() else None for t in bucket_np)
    bucket = jnp.asarray(bucket_np)
    base = jnp.asarray(_dilated_log_multiplicity(delta))
    n_heads = rel_bias.shape[1]
    nd = 2 * n_blk - 1
    return pl.pallas_call(
        functools.partial(_bias_kernel, n_a=A_HEADS, tile_bucket=tile_bucket),
        grid=(n_heads,),
        in_specs=[pl.BlockSpec(memory_space=pltpu.SMEM),
                  pl.BlockSpec((nd, TB, TB), lambda h: (0, 0, 0)),
                  pl.BlockSpec((nd, TB, TB), lambda h: (0, 0, 0))],
        out_specs=pl.BlockSpec((1, nd, TB, TB), lambda h: (h, 0, 0, 0)),
        out_shape=jax.ShapeDtypeStruct((n_heads, nd, TB, TB), F32),
        compiler_params=_params("arbitrary"),
        name="bias_tiles",
    )(rel_bias, bucket, base)


def _lane_lo(rows):
    return lax.broadcasted_iota(jnp.int32, (rows, LANES), 1) < HEAD


def _with_ones(v_bf):
    return jnp.concatenate([v_bf, jnp.ones(v_bf.shape, BF16)], axis=1)


def _store_logits(buf, hh, s, bias_fn=None):
    rows, keys = s.shape
    for rb in range(rows // TB):
        for j in range(keys // LANES):
            c = s[rb * TB:(rb + 1) * TB, j * LANES:(j + 1) * LANES]
            if bias_fn is not None:
                c = c + bias_fn(rb, j)
            buf[hh, rb * TB:(rb + 1) * TB, j * LANES:(j + 1) * LANES] = c


def _softmax_pv(buf, hh, v_ext):
    rows = buf.shape[1]
    keys = v_ext.shape[0]
    es = []
    for rb in range(rows // TB):
        def chunk(j):
            return buf[hh, rb * TB:(rb + 1) * TB, j * LANES:(j + 1) * LANES]
        m = chunk(0)
        for j in range(1, keys // LANES):
            m = jnp.maximum(m, chunk(j))
        m = jnp.max(m, axis=1, keepdims=True)
        es.append(jnp.concatenate([jnp.exp2(chunk(j) - m).astype(BF16) for j in range(keys // LANES)], axis=1))
    o = _dot(jnp.concatenate(es, axis=0), v_ext)
    return o[:, :LANES] / o[:, LANES:]


def _attn_loop(n_q, qk_store, finish, s_a, s_b):
    qk_store(s_a, 0)

    def body(i, carry):
        q0 = 2 * i
        qk_store(s_b, q0 + 1)
        finish(s_a, q0)
        qk_store(s_a, q0 + 2)
        finish(s_b, q0 + 1)
        return carry

    lax.fori_loop(0, n_q // 2 - 1, body, 0)
    qk_store(s_b, n_q - 1)
    finish(s_a, n_q - 2)
    finish(s_b, n_q - 1)


def _attn_windows(n_q, qk_store, finish, s_a, s_b):
    bufs = (s_a, s_b)
    qk_store(bufs[0], 0)
    for qi in range(n_q):
        if qi + 1 < n_q:
            qk_store(bufs[(qi + 1) % 2], qi + 1)
        finish(bufs[qi % 2], qi)


def _logit_bufs(keys):
    return [pltpu.VMEM((2, TQ, keys), F32), pltpu.VMEM((2, TQ, keys), F32)]


def _seg_sum(x, lo):
    s0 = jnp.sum(jnp.where(lo, x, 0.0), axis=1, keepdims=True)
    s1 = jnp.sum(jnp.where(lo, 0.0, x), axis=1, keepdims=True)
    return jnp.where(lo, s0, s1)


def _mixer_a_kernel(q_ref, k_ref, v_ref, g_ref, bias_ref, o_ref, s_a, s_b, *, n_blk):
    k_bf = k_ref[0].astype(BF16)
    v_ext = _with_ones(v_ref[0].astype(BF16))
    lo = _lane_lo(TQ)
    rpb = TQ // TB
    n_q = n_blk // rpb
    reach = -(-max(w // 2 for w, _ in A_GROUPS) // TQ)

    def window(qi):
        return max(0, qi - reach), min(n_q, qi + reach + 1)

    def qk_store(buf, qi):
        k0, k1 = window(qi)
        q = q_ref[0, qi * TQ:(qi + 1) * TQ, :] * (HEAD ** -0.5 * LOG2E)
        for hh in range(2):
            qm = jnp.where(lo if hh == 0 else jnp.logical_not(lo), q, 0.0).astype(BF16)
            _store_logits(buf, hh, _dot_nt(qm, k_bf[k0 * TQ:k1 * TQ]),
                          lambda rb, j, hh=hh: bias_ref[hh, n_blk - 1 - (qi * rpb + rb) + k0 * rpb + j])

    def finish(buf, qi):
        k0, k1 = window(qi)
        v_w = v_ext[k0 * TQ:k1 * TQ]
        o = jnp.where(lo, _softmax_pv(buf, 0, v_w), _softmax_pv(buf, 1, v_w))
        o_ref[0, qi * TQ:(qi + 1) * TQ, :] = o * _silu(g_ref[0, qi * TQ:(qi + 1) * TQ, :])

    _attn_windows(n_q, qk_store, finish, s_a, s_b)


def _mixer_a(p, bias):
    b, s, _ = p.shape
    n_blk = s // TB
    nd = 2 * n_blk - 1
    n_pairs = A_HEADS // 2

    def col(off):
        return pl.BlockSpec((1, s, LANES), lambda bi, j, off=off: (bi, 0, off // LANES + j))

    return pl.pallas_call(
        functools.partial(_mixer_a_kernel, n_blk=n_blk),
        grid=(b, n_pairs),
        in_specs=[col(A_Q), col(A_K), col(A_V), col(A_G),
                  pl.BlockSpec((2, nd, TB, TB), lambda bi, j: (j, 0, 0, 0))],
        out_specs=pl.BlockSpec((1, s, LANES), lambda bi, j: (bi, 0, j)),
        out_shape=jax.ShapeDtypeStruct((b, s, n_pairs * LANES), F32),
        scratch_shapes=_logit_bufs(s),
        compiler_params=_params("parallel", "parallel"),
        name="mixer_a",
    )(p, p, p, p, bias)


def _mixer_d_kernel(q_ref, k_ref, v_ref, g_ref, bias_ref, lam_ref, sg_ref, o_ref, s_a, s_b, *, n_blk, lam_init):
    k_bf = k_ref[0].astype(BF16)
    v_ext = _with_ones(v_ref[0].astype(BF16))
    lo = _lane_lo(TQ)
    rpb = TQ // TB
    dl = lam_ref[...]
    lam = (jnp.exp(jnp.sum(dl[0:1] * dl[1:2], axis=1, keepdims=True))
           - jnp.exp(jnp.sum(dl[2:3] * dl[3:4], axis=1, keepdims=True)) + lam_init)
    sg = sg_ref[...] * (1.0 - lam_init)

    def qk_store(buf, qi):
        r0 = pl.multiple_of(qi * TQ, TQ)
        q = q_ref[0, pl.ds(r0, TQ), :] * (HEAD ** -0.5 * LOG2E)
        for hh in range(2):
            qm = jnp.where(lo if hh == 0 else jnp.logical_not(lo), q, 0.0).astype(BF16)
            _store_logits(buf, hh, _dot_nt(qm, k_bf),
                          lambda rb, j: bias_ref[0, n_blk - 1 - (qi * rpb + rb) + j])

    def finish(buf, qi):
        r0 = pl.multiple_of(qi * TQ, TQ)
        o = _softmax_pv(buf, 0, v_ext) - lam * _softmax_pv(buf, 1, v_ext)
        o = o * lax.rsqrt(jnp.mean(o * o, axis=1, keepdims=True) + 1e-5) * sg
        o_ref[0, pl.ds(r0, TQ), :] = o * _silu(g_ref[0, pl.ds(r0, TQ), :])

    _attn_loop(n_blk // rpb, qk_store, finish, s_a, s_b)


def _mixer_d(p, bias, d_lambda, subln_g, layer_idx):
    b, s, _ = p.shape
    n_blk = s // TB
    nd = 2 * n_blk - 1
    lam_init = 0.8 - 0.6 * math.exp(-0.3 * layer_idx)

    def col(off):
        return pl.BlockSpec((1, s, LANES), lambda bi, j, off=off: (bi, 0, off // LANES + j))

    return pl.pallas_call(
        functools.partial(_mixer_d_kernel, n_blk=n_blk, lam_init=lam_init),
        grid=(b, D_HEADS),
        in_specs=[col(D_Q), col(D_K), col(D_V), col(D_G),
                  pl.BlockSpec((1, nd, TB, TB), lambda bi, j: (A_HEADS + j, 0, 0, 0)),
                  pl.BlockSpec((4, HEAD), lambda bi, j: (0, 0)),
                  pl.BlockSpec((1, LANES), lambda bi, j: (0, 0))],
        out_specs=pl.BlockSpec((1, s, LANES), lambda bi, j: (bi, 0, j)),
        out_shape=jax.ShapeDtypeStruct((b, s, D_HEADS * LANES), F32),
        scratch_shapes=_logit_bufs(s),
        compiler_params=_params("parallel", "parallel"),
        name="mixer_d",
    )(p, p, p, p, bias, d_lambda, subln_g.reshape(1, LANES))


def _rope_tables(s):
    t = np.arange(s)
    row, colp = t // GRID_W, t % GRID_W
    qtr = HEAD // 4
    freqs = np.float32(ROPE_THETA) ** (-(np.arange(qtr, dtype=np.float32) / np.float32(qtr)))
    lane = np.arange(LANES) % HEAD
    use_col = (lane // (HEAD // 2)) == 1
    second = (lane % (HEAD // 2)) >= qtr
    pos = np.where(use_col[None, :], colp[:, None], row[:, None]).astype(np.float32)
    ang = (pos * freqs[lane % qtr][None, :]).astype(np.float32)
    cos = np.cos(ang).astype(np.float32)
    sin = np.where(second[None, :], np.sin(ang), -np.sin(ang)).astype(np.float32)
    return jnp.asarray(cos), jnp.asarray(sin)


def _norm_rope(x, gain, cos, sin, lo, first):
    ms = _seg_sum(x * x, lo) * (1.0 / HEAD)
    x = x * lax.rsqrt(ms + 1e-6) * gain
    qtr = HEAD // 4
    partner = jnp.where(first, pltpu.roll(x, LANES - qtr, axis=1), pltpu.roll(x, qtr, axis=1))
    return x * cos + partner * sin


def _mixer_c_kernel(q_ref, k_ref, v_ref, g_ref, cos_ref, sin_ref, qg_ref, kg_ref, o_ref, s_a, s_b, krot_ref,
                    *, n_blk, s):
    pair = pl.program_id(1)
    grp = pair // 2
    lo_s = _lane_lo(s)

    @pl.when(pair == 0)
    def _():
        lane_s = lax.broadcasted_iota(jnp.int32, (s, LANES), 1)
        first_s = (lane_s % (HEAD // 2)) < (HEAD // 4)
        krot_ref[...] = _norm_rope(k_ref[0], kg_ref[...], cos_ref[...], sin_ref[...], lo_s, first_s)

    k = krot_ref[...]
    v = v_ref[0]
    k_sw = pltpu.roll(k, HEAD, axis=1)
    v_sw = pltpu.roll(v, HEAD, axis=1)
    keep = jnp.logical_xor(lo_s, grp == 1)
    k_bf = jnp.where(keep, k, k_sw).astype(BF16)
    v_ext = _with_ones(jnp.where(keep, v, v_sw).astype(BF16))
    lo = _lane_lo(TQ)
    lane = lax.broadcasted_iota(jnp.int32, (TQ, LANES), 1)
    first = (lane % (HEAD // 2)) < (HEAD // 4)

    def qk_store(buf, qi):
        rows = slice(qi * TQ, (qi + 1) * TQ)
        q = _norm_rope(q_ref[0, rows, :], qg_ref[...], cos_ref[rows, :], sin_ref[rows, :], lo, first)
        q = q * (HEAD ** -0.5 * LOG2E)
        for hh in range(2):
            qm = jnp.where(lo if hh == 0 else jnp.logical_not(lo), q, 0.0).astype(BF16)
            _store_logits(buf, hh, _dot_nt(qm, k_bf))

    def finish(buf, qi):
        rows = slice(qi * TQ, (qi + 1) * TQ)
        o = jnp.where(lo, _softmax_pv(buf, 0, v_ext), _softmax_pv(buf, 1, v_ext))
        o_ref[0, rows, :] = o * _silu(g_ref[0, rows, :])

    _attn_windows(n_blk, qk_store, finish, s_a, s_b)


def _mixer_c(p, cos, sin, qn_g, kn_g):
    b, s, _ = p.shape
    n_blk = s // TQ
    n_pairs = 4

    def col(off):
        return pl.BlockSpec((1, s, LANES), lambda bi, j, off=off: (bi, 0, off // LANES + j))

    def fixed(off):
        return pl.BlockSpec((1, s, LANES), lambda bi, j, off=off: (bi, 0, off // LANES))

    tab = pl.BlockSpec((s, LANES), lambda bi, j: (0, 0))
    gain = pl.BlockSpec((1, LANES), lambda bi, j: (0, 0))
    return pl.pallas_call(
        functools.partial(_mixer_c_kernel, n_blk=n_blk, s=s),
        grid=(b, n_pairs),
        in_specs=[col(C_Q), fixed(C_K), fixed(C_V), col(C_G), tab, tab, gain, gain],
        out_specs=pl.BlockSpec((1, s, LANES), lambda bi, j: (bi, 0, j)),
        out_shape=jax.ShapeDtypeStruct((b, s, n_pairs * LANES), F32),
        scratch_shapes=_logit_bufs(s) + [pltpu.VMEM((s, LANES), F32)],
        compiler_params=_params("parallel", "arbitrary"),
        name="mixer_c",
    )(p, p, p, p, cos, sin, jnp.tile(qn_g, 2).reshape(1, LANES), jnp.tile(kn_g, 2).reshape(1, LANES))


def _mixer_m_kernel(q_ref, g_ref, k_ref, v_ref, o_ref, s_a, s_b, *, n_blk):
    k_bf = k_ref[0].astype(BF16)
    v_ext = _with_ones(v_ref[0].astype(BF16))
    lo = _lane_lo(TQ)

    def qk_store(buf, qi):
        rows = slice(qi * TQ, (qi + 1) * TQ)
        q = q_ref[0, rows, :] * (HEAD ** -0.5 * LOG2E)
        for hh in range(2):
            qm = jnp.where(lo if hh == 0 else jnp.logical_not(lo), q, 0.0).astype(BF16)
            _store_logits(buf, hh, _dot_nt(qm, k_bf))

    def finish(buf, qi):
        rows = slice(qi * TQ, (qi + 1) * TQ)
        o = jnp.where(lo, _softmax_pv(buf, 0, v_ext), _softmax_pv(buf, 1, v_ext))
        o_ref[0, rows, :] = o * _silu(g_ref[0, rows, :])

    _attn_windows(n_blk, qk_store, finish, s_a, s_b)


def _mixer_m(p, kv):
    b, s, _ = p.shape
    n_mem = kv.shape[1]
    n_blk = s // TQ
    n_pairs = 2

    def col(off):
        return pl.BlockSpec((1, s, LANES), lambda bi, j, off=off: (bi, 0, off // LANES + j))

    return pl.pallas_call(
        functools.partial(_mixer_m_kernel, n_blk=n_blk),
        grid=(b, n_pairs),
        in_specs=[col(M_Q), col(M_G),
                  pl.BlockSpec((1, n_mem, LANES), lambda bi, j: (bi, 0, j)),
                  pl.BlockSpec((1, n_mem, LANES), lambda bi, j: (bi, 0, n_pairs + j))],
        out_specs=pl.BlockSpec((1, s, LANES), lambda bi, j: (bi, 0, j)),
        out_shape=jax.ShapeDtypeStruct((b, s, n_pairs * LANES), F32),
        scratch_shapes=_logit_bufs(n_mem),
        compiler_params=_params("parallel", "parallel"),
        name="mixer_m",
    )(p, p, kv, kv)


def _split3_dot(tri_bf, x):
    h1 = x.astype(BF16)
    r1 = x - h1.astype(F32)
    h2 = r1.astype(BF16)
    h3 = (r1 - h2.astype(F32)).astype(BF16)
    return _dot(tri_bf, h1) + _dot(tri_bf, h2) + _dot(tri_bf, h3)


def _rwkv_kernel(r_ref, k_ref, v_ref, wa_ref, w0_ref, wup_ref, a0_ref, aup_ref,
                 kk_ref, ka_ref, rk_ref, yf_ref, yb_ref, bvf_ref, bvb_ref, state_ref, *, nb, nch):
    C = CHUNK
    R = nch * C
    c = pl.program_id(1)

    @pl.when(c == 0)
    def _():
        state_ref[...] = jnp.zeros_like(state_ref)

    n_pairs = r_ref.shape[2] // LANES
    width = r_ref.shape[2]
    tt = lax.broadcasted_iota(jnp.int32, (C, LANES), 0)
    ss = lax.broadcasted_iota(jnp.int32, (C, LANES), 1) % C
    eye_bf = jnp.where(tt == ss, 1.0, 0.0).astype(BF16)
    rr = lax.broadcasted_iota(jnp.int32, (LANES, LANES), 0)
    cc = lax.broadcasted_iota(jnp.int32, (LANES, LANES), 1)
    eye = rr == cc
    tr = lax.broadcasted_iota(jnp.int32, (R, R), 0)
    tc = lax.broadcasted_iota(jnp.int32, (R, R), 1)
    same_chunk = (tr // C) == (tc // C)
    lo = _lane_lo(C)
    hi = jnp.logical_not(lo)
    lo_bf = jnp.where(lo, 1.0, 0.0).astype(BF16)
    hi_bf = jnp.where(hi, 1.0, 0.0).astype(BF16)
    lo_r = _lane_lo(R)

    def stack(x):
        return jnp.concatenate([jnp.where(lo, x, 0.0), jnp.where(hi, x, 0.0)], axis=0)

    def stack_bf(x):
        return jnp.concatenate([x * lo_bf, x * hi_bf], axis=0)

    units = []
    for e in range(2):
        rev = e == 1
        blk = (nb - 1 - c) if rev else c
        r0 = pl.multiple_of(blk * R, R)
        bv_ref = bvb_ref if rev else bvf_ref
        strict = (tt < ss) if rev else (tt > ss)
        incl = (tt <= ss) if rev else (tt >= ss)
        tri_bf = jnp.where(jnp.logical_and(same_chunk, (tr <= tc) if rev else (tr >= tc)), 1.0, 0.0).astype(BF16)
        last = 0 if rev else C - 1
        up, dn = (ss, tt) if rev else (tt, ss)
        level_masks = []
        m = 1
        while m < C:
            same_blk = ((tt ^ ss) >> (int(math.log2(m)) + 1)) == 0
            lvl_mask = jnp.logical_and(same_blk, jnp.logical_and((up & m) != 0, (dn & m) == 0))
            level_masks.append(jnp.where(lvl_mask, 1.0, 0.0).astype(BF16))
            m *= 2

        rs = r_ref[0, pl.ds(r0, R), :]
        ks = k_ref[0, pl.ds(r0, R), :]
        vs = v_ref[0, pl.ds(r0, R), :]
        wa = wa_ref[0, pl.ds(r0, R), :]
        sel = lo_r if e == 0 else jnp.logical_not(lo_r)
        wd = jnp.where(sel, jnp.tanh(wa[:, :LANES]), 0.0).astype(BF16)
        ad = jnp.where(sel, wa[:, LANES:], 0.0).astype(BF16)
        zw = w0_ref[e:e + 1, :] + _dot(wd, wup_ref[...])
        lw = -math.exp(-0.5) / (1.0 + jnp.exp(-zw))
        za = a0_ref[e:e + 1, :] + _dot(ad, aup_ref[...])
        a_sig = 1.0 / (1.0 + jnp.exp(-za))
        kk = ks * kk_ref[...]
        kk2 = kk * kk
        nrm2 = jnp.concatenate([_seg_sum(kk2[:, i * LANES:(i + 1) * LANES], lo_r) for i in range(n_pairs)], axis=1)
        kkn = kk / jnp.maximum(jnp.sqrt(nrm2), 1e-12)
        ka = ka_ref[...]
        ke = ks * ((1.0 - ka) + ka * a_sig)
        be = kkn * a_sig
        rkr = rs * ke * rk_ref[...]
        bonus = jnp.concatenate([_seg_sum(rkr[:, i * LANES:(i + 1) * LANES], lo_r) for i in range(n_pairs)], axis=1)
        bv_ref[0] = bonus * vs

        l_incl = _split3_dot(tri_bf, lw)
        l_tot = jnp.concatenate(
            [jnp.broadcast_to(l_incl[j * C + last:j * C + last + 1, :], (C, width)) for j in range(nch)], axis=0)
        w_inv = jnp.exp(-l_incl)
        d_end = jnp.exp(l_tot)
        w_end = d_end * w_inv
        at = -kkn * jnp.exp(l_incl - lw)
        rt = rs * jnp.exp(l_incl)
        bt = be * w_inv
        kt = ke * w_inv
        bh = be * w_end
        kh = ke * w_end
        for j in range(nch):
            rows = slice(j * C, (j + 1) * C)
            for pr in range(n_pairs):
                sl = slice(pr * LANES, (pr + 1) * LANES)
                units.append(dict(
                    e=e, j=j, pr=pr, strict=strict, incl=incl, levels=level_masks,
                    at=at[rows, sl], rt=rt[rows, sl], bt=bt[rows, sl], kt=kt[rows, sl],
                    bh=bh[rows, sl], kh=kh[rows, sl], v=vs[rows, sl], d_end=d_end[j * C:j * C + 1, sl]))

    for u in units:
        u["at_bf"] = u["at"].astype(BF16)
        u["v_st"] = stack_bf(u["v"].astype(BF16))
        lhs = jnp.concatenate([u["at_bf"], u["rt"].astype(BF16)], axis=0)
        rhs = jnp.concatenate([stack_bf(u["bt"].astype(BF16)), stack_bf(u["kt"].astype(BF16))], axis=0)
        g = _dot_nt(lhs, rhs)
        u["a_ab"] = jnp.where(u["strict"], g[:C, :LANES], 0.0).astype(BF16)
        u["a_ak"] = jnp.where(u["strict"], g[:C, LANES:], 0.0).astype(BF16)
        u["ly"] = jnp.concatenate([jnp.where(u["incl"], g[C:, :LANES], 0.0),
                                   jnp.where(u["incl"], g[C:, LANES:], 0.0)], axis=1).astype(BF16)
        u["t"] = eye_bf + u["a_ab"] * u["levels"][0]
    for lvl in range(1, len(units[0]["levels"])):
        for u in units:
            u["tmp"] = _dot(u["a_ab"] * u["levels"][lvl], stack_bf(u["t"])).astype(BF16)
        for u in units:
            u["t"] = u["t"] + _dot(u["t"], stack_bf(u["tmp"])).astype(BF16)
    for u in units:
        u["akv"] = _dot(u["a_ak"], u["v_st"]).astype(BF16)
    for u in units:
        x = jnp.concatenate([stack_bf(u["at_bf"]), stack_bf(u["akv"])], axis=1)
        pq = _dot(u["t"], x).astype(BF16)
        u["ry"] = jnp.concatenate(
            [jnp.concatenate([stack_bf(pq[:, :LANES]), stack_bf(pq[:, LANES:])], axis=1),
             jnp.concatenate([jnp.zeros((LANES, LANES), BF16), u["v_st"]], axis=1)], axis=0)
        ls = jnp.concatenate([stack(u["bh"]), stack(u["kh"])], axis=0)
        u["lyz"] = jnp.concatenate([u["ly"], ls.T.astype(BF16)], axis=0)
    for u in units:
        yz = _dot(u["lyz"], u["ry"])
        yy = yz[:C]
        zz = yz[C:]
        u["rpm"] = jnp.concatenate([u["rt"] + yy[:, :LANES], zz[:, :LANES]], axis=0).astype(BF16)
        u["y0"] = yy[:, LANES:]
        u["z"] = zz[:, LANES:]
        d_diag = jnp.where(eye, jnp.broadcast_to(u["d_end"], (LANES, LANES)), 0.0)
        u["d_col"] = jnp.sum(d_diag, axis=1, keepdims=True)

    by_key = {(u["e"], u["j"], u["pr"]): u for u in units}
    for e in range(2):
        y_ref = yb_ref if e == 1 else yf_ref
        order = list(range(nch))[::-1] if e == 1 else list(range(nch))
        ys = {}
        states = [state_ref[e, pr] for pr in range(n_pairs)]
        for j in order:
            for pr in range(n_pairs):
                u = by_key[(e, j, pr)]
                ym = _dot(u["rpm"], states[pr].astype(BF16))
                ys[(j, pr)] = ym[:C] + u["y0"]
                states[pr] = u["d_col"] * states[pr] + ym[C:] + u["z"]
        for pr in range(n_pairs):
            state_ref[e, pr] = states[pr]
        y_ref[0] = jnp.concatenate(
            [jnp.concatenate([ys[(j, pr)] for pr in range(n_pairs)], axis=1) for j in range(nch)], axis=0)


RWKV_CHUNKS_PER_STEP = 4


def _mixer_b_scan(p, w0, w_up, a0, a_up, k_k, k_a, r_k):
    b, s, _ = p.shape
    width = B_K - B_R
    nch = RWKV_CHUNKS_PER_STEP
    rows = nch * CHUNK
    nb = s // rows
    n_pairs = width // LANES

    def seq(off, w):
        return pl.BlockSpec((1, s, w), lambda bi, c, off=off, w=w: (bi, 0, off // w))

    def full(shape):
        return pl.BlockSpec(shape, lambda bi, c: tuple(0 for _ in shape))

    out_f = pl.BlockSpec((1, rows, width), lambda bi, c: (bi, c, 0))
    out_b = pl.BlockSpec((1, rows, width), lambda bi, c: (bi, nb - 1 - c, 0))
    sds = jax.ShapeDtypeStruct((b, s, width), F32)
    lora = 2 * HEAD
    return pl.pallas_call(
        functools.partial(_rwkv_kernel, nb=nb, nch=nch),
        grid=(b, nb),
        in_specs=[seq(B_R, width), seq(B_K, width), seq(B_V, width), seq(B_WA, 2 * LANES),
                  full((2, width)), full((lora, width)),
                  full((2, width)), full((lora, width)), full((1, width)), full((1, width)), full((1, width))],
        out_specs=[out_f, out_b, out_f, out_b],
        out_shape=[sds, sds, sds, sds],
        scratch_shapes=[pltpu.VMEM((2, n_pairs, LANES, LANES), F32)],
        compiler_params=_params("parallel", "arbitrary"),
        name="mixer_b",
    )(p, p, p, p, w0, w_up.reshape(lora, width).astype(BF16), a0, a_up.reshape(lora, width).astype(BF16),
      k_k.reshape(1, width), k_a.reshape(1, width), r_k.reshape(1, width))


def _merge_kernel(h_ref, hb_ref, oa_ref, yf_ref, yb_ref, bvf_ref, bvb_ref, gb0_ref, gb1_ref, oc_ref, od_ref,
                  om_ref, wg_ref, bg_ref, wb_ref, wo_ref, lng_ref, lnb_ref, bg_g_ref, bg_b_ref,
                  hn_ref, hnb_ref, *, alpha):
    d = h_ref.shape[1]
    tm = h_ref.shape[0]
    lo = _lane_lo(tm)
    y = yf_ref[...] + yb_ref[...]
    n_pairs = y.shape[1] // LANES
    gn = []
    for i in range(n_pairs):
        yp = y[:, i * LANES:(i + 1) * LANES]
        mu = _seg_sum(yp, lo) * (1.0 / HEAD)
        dy = yp - mu
        var = _seg_sum(dy * dy, lo) * (1.0 / HEAD)
        gn.append(dy * lax.rsqrt(var + B_GN_EPS))
    gn = jnp.concatenate(gn, axis=1) * bg_g_ref[...] + bg_b_ref[...]
    gb = jnp.concatenate([gb0_ref[...], gb1_ref[...]], axis=1)
    ob = (gn + bvf_ref[...] + bvb_ref[...]) * _silu(gb)

    hb = hb_ref[...]
    branches = (oa_ref[...], ob, oc_ref[...], od_ref[...], om_ref[...])
    acc = None
    row = 0
    for i, o in enumerate(branches):
        wdt = o.shape[1]
        gate = 1.0 / (1.0 + jnp.exp(-(_dot(hb, wg_ref[:, i * d:(i + 1) * d]) + bg_ref[:, i * d:(i + 1) * d])))
        proj = _dot(o.astype(BF16), wb_ref[row:row + wdt, :])
        term = gate * proj
        acc = term if acc is None else acc + term
        row += wdt
    out = _dot(acc.astype(BF16), wo_ref[...])
    z = alpha * h_ref[...] + out
    mu = jnp.mean(z, -1, keepdims=True)
    dz = z - mu
    var = jnp.mean(dz * dz, -1, keepdims=True)
    hn = dz * lax.rsqrt(var + 1e-5) * lng_ref[...] + lnb_ref[...]
    hn_ref[...] = hn
    hnb_ref[...] = hn.astype(BF16)


def _merge(h, hb, o_a, yf, yb, bvf, bvb, p2, o_c, o_d, o_m, layer, w_gate, b_gate, w_branch, w_out, ln_g, ln_b,
           bln_g, bln_b):
    n, d = h.shape
    tm = MERGE_ROWS
    alpha = (2 * DEPTH) ** 0.25
    wb = yf.shape[1]

    def rows(t):
        return pl.BlockSpec((tm, t.shape[1]), lambda i: (i, 0))

    def full(shape):
        return pl.BlockSpec(shape, lambda i: tuple(0 for _ in shape))

    def of_layer(w):
        return pl.BlockSpec((None,) + w.shape[1:], lambda i: (layer, 0, 0))

    half = wb // 2
    assert B_G % half == 0
    gb_specs = pl.BlockSpec((tm, half), lambda i: (i, B_G // half))
    gb_specs2 = pl.BlockSpec((tm, half), lambda i: (i, B_G // half + 1))
    return pl.pallas_call(
        functools.partial(_merge_kernel, alpha=alpha),
        grid=(n // tm,),
        in_specs=[rows(h), rows(hb), rows(o_a), rows(yf), rows(yb), rows(bvf), rows(bvb),
                  gb_specs, gb_specs2, rows(o_c), rows(o_d), rows(o_m),
                  of_layer(w_gate), full((1, b_gate.shape[0])), of_layer(w_branch), of_layer(w_out),
                  full((1, d)), full((1, d)), full((1, wb)), full((1, wb))],
        out_specs=[rows(h), rows(hb)],
        out_shape=[jax.ShapeDtypeStruct((n, d), F32), jax.ShapeDtypeStruct((n, d), BF16)],
        compiler_params=_params("parallel"),
        name="merge",
    )(h, hb, o_a, yf, yb, bvf, bvb, p2, p2, o_c, o_d, o_m,
      w_gate, b_gate.reshape(1, -1), w_branch, w_out, ln_g.reshape(1, d), ln_b.reshape(1, d),
      bln_g.reshape(1, wb), bln_b.reshape(1, wb))


def kernel(x, mem, ln_in_g, ln_in_b, rel_bias, w_in, shift_mu, rwkv_w0, rwkv_w_up, rwkv_a0, rwkv_a_up,
           rwkv_k_k, rwkv_k_a, rwkv_r_k, rwkv_ln_g, rwkv_ln_b, c_qnorm_g, c_knorm_g, d_lambda, d_subln_g,
           w_mem_kv, w_branch, w_gate, b_gate, w_out, ln_g, ln_b):
    b, s, d = x.shape
    n = b * s
    bias = _bias_tiles(rel_bias, s // TB)
    cos, sin = _rope_tables(s)
    mem_bf = mem.reshape(b * mem.shape[1], d).astype(BF16)
    h, hb = _ln_in(x.reshape(n, d), ln_in_g, ln_in_b)
    w_gate, w_branch, w_out = (w.astype(BF16) for w in (w_gate, w_branch, w_out))
    for l in range(DEPTH):
        p2 = _proj_in(hb, w_in, l, shift_mu[l], s)
        p = p2.reshape(b, s, IN_COLS)
        kv = _matmul(mem_bf, w_mem_kv, l, MEM_TILE, MEM_TILE, "proj_mem").reshape(b, mem.shape[1], -1)
        o_a = _mixer_a(p, bias)
        yf, yb, bvf, bvb = _mixer_b_scan(p, rwkv_w0[l], rwkv_w_up[l], rwkv_a0[l], rwkv_a_up[l],
                                         rwkv_k_k[l], rwkv_k_a[l], rwkv_r_k[l])
        o_c = _mixer_c(p, cos, sin, c_qnorm_g[l], c_knorm_g[l])
        o_d = _mixer_d(p, bias, d_lambda[l], d_subln_g[l], l)
        o_m = _mixer_m(p, kv)
        flat = lambda t: t.reshape(n, t.shape[-1])
        h, hb = _merge(h, hb, flat(o_a), flat(yf), flat(yb), flat(bvf), flat(bvb), p2, flat(o_c), flat(o_d),
                       flat(o_m), l, w_gate, b_gate[l], w_branch, w_out, ln_g[l], ln_b[l],
                       rwkv_ln_g[l], rwkv_ln_b[l])
    return h.reshape(b, s, d)
```

```python
import functools
import math

import numpy as np
import jax
import jax.numpy as jnp
from jax import lax
from jax.experimental import pallas as pl
from jax.experimental.pallas import tpu as pltpu

F32 = jnp.float32
BF16 = jnp.bfloat16

LANES = 128
SUBLANES = 8
HEAD = 64
VMEM_LIMIT = 56 * 1024 * 1024

DEPTH = 2
GRID_W = 64
ROPE_THETA = 10000.0
NUM_BUCKETS = 32
REL_MAX_DISTANCE = 1024
A_HEADS = 8
D_HEADS = 4
A_GROUPS = ((128, 1), (512, 4), (2048, 16))
B_GN_EPS = 64e-5
NEG_INF = -1e30

A_Q, A_K, A_V, A_G = 0, 512, 1024, 1536
B_R, B_K, B_V, B_WA, B_G = 2048, 2560, 3072, 3584, 3840
C_Q, C_K, C_V, C_G = 4352, 4864, 4992, 5120
D_Q, D_K, D_V, D_G = 5632, 6144, 6656, 7168
M_Q, M_G = 7680, 7936
IN_COLS = 8192

TB = 128
TQ = 256
LOG2E = math.log2(math.e)
CHUNK = 64
LN_ROWS = 512
MERGE_ROWS = 256
PROJ_COLS = 2048
SHIFT_COLS = 256


def _dot(a, b):
    return jnp.dot(a, b, preferred_element_type=F32)


def _dot_nt(a, b):
    return lax.dot_general(a, b, (((1,), (1,)), ((), ())), preferred_element_type=F32)


def _silu(g):
    return g / (1.0 + jnp.exp(-g))


def _params(*sem):
    return pltpu.CompilerParams(dimension_semantics=sem, vmem_limit_bytes=VMEM_LIMIT)


def _ln_in_kernel(x_ref, g_ref, b_ref, h_ref, hb_ref):
    x = x_ref[...]
    mu = jnp.mean(x, -1, keepdims=True)
    d = x - mu
    var = jnp.mean(d * d, -1, keepdims=True)
    h = d * lax.rsqrt(var + 1e-5) * g_ref[...] + b_ref[...]
    h_ref[...] = h
    hb_ref[...] = h.astype(BF16)


def _ln_in(x2, g, b):
    n, d = x2.shape
    tm = LN_ROWS
    return pl.pallas_call(
        _ln_in_kernel,
        grid=(n // tm,),
        in_specs=[pl.BlockSpec((tm, d), lambda i: (i, 0)),
                  pl.BlockSpec((1, d), lambda i: (0, 0)),
                  pl.BlockSpec((1, d), lambda i: (0, 0))],
        out_specs=[pl.BlockSpec((tm, d), lambda i: (i, 0)),
                   pl.BlockSpec((tm, d), lambda i: (i, 0))],
        out_shape=[jax.ShapeDtypeStruct((n, d), F32), jax.ShapeDtypeStruct((n, d), BF16)],
        compiler_params=_params("parallel"),
        name="ln_in",
    )(x2, g.reshape(1, d), b.reshape(1, d))


HALO = 16


def _proj_in_kernel(a_ref, halo_ref, w_ref, mu_ref, o_ref, wbf_ref, sh_ref, *, shift_block, n_shift, sub):
    j = pl.program_id(0)
    i = pl.program_id(1)

    @pl.when(i == 0)
    def _():
        wbf_ref[...] = w_ref[...].astype(BF16)

    @pl.when(j != shift_block)
    def _():
        o_ref[...] = _dot(a_ref[...], wbf_ref[...])

    @pl.when(j == shift_block)
    def _():
        tm, tn = o_ref.shape
        first_half = (i % 2) == 0
        for cb in range(tn // sub):
            cols = slice(cb * sub, (cb + 1) * sub)
            if cols.start >= n_shift:
                o_ref[:, cols] = _dot(a_ref[...], wbf_ref[:, cols])
                continue
            xh = _dot(jnp.concatenate([a_ref[...], halo_ref[...]], axis=0), wbf_ref[:, cols])
            x = xh[:tm]
            prev_row = jnp.where(first_half, 0.0, xh[tm + HALO - 1:tm + HALO])
            next_row = jnp.where(first_half, xh[tm:tm + 1], 0.0)
            at = SUBLANES
            sh_ref[at - 1:at, :] = prev_row
            sh_ref[at:at + tm, :] = x
            sh_ref[at + tm:at + tm + 1, :] = next_row
            prev = sh_ref[at - 1:at - 1 + tm, :]
            nxt = sh_ref[at + 1:at + 1 + tm, :]
            mu0 = mu_ref[0:1, cols]
            mu1 = mu_ref[1:2, cols]
            o_ref[:, cols] = x * (1.0 - mu0 - mu1) + mu0 * prev + mu1 * nxt


def _proj_in(a, w, layer, mu, seq):
    m, k = a.shape
    n = w.shape[2]
    tn = PROJ_COLS
    tm = seq // 2
    assert B_R % tn == 0 and B_G < B_R + tn, "the RWKV mix columns must sit inside one column block"
    mu_pad = jnp.zeros((2, tn), F32).at[:, :mu.shape[1]].set(mu)
    halo_rows = tm // HALO

    def halo_index(j, i):
        return ((i + 1 - i % 2) * halo_rows - i % 2, 0)

    return pl.pallas_call(
        functools.partial(_proj_in_kernel, shift_block=B_R // tn, n_shift=mu.shape[1], sub=SHIFT_COLS),
        grid=(n // tn, m // tm),
        in_specs=[pl.BlockSpec((tm, k), lambda j, i: (i, 0)),
                  pl.BlockSpec((HALO, k), halo_index),
                  pl.BlockSpec((None, k, tn), lambda j, i: (layer, 0, j)),
                  pl.BlockSpec((2, tn), lambda j, i: (0, 0))],
        out_specs=pl.BlockSpec((tm, tn), lambda j, i: (i, j)),
        out_shape=jax.ShapeDtypeStruct((m, n), F32),
        scratch_shapes=[pltpu.VMEM((k, tn), BF16), pltpu.VMEM((tm + 2 * SUBLANES, SHIFT_COLS), F32)],
        compiler_params=_params("parallel", "arbitrary"),
        name="proj_in",
    )(a, a, w, mu_pad)


def _rel_bucket_np(rel):
    nb = NUM_BUCKETS // 2
    max_exact = nb // 2
    n = np.abs(rel)
    nf = np.maximum(n, 1).astype(np.float32)
    large = max_exact + (np.log(nf / np.float32(max_exact)) / np.float32(math.log(REL_MAX_DISTANCE / max_exact))
                         * np.float32(nb - max_exact)).astype(np.int32)
    large = np.minimum(large, nb - 1)
    return (np.where(rel > 0, nb, 0) + np.where(n < max_exact, n, large)).astype(np.int32)


def _tile_deltas(n_blk):
    d = np.arange(2 * n_blk - 1)[:, None, None] - (n_blk - 1)
    r = np.arange(TB)[None, :, None]
    c = np.arange(TB)[None, None, :]
    return d * TB + c - r


def _dilated_log_multiplicity(delta):
    mult = np.zeros(delta.shape, np.float32)
    for window, dil in A_GROUPS:
        mult += ((delta % dil == 0) & (np.abs(delta) <= window // 2)).astype(np.float32)
    with np.errstate(divide="ignore"):
        return np.where(mult > 0, np.log(np.maximum(mult, 1.0)), NEG_INF).astype(np.float32)


def _bias_kernel(table_ref, bucket_ref, base_ref, o_ref, *, n_a, tile_bucket):
    h = pl.program_id(0)
    for d, const in enumerate(tile_bucket):
        acc = jnp.where(h < n_a, base_ref[d], 0.0)
        if const is not None:
            acc = acc + table_ref[const, h]
        else:
            bucket = bucket_ref[d]
            for b in range(NUM_BUCKETS):
                acc = acc + jnp.where(bucket == b, table_ref[b, h], 0.0)
        o_ref[0, d] = acc * LOG2E


def _bias_tiles(rel_bias, n_blk):
    delta = _tile_deltas(n_blk)
    bucket_np = _rel_bucket_np(delta)
    tile_bucket = tuple(int(t.flat[0]) if (t == t.flat[0]).all() else None for t in bucket_np)
    bucket = jnp.asarray(bucket_np)
    base = jnp.asarray(_dilated_log_multiplicity(delta))
    n_heads = rel_bias.shape[1]
    nd = 2 * n_blk - 1
    return pl.pallas_call(
        functools.partial(_bias_kernel, n_a=A_HEADS, tile_bucket=tile_bucket),
        grid=(n_heads,),
        in_specs=[pl.BlockSpec(memory_space=pltpu.SMEM),
                  pl.BlockSpec((nd, TB, TB), lambda h: (0, 0, 0)),
                  pl.BlockSpec((nd, TB, TB), lambda h: (0, 0, 0))],
        out_specs=pl.BlockSpec((1, nd, TB, TB), lambda h: (h, 0, 0, 0)),
        out_shape=jax.ShapeDtypeStruct((n_heads, nd, TB, TB), F32),
        compiler_params=_params("arbitrary"),
        name="bias_tiles",
    )(rel_bias, bucket, base)


def _lane_lo(rows):
    return lax.broadcasted_iota(jnp.int32, (rows, LANES), 1) < HEAD


def _with_ones(v_bf):
    return jnp.concatenate([v_bf, jnp.ones(v_bf.shape, BF16)], axis=1)


def _store_logits(buf, hh, s, bias_fn=None):
    rows, keys = s.shape
    for rb in range(rows // TB):
        for j in range(keys // LANES):
            c = s[rb * TB:(rb + 1) * TB, j * LANES:(j + 1) * LANES]
            if bias_fn is not None:
                c = c + bias_fn(rb, j)
            buf[hh, rb * TB:(rb + 1) * TB, j * LANES:(j + 1) * LANES] = c


def _softmax_pv(buf, hh, v_ext):
    rows = buf.shape[1]
    keys = v_ext.shape[0]
    es = []
    for rb in range(rows // TB):
        def chunk(j):
            return buf[hh, rb * TB:(rb + 1) * TB, j * LANES:(j + 1) * LANES]
        m = chunk(0)
        for j in range(1, keys // LANES):
            m = jnp.maximum(m, chunk(j))
        m = jnp.max(m, axis=1, keepdims=True)
        es.append(jnp.concatenate([jnp.exp2(chunk(j) - m).astype(BF16) for j in range(keys // LANES)], axis=1))
    o = _dot(jnp.concatenate(es, axis=0), v_ext)
    return o[:, :LANES] / o[:, LANES:]


def _attn_loop(n_q, qk_store, finish, s_a, s_b):
    qk_store(s_a, 0)

    def body(i, carry):
        q0 = 2 * i
        qk_store(s_b, q0 + 1)
        finish(s_a, q0)
        qk_store(s_a, q0 + 2)
        finish(s_b, q0 + 1)
        return carry

    lax.fori_loop(0, n_q // 2 - 1, body, 0)
    qk_store(s_b, n_q - 1)
    finish(s_a, n_q - 2)
    finish(s_b, n_q - 1)


def _attn_windows(n_q, qk_store, finish, s_a, s_b):
    bufs = (s_a, s_b)
    qk_store(bufs[0], 0)
    for qi in range(n_q):
        if qi + 1 < n_q:
            qk_store(bufs[(qi + 1) % 2], qi + 1)
        finish(bufs[qi % 2], qi)


def _logit_bufs(keys):
    return [pltpu.VMEM((2, TQ, keys), F32), pltpu.VMEM((2, TQ, keys), F32)]


def _seg_sum(x, lo):
    s0 = jnp.sum(jnp.where(lo, x, 0.0), axis=1, keepdims=True)
    s1 = jnp.sum(jnp.where(lo, 0.0, x), axis=1, keepdims=True)
    return jnp.where(lo, s0, s1)


def _mixer_a_kernel(q_ref, k_ref, v_ref, g_ref, bias_ref, o_ref, s_a, s_b, *, n_blk):
    k_bf = k_ref[0].astype(BF16)
    v_ext = _with_ones(v_ref[0].astype(BF16))
    lo = _lane_lo(TQ)
    rpb = TQ // TB
    n_q = n_blk // rpb
    reach = -(-max(w // 2 for w, _ in A_GROUPS) // TQ)

    def window(qi):
        return max(0, qi - reach), min(n_q, qi + reach + 1)

    def qk_store(buf, qi):
        k0, k1 = window(qi)
        q = q_ref[0, qi * TQ:(qi + 1) * TQ, :] * (HEAD ** -0.5 * LOG2E)
        for hh in range(2):
            qm = jnp.where(lo if hh == 0 else jnp.logical_not(lo), q, 0.0).astype(BF16)
            _store_logits(buf, hh, _dot_nt(qm, k_bf[k0 * TQ:k1 * TQ]),
                          lambda rb, j, hh=hh: bias_ref[hh, n_blk - 1 - (qi * rpb + rb) + k0 * rpb + j])

    def finish(buf, qi):
        k0, k1 = window(qi)
        v_w = v_ext[k0 * TQ:k1 * TQ]
        o = jnp.where(lo, _softmax_pv(buf, 0, v_w), _softmax_pv(buf, 1, v_w))
        o_ref[0, qi * TQ:(qi + 1) * TQ, :] = o * _silu(g_ref[0, qi * TQ:(qi + 1) * TQ, :])

    _attn_windows(n_q, qk_store, finish, s_a, s_b)


def _mixer_a(p, bias):
    b, s, _ = p.shape
    n_blk = s // TB
    nd = 2 * n_blk - 1
    n_pairs = A_HEADS // 2

    def col(off):
        return pl.BlockSpec((1, s, LANES), lambda bi, j, off=off: (bi, 0, off // LANES + j))

    return pl.pallas_call(
        functools.partial(_mixer_a_kernel, n_blk=n_blk),
        grid=(b, n_pairs),
        in_specs=[col(A_Q), col(A_K), col(A_V), col(A_G),
                  pl.BlockSpec((2, nd, TB, TB), lambda bi, j: (j, 0, 0, 0))],
        out_specs=pl.BlockSpec((1, s, LANES), lambda bi, j: (bi, 0, j)),
        out_shape=jax.ShapeDtypeStruct((b, s, n_pairs * LANES), F32),
        scratch_shapes=_logit_bufs(s),
        compiler_params=_params("parallel", "parallel"),
        name="mixer_a",
    )(p, p, p, p, bias)


def _mixer_d_kernel(q_ref, k_ref, v_ref, g_ref, bias_ref, lam_ref, sg_ref, o_ref, s_a, s_b, *, n_blk, lam_init):
    k_bf = k_ref[0].astype(BF16)
    v_ext = _with_ones(v_ref[0].astype(BF16))
    lo = _lane_lo(TQ)
    rpb = TQ // TB
    dl = lam_ref[...]
    lam = (jnp.exp(jnp.sum(dl[0:1] * dl[1:2], axis=1, keepdims=True))
           - jnp.exp(jnp.sum(dl[2:3] * dl[3:4], axis=1, keepdims=True)) + lam_init)
    sg = sg_ref[...] * (1.0 - lam_init)

    def qk_store(buf, qi):
        r0 = pl.multiple_of(qi * TQ, TQ)
        q = q_ref[0, pl.ds(r0, TQ), :] * (HEAD ** -0.5 * LOG2E)
        for hh in range(2):
            qm = jnp.where(lo if hh == 0 else jnp.logical_not(lo), q, 0.0).astype(BF16)
            _store_logits(buf, hh, _dot_nt(qm, k_bf),
                          lambda rb, j: bias_ref[0, n_blk - 1 - (qi * rpb + rb) + j])

    def finish(buf, qi):
        r0 = pl.multiple_of(qi * TQ, TQ)
        o = _softmax_pv(buf, 0, v_ext) - lam * _softmax_pv(buf, 1, v_ext)
        o = o * lax.rsqrt(jnp.mean(o * o, axis=1, keepdims=True) + 1e-5) * sg
        o_ref[0, pl.ds(r0, TQ), :] = o * _silu(g_ref[0, pl.ds(r0, TQ), :])

    _attn_loop(n_blk // rpb, qk_store, finish, s_a, s_b)


def _mixer_d(p, bias, d_lambda, subln_g, layer_idx):
    b, s, _ = p.shape
    n_blk = s // TB
    nd = 2 * n_blk - 1
    lam_init = 0.8 - 0.6 * math.exp(-0.3 * layer_idx)

    def col(off):
        return pl.BlockSpec((1, s, LANES), lambda bi, j, off=off: (bi, 0, off // LANES + j))

    return pl.pallas_call(
        functools.partial(_mixer_d_kernel, n_blk=n_blk, lam_init=lam_init),
        grid=(b, D_HEADS),
        in_specs=[col(D_Q), col(D_K), col(D_V), col(D_G),
                  pl.BlockSpec((1, nd, TB, TB), lambda bi, j: (A_HEADS + j, 0, 0, 0)),
                  pl.BlockSpec((4, HEAD), lambda bi, j: (0, 0)),
                  pl.BlockSpec((1, LANES), lambda bi, j: (0, 0))],
        out_specs=pl.BlockSpec((1, s, LANES), lambda bi, j: (bi, 0, j)),
        out_shape=jax.ShapeDtypeStruct((b, s, D_HEADS * LANES), F32),
        scratch_shapes=_logit_bufs(s),
        compiler_params=_params("parallel", "parallel"),
        name="mixer_d",
    )(p, p, p, p, bias, d_lambda, subln_g.reshape(1, LANES))


def _rope_tables(s):
    t = np.arange(s)
    row, colp = t // GRID_W, t % GRID_W
    qtr = HEAD // 4
    freqs = np.float32(ROPE_THETA) ** (-(np.arange(qtr, dtype=np.float32) / np.float32(qtr)))
    lane = np.arange(LANES) % HEAD
    use_col = (lane // (HEAD // 2)) == 1
    second = (lane % (HEAD // 2)) >= qtr
    pos = np.where(use_col[None, :], colp[:, None], row[:, None]).astype(np.float32)
    ang = (pos * freqs[lane % qtr][None, :]).astype(np.float32)
    cos = np.cos(ang).astype(np.float32)
    sin = np.where(second[None, :], np.sin(ang), -np.sin(ang)).astype(np.float32)
    return jnp.asarray(cos), jnp.asarray(sin)


def _norm_rope(x, gain, cos, sin, lo, first):
    ms = _seg_sum(x * x, lo) * (1.0 / HEAD)
    x = x * lax.rsqrt(ms + 1e-6) * gain
    qtr = HEAD // 4
    partner = jnp.where(first, pltpu.roll(x, LANES - qtr, axis=1), pltpu.roll(x, qtr, axis=1))
    return x * cos + partner * sin


def _mixer_c_kernel(q_ref, k_ref, v_ref, g_ref, cos_ref, sin_ref, qg_ref, kg_ref, o_ref, s_a, s_b, krot_ref,
                    *, n_blk, s):
    pair = pl.program_id(1)
    grp = pair // 2
    lo_s = _lane_lo(s)

    @pl.when(pair == 0)
    def _():
        lane_s = lax.broadcasted_iota(jnp.int32, (s, LANES), 1)
        first_s = (lane_s % (HEAD // 2)) < (HEAD // 4)
        krot_ref[...] = _norm_rope(k_ref[0], kg_ref[...], cos_ref[...], sin_ref[...], lo_s, first_s)

    k = krot_ref[...]
    v = v_ref[0]
    k_sw = pltpu.roll(k, HEAD, axis=1)
    v_sw = pltpu.roll(v, HEAD, axis=1)
    keep = jnp.logical_xor(lo_s, grp == 1)
    k_bf = jnp.where(keep, k, k_sw).astype(BF16)
    v_ext = _with_ones(jnp.where(keep, v, v_sw).astype(BF16))
    lo = _lane_lo(TQ)
    lane = lax.broadcasted_iota(jnp.int32, (TQ, LANES), 1)
    first = (lane % (HEAD // 2)) < (HEAD // 4)

    def qk_store(buf, qi):
        rows = slice(qi * TQ, (qi + 1) * TQ)
        q = _norm_rope(q_ref[0, rows, :], qg_ref[...], cos_ref[rows, :], sin_ref[rows, :], lo, first)
        q = q * (HEAD ** -0.5 * LOG2E)
        for hh in range(2):
            qm = jnp.where(lo if hh == 0 else jnp.logical_not(lo), q, 0.0).astype(BF16)
            _store_logits(buf, hh, _dot_nt(qm, k_bf))

    def finish(buf, qi):
        rows = slice(qi * TQ, (qi + 1) * TQ)
        o = jnp.where(lo, _softmax_pv(buf, 0, v_ext), _softmax_pv(buf, 1, v_ext))
        o_ref[0, rows, :] = o * _silu(g_ref[0, rows, :])

    _attn_windows(n_blk, qk_store, finish, s_a, s_b)


def _mixer_c(p, cos, sin, qn_g, kn_g):
    b, s, _ = p.shape
    n_blk = s // TQ
    n_pairs = 4

    def col(off):
        return pl.BlockSpec((1, s, LANES), lambda bi, j, off=off: (bi, 0, off // LANES + j))

    def fixed(off):
        return pl.BlockSpec((1, s, LANES), lambda bi, j, off=off: (bi, 0, off // LANES))

    tab = pl.BlockSpec((s, LANES), lambda bi, j: (0, 0))
    gain = pl.BlockSpec((1, LANES), lambda bi, j: (0, 0))
    return pl.pallas_call(
        functools.partial(_mixer_c_kernel, n_blk=n_blk, s=s),
        grid=(b, n_pairs),
        in_specs=[col(C_Q), fixed(C_K), fixed(C_V), col(C_G), tab, tab, gain, gain],
        out_specs=pl.BlockSpec((1, s, LANES), lambda bi, j: (bi, 0, j)),
        out_shape=jax.ShapeDtypeStruct((b, s, n_pairs * LANES), F32),
        scratch_shapes=_logit_bufs(s) + [pltpu.VMEM((s, LANES), F32)],
        compiler_params=_params("parallel", "arbitrary"),
        name="mixer_c",
    )(p, p, p, p, cos, sin, jnp.tile(qn_g, 2).reshape(1, LANES), jnp.tile(kn_g, 2).reshape(1, LANES))


def _mixer_m_kernel(q_ref, g_ref, mem_ref, wk_ref, wv_ref, o_ref, s_a, s_b, *, n_blk):
    mem_bf = mem_ref[0]
    k_bf = _dot(mem_bf, wk_ref[...].astype(BF16)).astype(BF16)
    v_ext = _with_ones(_dot(mem_bf, wv_ref[...].astype(BF16)).astype(BF16))
    lo = _lane_lo(TQ)

    def qk_store(buf, qi):
        rows = slice(qi * TQ, (qi + 1) * TQ)
        q = q_ref[0, rows, :] * (HEAD ** -0.5 * LOG2E)
        for hh in range(2):
            qm = jnp.where(lo if hh == 0 else jnp.logical_not(lo), q, 0.0).astype(BF16)
            _store_logits(buf, hh, _dot_nt(qm, k_bf))

    def finish(buf, qi):
        rows = slice(qi * TQ, (qi + 1) * TQ)
        o = jnp.where(lo, _softmax_pv(buf, 0, v_ext), _softmax_pv(buf, 1, v_ext))
        o_ref[0, rows, :] = o * _silu(g_ref[0, rows, :])

    _attn_windows(n_blk, qk_store, finish, s_a, s_b)


def _mixer_m(p, mem_bf, w_mem_kv, layer):
    b, s, _ = p.shape
    n_mem, d = mem_bf.shape[1:]
    n_blk = s // TQ
    n_pairs = 2

    def col(off):
        return pl.BlockSpec((1, s, LANES), lambda bi, j, off=off: (bi, 0, off // LANES + j))

    return pl.pallas_call(
        functools.partial(_mixer_m_kernel, n_blk=n_blk),
        grid=(b, n_pairs),
        in_specs=[col(M_Q), col(M_G),
                  pl.BlockSpec((1, n_mem, d), lambda bi, j: (bi, 0, 0)),
                  pl.BlockSpec((None, d, LANES), lambda bi, j: (layer, 0, j)),
                  pl.BlockSpec((None, d, LANES), lambda bi, j: (layer, 0, n_pairs + j))],
        out_specs=pl.BlockSpec((1, s, LANES), lambda bi, j: (bi, 0, j)),
        out_shape=jax.ShapeDtypeStruct((b, s, n_pairs * LANES), F32),
        scratch_shapes=_logit_bufs(n_mem),
        compiler_params=_params("parallel", "parallel"),
        name="mixer_m",
    )(p, p, mem_bf, w_mem_kv, w_mem_kv)


def _split3_dot(tri_bf, x):
    h1 = x.astype(BF16)
    r1 = x - h1.astype(F32)
    h2 = r1.astype(BF16)
    h3 = (r1 - h2.astype(F32)).astype(BF16)
    return _dot(tri_bf, h1) + _dot(tri_bf, h2) + _dot(tri_bf, h3)


def _rwkv_kernel(r_ref, k_ref, v_ref, wa_ref, w0_ref, wup_ref, a0_ref, aup_ref,
                 kk_ref, ka_ref, rk_ref, yf_ref, yb_ref, bvf_ref, bvb_ref, state_ref, *, nb, nch):
    C = CHUNK
    R = nch * C
    c = pl.program_id(1)

    @pl.when(c == 0)
    def _():
        state_ref[...] = jnp.zeros_like(state_ref)

    n_pairs = r_ref.shape[2] // LANES
    width = r_ref.shape[2]
    tt = lax.broadcasted_iota(jnp.int32, (C, LANES), 0)
    ss = lax.broadcasted_iota(jnp.int32, (C, LANES), 1) % C
    eye_bf = jnp.where(tt == ss, 1.0, 0.0).astype(BF16)
    rr = lax.broadcasted_iota(jnp.int32, (LANES, LANES), 0)
    cc = lax.broadcasted_iota(jnp.int32, (LANES, LANES), 1)
    eye = rr == cc
    tr = lax.broadcasted_iota(jnp.int32, (R, R), 0)
    tc = lax.broadcasted_iota(jnp.int32, (R, R), 1)
    same_chunk = (tr // C) == (tc // C)
    lo = _lane_lo(C)
    hi = jnp.logical_not(lo)
    lo_bf = jnp.where(lo, 1.0, 0.0).astype(BF16)
    hi_bf = jnp.where(hi, 1.0, 0.0).astype(BF16)
    lo_r = _lane_lo(R)

    def stack(x):
        return jnp.concatenate([jnp.where(lo, x, 0.0), jnp.where(hi, x, 0.0)], axis=0)

    def stack_bf(x):
        return jnp.concatenate([x * lo_bf, x * hi_bf], axis=0)

    units = []
    for e in range(2):
        rev = e == 1
        blk = (nb - 1 - c) if rev else c
        r0 = pl.multiple_of(blk * R, R)
        bv_ref = bvb_ref if rev else bvf_ref
        strict = (tt < ss) if rev else (tt > ss)
        incl = (tt <= ss) if rev else (tt >= ss)
        tri_bf = jnp.where(jnp.logical_and(same_chunk, (tr <= tc) if rev else (tr >= tc)), 1.0, 0.0).astype(BF16)
        last = 0 if rev else C - 1
        up, dn = (ss, tt) if rev else (tt, ss)
        level_masks = []
        m = 1
        while m < C:
            same_blk = ((tt ^ ss) >> (int(math.log2(m)) + 1)) == 0
            lvl_mask = jnp.logical_and(same_blk, jnp.logical_and((up & m) != 0, (dn & m) == 0))
            level_masks.append(jnp.where(lvl_mask, 1.0, 0.0).astype(BF16))
            m *= 2

        rs = r_ref[0, pl.ds(r0, R), :]
        ks = k_ref[0, pl.ds(r0, R), :]
        vs = v_ref[0, pl.ds(r0, R), :]
        wa = wa_ref[0, pl.ds(r0, R), :]
        sel = lo_r if e == 0 else jnp.logical_not(lo_r)
        wd = jnp.where(sel, jnp.tanh(wa[:, :LANES]), 0.0).astype(BF16)
        ad = jnp.where(sel, wa[:, LANES:], 0.0).astype(BF16)
        zw = w0_ref[e:e + 1, :] + _dot(wd, wup_ref[...])
        lw = -math.exp(-0.5) / (1.0 + jnp.exp(-zw))
        za = a0_ref[e:e + 1, :] + _dot(ad, aup_ref[...])
        a_sig = 1.0 / (1.0 + jnp.exp(-za))
        kk = ks * kk_ref[...]
        kk2 = kk * kk
        nrm2 = jnp.concatenate([_seg_sum(kk2[:, i * LANES:(i + 1) * LANES], lo_r) for i in range(n_pairs)], axis=1)
        kkn = kk / jnp.maximum(jnp.sqrt(nrm2), 1e-12)
        ka = ka_ref[...]
        ke = ks * ((1.0 - ka) + ka * a_sig)
        be = kkn * a_sig
        rkr = rs * ke * rk_ref[...]
        bonus = jnp.concatenate([_seg_sum(rkr[:, i * LANES:(i + 1) * LANES], lo_r) for i in range(n_pairs)], axis=1)
        bv_ref[0] = bonus * vs

        l_incl = _split3_dot(tri_bf, lw)
        l_tot = jnp.concatenate(
            [jnp.broadcast_to(l_incl[j * C + last:j * C + last + 1, :], (C, width)) for j in range(nch)], axis=0)
        w_inv = jnp.exp(-l_incl)
        d_end = jnp.exp(l_tot)
        w_end = d_end * w_inv
        at = -kkn * jnp.exp(l_incl - lw)
        rt = rs * jnp.exp(l_incl)
        bt = be * w_inv
        kt = ke * w_inv
        bh = be * w_end
        kh = ke * w_end
        for j in range(nch):
            rows = slice(j * C, (j + 1) * C)
            for pr in range(n_pairs):
                sl = slice(pr * LANES, (pr + 1) * LANES)
                units.append(dict(
                    e=e, j=j, pr=pr, strict=strict, incl=incl, levels=level_masks,
                    at=at[rows, sl], rt=rt[rows, sl], bt=bt[rows, sl], kt=kt[rows, sl],
                    bh=bh[rows, sl], kh=kh[rows, sl], v=vs[rows, sl], d_end=d_end[j * C:j * C + 1, sl]))

    for u in units:
        u["at_bf"] = u["at"].astype(BF16)
        u["v_st"] = stack_bf(u["v"].astype(BF16))
        lhs = jnp.concatenate([u["at_bf"], u["rt"].astype(BF16)], axis=0)
        rhs = jnp.concatenate([stack_bf(u["bt"].astype(BF16)), stack_bf(u["kt"].astype(BF16))], axis=0)
        g = _dot_nt(lhs, rhs)
        u["a_ab"] = jnp.where(u["strict"], g[:C, :LANES], 0.0).astype(BF16)
        u["a_ak"] = jnp.where(u["strict"], g[:C, LANES:], 0.0).astype(BF16)
        u["ly"] = jnp.concatenate([jnp.where(u["incl"], g[C:, :LANES], 0.0),
                                   jnp.where(u["incl"], g[C:, LANES:], 0.0)], axis=1).astype(BF16)
        u["t"] = eye_bf + u["a_ab"] * u["levels"][0]
    for lvl in range(1, len(units[0]["levels"])):
        for u in units:
            u["tmp"] = _dot(u["a_ab"] * u["levels"][lvl], stack_bf(u["t"])).astype(BF16)
        for u in units:
            u["t"] = u["t"] + _dot(u["t"], stack_bf(u["tmp"])).astype(BF16)
    for u in units:
        u["akv"] = _dot(u["a_ak"], u["v_st"]).astype(BF16)
    for u in units:
        x = jnp.concatenate([stack_bf(u["at_bf"]), stack_bf(u["akv"])], axis=1)
        pq = _dot(u["t"], x).astype(BF16)
        u["ry"] = jnp.concatenate(
            [jnp.concatenate([stack_bf(pq[:, :LANES]), stack_bf(pq[:, LANES:])], axis=1),
             jnp.concatenate([jnp.zeros((LANES, LANES), BF16), u["v_st"]], axis=1)], axis=0)
        ls = jnp.concatenate([stack(u["bh"]), stack(u["kh"])], axis=0)
        u["lyz"] = jnp.concatenate([u["ly"], ls.T.astype(BF16)], axis=0)
    for u in units:
        yz = _dot(u["lyz"], u["ry"])
        yy = yz[:C]
        zz = yz[C:]
        u["rpm"] = jnp.concatenate([u["rt"] + yy[:, :LANES], zz[:, :LANES]], axis=0).astype(BF16)
        u["y0"] = yy[:, LANES:]
        u["z"] = zz[:, LANES:]
        d_diag = jnp.where(eye, jnp.broadcast_to(u["d_end"], (LANES, LANES)), 0.0)
        u["d_col"] = jnp.sum(d_diag, axis=1, keepdims=True)

    by_key = {(u["e"], u["j"], u["pr"]): u for u in units}
    for e in range(2):
        y_ref = yb_ref if e == 1 else yf_ref
        order = list(range(nch))[::-1] if e == 1 else list(range(nch))
        ys = {}
        states = [state_ref[e, pr] for pr in range(n_pairs)]
        for j in order:
            for pr in range(n_pairs):
                u = by_key[(e, j, pr)]
                ym = _dot(u["rpm"], states[pr].astype(BF16))
                ys[(j, pr)] = ym[:C] + u["y0"]
                states[pr] = u["d_col"] * states[pr] + ym[C:] + u["z"]
        for pr in range(n_pairs):
            state_ref[e, pr] = states[pr]
        y_ref[0] = jnp.concatenate(
            [jnp.concatenate([ys[(j, pr)] for pr in range(n_pairs)], axis=1) for j in range(nch)], axis=0)


RWKV_CHUNKS_PER_STEP = 4


def _mixer_b_scan(p, w0, w_up, a0, a_up, k_k, k_a, r_k):
    b, s, _ = p.shape
    width = B_K - B_R
    nch = RWKV_CHUNKS_PER_STEP
    rows = nch * CHUNK
    nb = s // rows
    n_pairs = width // LANES

    def seq(off, w):
        return pl.BlockSpec((1, s, w), lambda bi, c, off=off, w=w: (bi, 0, off // w))

    def full(shape):
        return pl.BlockSpec(shape, lambda bi, c: tuple(0 for _ in shape))

    out_f = pl.BlockSpec((1, rows, width), lambda bi, c: (bi, c, 0))
    out_b = pl.BlockSpec((1, rows, width), lambda bi, c: (bi, nb - 1 - c, 0))
    sds = jax.ShapeDtypeStruct((b, s, width), F32)
    lora = 2 * HEAD
    return pl.pallas_call(
        functools.partial(_rwkv_kernel, nb=nb, nch=nch),
        grid=(b, nb),
        in_specs=[seq(B_R, width), seq(B_K, width), seq(B_V, width), seq(B_WA, 2 * LANES),
                  full((2, width)), full((lora, width)),
                  full((2, width)), full((lora, width)), full((1, width)), full((1, width)), full((1, width))],
        out_specs=[out_f, out_b, out_f, out_b],
        out_shape=[sds, sds, sds, sds],
        scratch_shapes=[pltpu.VMEM((2, n_pairs, LANES, LANES), F32)],
        compiler_params=_params("parallel", "arbitrary"),
        name="mixer_b",
    )(p, p, p, p, w0, w_up.reshape(lora, width).astype(BF16), a0, a_up.reshape(lora, width).astype(BF16),
      k_k.reshape(1, width), k_a.reshape(1, width), r_k.reshape(1, width))


def _merge_kernel(h_ref, hb_ref, oa_ref, yf_ref, yb_ref, bvf_ref, bvb_ref, gb0_ref, gb1_ref, oc_ref, od_ref,
                  om_ref, wg_ref, bg_ref, wb_ref, wo_ref, lng_ref, lnb_ref, bg_g_ref, bg_b_ref,
                  hn_ref, hnb_ref, *, alpha):
    d = h_ref.shape[1]
    tm = h_ref.shape[0]
    lo = _lane_lo(tm)
    y = yf_ref[...] + yb_ref[...]
    n_pairs = y.shape[1] // LANES
    gn = []
    for i in range(n_pairs):
        yp = y[:, i * LANES:(i + 1) * LANES]
        mu = _seg_sum(yp, lo) * (1.0 / HEAD)
        dy = yp - mu
        var = _seg_sum(dy * dy, lo) * (1.0 / HEAD)
        gn.append(dy * lax.rsqrt(var + B_GN_EPS))
    gn = jnp.concatenate(gn, axis=1) * bg_g_ref[...] + bg_b_ref[...]
    gb = jnp.concatenate([gb0_ref[...], gb1_ref[...]], axis=1)
    ob = (gn + bvf_ref[...] + bvb_ref[...]) * _silu(gb)

    hb = hb_ref[...]
    branches = (oa_ref[...], ob, oc_ref[...], od_ref[...], om_ref[...])
    acc = None
    row = 0
    for i, o in enumerate(branches):
        wdt = o.shape[1]
        gate = 1.0 / (1.0 + jnp.exp(-(_dot(hb, wg_ref[:, i * d:(i + 1) * d]) + bg_ref[:, i * d:(i + 1) * d])))
        proj = _dot(o.astype(BF16), wb_ref[row:row + wdt, :])
        term = gate * proj
        acc = term if acc is None else acc + term
        row += wdt
    out = _dot(acc.astype(BF16), wo_ref[...])
    z = alpha * h_ref[...] + out
    mu = jnp.mean(z, -1, keepdims=True)
    dz = z - mu
    var = jnp.mean(dz * dz, -1, keepdims=True)
    hn = dz * lax.rsqrt(var + 1e-5) * lng_ref[...] + lnb_ref[...]
    hn_ref[...] = hn
    hnb_ref[...] = hn.astype(BF16)


def _merge(h, hb, o_a, yf, yb, bvf, bvb, p2, o_c, o_d, o_m, layer, w_gate, b_gate, w_branch, w_out, ln_g, ln_b,
           bln_g, bln_b):
    n, d = h.shape
    tm = MERGE_ROWS
    alpha = (2 * DEPTH) ** 0.25
    wb = yf.shape[1]

    def rows(t):
        return pl.BlockSpec((tm, t.shape[1]), lambda i: (i, 0))

    def full(shape):
        return pl.BlockSpec(shape, lambda i: tuple(0 for _ in shape))

    def of_layer(w):
        return pl.BlockSpec((None,) + w.shape[1:], lambda i: (layer, 0, 0))

    half = wb // 2
    assert B_G % half == 0
    gb_specs = pl.BlockSpec((tm, half), lambda i: (i, B_G // half))
    gb_specs2 = pl.BlockSpec((tm, half), lambda i: (i, B_G // half + 1))
    return pl.pallas_call(
        functools.partial(_merge_kernel, alpha=alpha),
        grid=(n // tm,),
        in_specs=[rows(h), rows(hb), rows(o_a), rows(yf), rows(yb), rows(bvf), rows(bvb),
                  gb_specs, gb_specs2, rows(o_c), rows(o_d), rows(o_m),
                  of_layer(w_gate), full((1, b_gate.shape[0])), of_layer(w_branch), of_layer(w_out),
                  full((1, d)), full((1, d)), full((1, wb)), full((1, wb))],
        out_specs=[rows(h), rows(hb)],
        out_shape=[jax.ShapeDtypeStruct((n, d), F32), jax.ShapeDtypeStruct((n, d), BF16)],
        compiler_params=_params("parallel"),
        name="merge",
    )(h, hb, o_a, yf, yb, bvf, bvb, p2, p2, o_c, o_d, o_m,
      w_gate, b_gate.reshape(1, -1), w_branch, w_out, ln_g.reshape(1, d), ln_b.reshape(1, d),
      bln_g.reshape(1, wb), bln_b.reshape(1, wb))


def kernel(x, mem, ln_in_g, ln_in_b, rel_bias, w_in, shift_mu, rwkv_w0, rwkv_w_up, rwkv_a0, rwkv_a_up,
           rwkv_k_k, rwkv_k_a, rwkv_r_k, rwkv_ln_g, rwkv_ln_b, c_qnorm_g, c_knorm_g, d_lambda, d_subln_g,
           w_mem_kv, w_branch, w_gate, b_gate, w_out, ln_g, ln_b):
    b, s, d = x.shape
    n = b * s
    bias = _bias_tiles(rel_bias, s // TB)
    cos, sin = _rope_tables(s)
    mem_bf = mem.astype(BF16)
    h, hb = _ln_in(x.reshape(n, d), ln_in_g, ln_in_b)
    w_gate, w_branch, w_out = (w.astype(BF16) for w in (w_gate, w_branch, w_out))
    for l in range(DEPTH):
        p2 = _proj_in(hb, w_in, l, shift_mu[l], s)
        p = p2.reshape(b, s, IN_COLS)
        o_a = _mixer_a(p, bias)
        yf, yb, bvf, bvb = _mixer_b_scan(p, rwkv_w0[l], rwkv_w_up[l], rwkv_a0[l], rwkv_a_up[l],
                                         rwkv_k_k[l], rwkv_k_a[l], rwkv_r_k[l])
        o_c = _mixer_c(p, cos, sin, c_qnorm_g[l], c_knorm_g[l])
        o_d = _mixer_d(p, bias, d_lambda[l], d_subln_g[l], l)
        o_m = _mixer_m(p, mem_bf, w_mem_kv, l)
        flat = lambda t: t.reshape(n, t.shape[-1])
        h, hb = _merge(h, hb, flat(o_a), flat(yf), flat(yb), flat(bvf), flat(bvb), p2, flat(o_c), flat(o_d),
                       flat(o_m), l, w_gate, b_gate[l], w_branch, w_out, ln_g[l], ln_b[l],
                       rwkv_ln_g[l], rwkv_ln_b[l])
    return h.reshape(b, s, d)
```

```python
import functools
import math

import numpy as np
import jax
import jax.numpy as jnp
from jax import lax
from jax.experimental import pallas as pl
from jax.experimental.pallas import tpu as pltpu

F32 = jnp.float32
BF16 = jnp.bfloat16

LANES = 128
SUBLANES = 8
HEAD = 64
VMEM_LIMIT = 56 * 1024 * 1024

DEPTH = 2
GRID_W = 64
ROPE_THETA = 10000.0
NUM_BUCKETS = 32
REL_MAX_DISTANCE = 1024
A_HEADS = 8
D_HEADS = 4
A_GROUPS = ((128, 1), (512, 4), (2048, 16))
B_GN_EPS = 64e-5
NEG_INF = -1e30

A_Q, A_K, A_V, A_G = 0, 512, 1024, 1536
B_R, B_K, B_V, B_WA, B_G = 2048, 2560, 3072, 3584, 3840
C_Q, C_K, C_V, C_G = 4352, 4864, 4992, 5120
D_Q, D_K, D_V, D_G = 5632, 6144, 6656, 7168
M_Q, M_G = 7680, 7936
IN_COLS = 8192

TB = 128
TQ = 256
LOG2E = math.log2(math.e)
CHUNK = 64
LN_ROWS = 512
MERGE_ROWS = 256
PROJ_COLS = 2048
SHIFT_COLS = 256


def _dot(a, b):
    return jnp.dot(a, b, preferred_element_type=F32)


def _dot_nt(a, b):
    return lax.dot_general(a, b, (((1,), (1,)), ((), ())), preferred_element_type=F32)


def _silu(g):
    return g / (1.0 + jnp.exp(-g))


def _params(*sem):
    return pltpu.CompilerParams(dimension_semantics=sem, vmem_limit_bytes=VMEM_LIMIT)


def _ln_in_kernel(x_ref, g_ref, b_ref, h_ref, hb_ref):
    x = x_ref[...]
    mu = jnp.mean(x, -1, keepdims=True)
    d = x - mu
    var = jnp.mean(d * d, -1, keepdims=True)
    h = d * lax.rsqrt(var + 1e-5) * g_ref[...] + b_ref[...]
    h_ref[...] = h
    hb_ref[...] = h.astype(BF16)


def _ln_in(x2, g, b):
    n, d = x2.shape
    tm = LN_ROWS
    return pl.pallas_call(
        _ln_in_kernel,
        grid=(n // tm,),
        in_specs=[pl.BlockSpec((tm, d), lambda i: (i, 0)),
                  pl.BlockSpec((1, d), lambda i: (0, 0)),
                  pl.BlockSpec((1, d), lambda i: (0, 0))],
        out_specs=[pl.BlockSpec((tm, d), lambda i: (i, 0)),
                   pl.BlockSpec((tm, d), lambda i: (i, 0))],
        out_shape=[jax.ShapeDtypeStruct((n, d), F32), jax.ShapeDtypeStruct((n, d), BF16)],
        compiler_params=_params("parallel"),
        name="ln_in",
    )(x2, g.reshape(1, d), b.reshape(1, d))


HALO = 16


def _proj_in_kernel(a_ref, halo_ref, w_ref, mu_ref, o_ref, wbf_ref, sh_ref, *, shift_block, n_shift, sub):
    j = pl.program_id(0)
    i = pl.program_id(1)

    @pl.when(i == 0)
    def _():
        wbf_ref[...] = w_ref[...].astype(BF16)

    @pl.when(j != shift_block)
    def _():
        o_ref[...] = _dot(a_ref[...], wbf_ref[...])

    @pl.when(j == shift_block)
    def _():
        tm, tn = o_ref.shape
        first_half = (i % 2) == 0
        for cb in range(tn // sub):
            cols = slice(cb * sub, (cb + 1) * sub)
            if cols.start >= n_shift:
                o_ref[:, cols] = _dot(a_ref[...], wbf_ref[:, cols])
                continue
            xh = _dot(jnp.concatenate([a_ref[...], halo_ref[...]], axis=0), wbf_ref[:, cols])
            x = xh[:tm]
            prev_row = jnp.where(first_half, 0.0, xh[tm + HALO - 1:tm + HALO])
            next_row = jnp.where(first_half, xh[tm:tm + 1], 0.0)
            at = SUBLANES
            sh_ref[at - 1:at, :] = prev_row
            sh_ref[at:at + tm, :] = x
            sh_ref[at + tm:at + tm + 1, :] = next_row
            prev = sh_ref[at - 1:at - 1 + tm, :]
            nxt = sh_ref[at + 1:at + 1 + tm, :]
            mu0 = mu_ref[0:1, cols]
            mu1 = mu_ref[1:2, cols]
            o_ref[:, cols] = x * (1.0 - mu0 - mu1) + mu0 * prev + mu1 * nxt


def _proj_in(a, w, layer, mu, seq):
    m, k = a.shape
    n = w.shape[2]
    tn = PROJ_COLS
    tm = seq // 2
    assert B_R % tn == 0 and B_G < B_R + tn, "the RWKV mix columns must sit inside one column block"
    mu_pad = jnp.zeros((2, tn), F32).at[:, :mu.shape[1]].set(mu)
    halo_rows = tm // HALO

    def halo_index(j, i):
        return ((i + 1 - i % 2) * halo_rows - i % 2, 0)

    return pl.pallas_call(
        functools.partial(_proj_in_kernel, shift_block=B_R // tn, n_shift=mu.shape[1], sub=SHIFT_COLS),
        grid=(n // tn, m // tm),
        in_specs=[pl.BlockSpec((tm, k), lambda j, i: (i, 0)),
                  pl.BlockSpec((HALO, k), halo_index),
                  pl.BlockSpec((None, k, tn), lambda j, i: (layer, 0, j)),
                  pl.BlockSpec((2, tn), lambda j, i: (0, 0))],
        out_specs=pl.BlockSpec((tm, tn), lambda j, i: (i, j)),
        out_shape=jax.ShapeDtypeStruct((m, n), F32),
        scratch_shapes=[pltpu.VMEM((k, tn), BF16), pltpu.VMEM((tm + 2 * SUBLANES, SHIFT_COLS), F32)],
        compiler_params=_params("parallel", "arbitrary"),
        name="proj_in",
    )(a, a, w, mu_pad)


def _rel_bucket_np(rel):
    nb = NUM_BUCKETS // 2
    max_exact = nb // 2
    n = np.abs(rel)
    nf = np.maximum(n, 1).astype(np.float32)
    large = max_exact + (np.log(nf / np.float32(max_exact)) / np.float32(math.log(REL_MAX_DISTANCE / max_exact))
                         * np.float32(nb - max_exact)).astype(np.int32)
    large = np.minimum(large, nb - 1)
    return (np.where(rel > 0, nb, 0) + np.where(n < max_exact, n, large)).astype(np.int32)


def _tile_deltas(n_blk):
    d = np.arange(2 * n_blk - 1)[:, None, None] - (n_blk - 1)
    r = np.arange(TB)[None, :, None]
    c = np.arange(TB)[None, None, :]
    return d * TB + c - r


def _dilated_log_multiplicity(delta):
    mult = np.zeros(delta.shape, np.float32)
    for window, dil in A_GROUPS:
        mult += ((delta % dil == 0) & (np.abs(delta) <= window // 2)).astype(np.float32)
    with np.errstate(divide="ignore"):
        return np.where(mult > 0, np.log(np.maximum(mult, 1.0)), NEG_INF).astype(np.float32)


def _bias_kernel(table_ref, bucket_ref, base_ref, o_ref, *, n_a, tile_bucket):
    h = pl.program_id(0)
    for d, const in enumerate(tile_bucket):
        acc = jnp.where(h < n_a, base_ref[d], 0.0)
        if const is not None:
            acc = acc + table_ref[const, h]
        else:
            bucket = bucket_ref[d]
            for b in range(NUM_BUCKETS):
                acc = acc + jnp.where(bucket == b, table_ref[b, h], 0.0)
        o_ref[0, d] = acc * LOG2E


def _bias_tiles(rel_bias, n_blk):
    delta = _tile_deltas(n_blk)
    bucket_np = _rel_bucket_np(delta)
    tile_bucket = tuple(int(t.flat[0]) if (t == t.flat[0]).all() else None for t in bucket_np)
    bucket = jnp.asarray(bucket_np)
    base = jnp.asarray(_dilated_log_multiplicity(delta))
    n_heads = rel_bias.shape[1]
    nd = 2 * n_blk - 1
    return pl.pallas_call(
        functools.partial(_bias_kernel, n_a=A_HEADS, tile_bucket=tile_bucket),
        grid=(n_heads,),
        in_specs=[pl.BlockSpec(memory_space=pltpu.SMEM),
                  pl.BlockSpec((nd, TB, TB), lambda h: (0, 0, 0)),
                  pl.BlockSpec((nd, TB, TB), lambda h: (0, 0, 0))],
        out_specs=pl.BlockSpec((1, nd, TB, TB), lambda h: (h, 0, 0, 0)),
        out_shape=jax.ShapeDtypeStruct((n_heads, nd, TB, TB), F32),
        compiler_params=_params("arbitrary"),
        name="bias_tiles",
    )(rel_bias, bucket, base)


def _lane_lo(rows):
    return lax.broadcasted_iota(jnp.int32, (rows, LANES), 1) < HEAD


def _with_ones(v_bf):
    return jnp.concatenate([v_bf, jnp.ones(v_bf.shape, BF16)], axis=1)


def _store_logits(buf, hh, s, bias_fn=None):
    rows, keys = s.shape
    for rb in range(rows // TB):
        for j in range(keys // LANES):
            c = s[rb * TB:(rb + 1) * TB, j * LANES:(j + 1) * LANES]
            if bias_fn is not None:
                c = c + bias_fn(rb, j)
            buf[hh, rb * TB:(rb + 1) * TB, j * LANES:(j + 1) * LANES] = c


def _softmax_pv(buf, hh, v_ext):
    rows = buf.shape[1]
    keys = v_ext.shape[0]
    es = []
    for rb in range(rows // TB):
        def chunk(j):
            return buf[hh, rb * TB:(rb + 1) * TB, j * LANES:(j + 1) * LANES]
        m = chunk(0)
        for j in range(1, keys // LANES):
            m = jnp.maximum(m, chunk(j))
        m = jnp.max(m, axis=1, keepdims=True)
        es.append(jnp.concatenate([jnp.exp2(chunk(j) - m).astype(BF16) for j in range(keys // LANES)], axis=1))
    o = _dot(jnp.concatenate(es, axis=0), v_ext)
    return o[:, :LANES] / o[:, LANES:]


def _attn_loop(n_q, qk_store, finish, s_a, s_b):
    qk_store(s_a, 0)

    def body(i, carry):
        q0 = 2 * i
        qk_store(s_b, q0 + 1)
        finish(s_a, q0)
        qk_store(s_a, q0 + 2)
        finish(s_b, q0 + 1)
        return carry

    lax.fori_loop(0, n_q // 2 - 1, body, 0)
    qk_store(s_b, n_q - 1)
    finish(s_a, n_q - 2)
    finish(s_b, n_q - 1)


def _attn_windows(n_q, qk_store, finish, s_a, s_b):
    bufs = (s_a, s_b)
    qk_store(bufs[0], 0)
    for qi in range(n_q):
        if qi + 1 < n_q:
            qk_store(bufs[(qi + 1) % 2], qi + 1)
        finish(bufs[qi % 2], qi)


def _logit_bufs(keys):
    return [pltpu.VMEM((2, TQ, keys), F32), pltpu.VMEM((2, TQ, keys), F32)]


def _seg_sum(x, lo):
    s0 = jnp.sum(jnp.where(lo, x, 0.0), axis=1, keepdims=True)
    s1 = jnp.sum(jnp.where(lo, 0.0, x), axis=1, keepdims=True)
    return jnp.where(lo, s0, s1)


def _mixer_a_kernel(q_ref, k_ref, v_ref, g_ref, bias_ref, o_ref, s_a, s_b, *, n_blk):
    k_bf = k_ref[0].astype(BF16)
    v_ext = _with_ones(v_ref[0].astype(BF16))
    lo = _lane_lo(TQ)
    rpb = TQ // TB
    n_q = n_blk // rpb
    reach = -(-max(w // 2 for w, _ in A_GROUPS) // TQ)

    def window(qi):
        return max(0, qi - reach), min(n_q, qi + reach + 1)

    def qk_store(buf, qi):
        k0, k1 = window(qi)
        q = q_ref[0, qi * TQ:(qi + 1) * TQ, :] * (HEAD ** -0.5 * LOG2E)
        for hh in range(2):
            qm = jnp.where(lo if hh == 0 else jnp.logical_not(lo), q, 0.0).astype(BF16)
            _store_logits(buf, hh, _dot_nt(qm, k_bf[k0 * TQ:k1 * TQ]),
                          lambda rb, j, hh=hh: bias_ref[hh, n_blk - 1 - (qi * rpb + rb) + k0 * rpb + j])

    def finish(buf, qi):
        k0, k1 = window(qi)
        v_w = v_ext[k0 * TQ:k1 * TQ]
        o = jnp.where(lo, _softmax_pv(buf, 0, v_w), _softmax_pv(buf, 1, v_w))
        o_ref[0, qi * TQ:(qi + 1) * TQ, :] = o * _silu(g_ref[0, qi * TQ:(qi + 1) * TQ, :])

    _attn_windows(n_q, qk_store, finish, s_a, s_b)


def _mixer_a(p, bias):
    b, s, _ = p.shape
    n_blk = s // TB
    nd = 2 * n_blk - 1
    n_pairs = A_HEADS // 2

    def col(off):
        return pl.BlockSpec((1, s, LANES), lambda bi, j, off=off: (bi, 0, off // LANES + j))

    return pl.pallas_call(
        functools.partial(_mixer_a_kernel, n_blk=n_blk),
        grid=(b, n_pairs),
        in_specs=[col(A_Q), col(A_K), col(A_V), col(A_G),
                  pl.BlockSpec((2, nd, TB, TB), lambda bi, j: (j, 0, 0, 0))],
        out_specs=pl.BlockSpec((1, s, LANES), lambda bi, j: (bi, 0, j)),
        out_shape=jax.ShapeDtypeStruct((b, s, n_pairs * LANES), F32),
        scratch_shapes=_logit_bufs(s),
        compiler_params=_params("parallel", "parallel"),
        name="mixer_a",
    )(p, p, p, p, bias)


def _mixer_d_kernel(q_ref, k_ref, v_ref, g_ref, bias_ref, lam_ref, sg_ref, o_ref, s_a, s_b, *, n_blk, lam_init):
    k_bf = k_ref[0].astype(BF16)
    v_ext = _with_ones(v_ref[0].astype(BF16))
    lo = _lane_lo(TQ)
    rpb = TQ // TB
    dl = lam_ref[...]
    lam = (jnp.exp(jnp.sum(dl[0:1] * dl[1:2], axis=1, keepdims=True))
           - jnp.exp(jnp.sum(dl[2:3] * dl[3:4], axis=1, keepdims=True)) + lam_init)
    sg = sg_ref[...] * (1.0 - lam_init)

    def qk_store(buf, qi):
        r0 = pl.multiple_of(qi * TQ, TQ)
        q = q_ref[0, pl.ds(r0, TQ), :] * (HEAD ** -0.5 * LOG2E)
        for hh in range(2):
            qm = jnp.where(lo if hh == 0 else jnp.logical_not(lo), q, 0.0).astype(BF16)
            _store_logits(buf, hh, _dot_nt(qm, k_bf),
                          lambda rb, j: bias_ref[0, n_blk - 1 - (qi * rpb + rb) + j])

    def finish(buf, qi):
        r0 = pl.multiple_of(qi * TQ, TQ)
        o = _softmax_pv(buf, 0, v_ext) - lam * _softmax_pv(buf, 1, v_ext)
        o = o * lax.rsqrt(jnp.mean(o * o, axis=1, keepdims=True) + 1e-5) * sg
        o_ref[0, pl.ds(r0, TQ), :] = o * _silu(g_ref[0, pl.ds(r0, TQ), :])

    _attn_loop(n_blk // rpb, qk_store, finish, s_a, s_b)


def _mixer_d(p, bias, d_lambda, subln_g, layer_idx):
    b, s, _ = p.shape
    n_blk = s // TB
    nd = 2 * n_blk - 1
    lam_init = 0.8 - 0.6 * math.exp(-0.3 * layer_idx)

    def col(off):
        return pl.BlockSpec((1, s, LANES), lambda bi, j, off=off: (bi, 0, off // LANES + j))

    return pl.pallas_call(
        functools.partial(_mixer_d_kernel, n_blk=n_blk, lam_init=lam_init),
        grid=(b, D_HEADS),
        in_specs=[col(D_Q), col(D_K), col(D_V), col(D_G),
                  pl.BlockSpec((1, nd, TB, TB), lambda bi, j: (A_HEADS + j, 0, 0, 0)),
                  pl.BlockSpec((4, HEAD), lambda bi, j: (0, 0)),
                  pl.BlockSpec((1, LANES), lambda bi, j: (0, 0))],
        out_specs=pl.BlockSpec((1, s, LANES), lambda bi, j: (bi, 0, j)),
        out_shape=jax.ShapeDtypeStruct((b, s, D_HEADS * LANES), F32),
        scratch_shapes=_logit_bufs(s),
        compiler_params=_params("parallel", "parallel"),
        name="mixer_d",
    )(p, p, p, p, bias, d_lambda, subln_g.reshape(1, LANES))


def _rope_tables(s):
    t = np.arange(s)
    row, colp = t // GRID_W, t % GRID_W
    qtr = HEAD // 4
    freqs = np.float32(ROPE_THETA) ** (-(np.arange(qtr, dtype=np.float32) / np.float32(qtr)))
    lane = np.arange(LANES) % HEAD
    use_col = (lane // (HEAD // 2)) == 1
    second = (lane % (HEAD // 2)) >= qtr
    pos = np.where(use_col[None, :], colp[:, None], row[:, None]).astype(np.float32)
    ang = (pos * freqs[lane % qtr][None, :]).astype(np.float32)
    cos = np.cos(ang).astype(np.float32)
    sin = np.where(second[None, :], np.sin(ang), -np.sin(ang)).astype(np.float32)
    return jnp.asarray(cos), jnp.asarray(sin)


def _norm_rope(x, gain, cos, sin, lo, first):
    ms = _seg_sum(x * x, lo) * (1.0 / HEAD)
    x = x * lax.rsqrt(ms + 1e-6) * gain
    qtr = HEAD // 4
    partner = jnp.where(first, pltpu.roll(x, LANES - qtr, axis=1), pltpu.roll(x, qtr, axis=1))
    return x * cos + partner * sin


def _mixer_c_kernel(q_ref, k_ref, v_ref, g_ref, cos_ref, sin_ref, qg_ref, kg_ref, o_ref, s_a, s_b, krot_ref,
                    *, n_blk, s):
    pair = pl.program_id(1)
    grp = pair // 2
    lo_s = _lane_lo(s)

    @pl.when(pair == 0)
    def _():
        lane_s = lax.broadcasted_iota(jnp.int32, (s, LANES), 1)
        first_s = (lane_s % (HEAD // 2)) < (HEAD // 4)
        krot_ref[...] = _norm_rope(k_ref[0], kg_ref[...], cos_ref[...], sin_ref[...], lo_s, first_s)

    k = krot_ref[...]
    v = v_ref[0]
    k_sw = pltpu.roll(k, HEAD, axis=1)
    v_sw = pltpu.roll(v, HEAD, axis=1)
    keep = jnp.logical_xor(lo_s, grp == 1)
    k_bf = jnp.where(keep, k, k_sw).astype(BF16)
    v_ext = _with_ones(jnp.where(keep, v, v_sw).astype(BF16))
    lo = _lane_lo(TQ)
    lane = lax.broadcasted_iota(jnp.int32, (TQ, LANES), 1)
    first = (lane % (HEAD // 2)) < (HEAD // 4)

    def qk_store(buf, qi):
        rows = slice(qi * TQ, (qi + 1) * TQ)
        q = _norm_rope(q_ref[0, rows, :], qg_ref[...], cos_ref[rows, :], sin_ref[rows, :], lo, first)
        q = q * (HEAD ** -0.5 * LOG2E)
        for hh in range(2):
            qm = jnp.where(lo if hh == 0 else jnp.logical_not(lo), q, 0.0).astype(BF16)
            _store_logits(buf, hh, _dot_nt(qm, k_bf))

    def finish(buf, qi):
        rows = slice(qi * TQ, (qi + 1) * TQ)
        o = jnp.where(lo, _softmax_pv(buf, 0, v_ext), _softmax_pv(buf, 1, v_ext))
        o_ref[0, rows, :] = o * _silu(g_ref[0, rows, :])

    _attn_windows(n_blk, qk_store, finish, s_a, s_b)


def _mixer_c(p, cos, sin, qn_g, kn_g):
    b, s, _ = p.shape
    n_blk = s // TQ
    n_pairs = 4

    def col(off):
        return pl.BlockSpec((1, s, LANES), lambda bi, j, off=off: (bi, 0, off // LANES + j))

    def fixed(off):
        return pl.BlockSpec((1, s, LANES), lambda bi, j, off=off: (bi, 0, off // LANES))

    tab = pl.BlockSpec((s, LANES), lambda bi, j: (0, 0))
    gain = pl.BlockSpec((1, LANES), lambda bi, j: (0, 0))
    return pl.pallas_call(
        functools.partial(_mixer_c_kernel, n_blk=n_blk, s=s),
        grid=(b, n_pairs),
        in_specs=[col(C_Q), fixed(C_K), fixed(C_V), col(C_G), tab, tab, gain, gain],
        out_specs=pl.BlockSpec((1, s, LANES), lambda bi, j: (bi, 0, j)),
        out_shape=jax.ShapeDtypeStruct((b, s, n_pairs * LANES), F32),
        scratch_shapes=_logit_bufs(s) + [pltpu.VMEM((s, LANES), F32)],
        compiler_params=_params("parallel", "arbitrary"),
        name="mixer_c",
    )(p, p, p, p, cos, sin, jnp.tile(qn_g, 2).reshape(1, LANES), jnp.tile(kn_g, 2).reshape(1, LANES))


def _mixer_m_kernel(q_ref, g_ref, mem_ref, wk_ref, wv_ref, o_ref, s_a, s_b, *, n_blk):
    mem_bf = mem_ref[0]
    k_bf = _dot(mem_bf, wk_ref[...].astype(BF16)).astype(BF16)
    v_ext = _with_ones(_dot(mem_bf, wv_ref[...].astype(BF16)).astype(BF16))
    lo = _lane_lo(TQ)

    def qk_store(buf, qi):
        rows = slice(qi * TQ, (qi + 1) * TQ)
        q = q_ref[0, rows, :] * (HEAD ** -0.5 * LOG2E)
        for hh in range(2):
            qm = jnp.where(lo if hh == 0 else jnp.logical_not(lo), q, 0.0).astype(BF16)
            _store_logits(buf, hh, _dot_nt(qm, k_bf))

    def finish(buf, qi):
        rows = slice(qi * TQ, (qi + 1) * TQ)
        o = jnp.where(lo, _softmax_pv(buf, 0, v_ext), _softmax_pv(buf, 1, v_ext))
        o_ref[0, rows, :] = o * _silu(g_ref[0, rows, :])

    _attn_windows(n_blk, qk_store, finish, s_a, s_b)


def _mixer_m(p, mem_bf, w_mem_kv, layer):
    b, s, _ = p.shape
    n_mem, d = mem_bf.shape[1:]
    n_blk = s // TQ
    n_pairs = 2

    def col(off):
        return pl.BlockSpec((1, s, LANES), lambda bi, j, off=off: (bi, 0, off // LANES + j))

    return pl.pallas_call(
        functools.partial(_mixer_m_kernel, n_blk=n_blk),
        grid=(b, n_pairs),
        in_specs=[col(M_Q), col(M_G),
                  pl.BlockSpec((1, n_mem, d), lambda bi, j: (bi, 0, 0)),
                  pl.BlockSpec((None, d, LANES), lambda bi, j: (layer, 0, j)),
                  pl.BlockSpec((None, d, LANES), lambda bi, j: (layer, 0, n_pairs + j))],
        out_specs=pl.BlockSpec((1, s, LANES), lambda bi, j: (bi, 0, j)),
        out_shape=jax.ShapeDtypeStruct((b, s, n_pairs * LANES), F32),
        scratch_shapes=_logit_bufs(n_mem),
        compiler_params=_params("parallel", "parallel"),
        name="mixer_m",
    )(p, p, mem_bf, w_mem_kv, w_mem_kv)


def _split3_dot(tri_bf, x):
    h1 = x.astype(BF16)
    r1 = x - h1.astype(F32)
    h2 = r1.astype(BF16)
    h3 = (r1 - h2.astype(F32)).astype(BF16)
    return _dot(tri_bf, h1) + _dot(tri_bf, h2) + _dot(tri_bf, h3)


def _rwkv_kernel(r_ref, k_ref, v_ref, wa_ref, w0_ref, wup_ref, a0_ref, aup_ref,
                 kk_ref, ka_ref, rk_ref, yf_ref, yb_ref, bvf_ref, bvb_ref, state_ref, *, nb, nch):
    C = CHUNK
    R = nch * C
    c = pl.program_id(1)

    @pl.when(c == 0)
    def _():
        state_ref[...] = jnp.zeros_like(state_ref)

    n_pairs = r_ref.shape[2] // LANES
    width = r_ref.shape[2]
    tt = lax.broadcasted_iota(jnp.int32, (C, LANES), 0)
    ss = lax.broadcasted_iota(jnp.int32, (C, LANES), 1) % C
    eye_bf = jnp.where(tt == ss, 1.0, 0.0).astype(BF16)
    rr = lax.broadcasted_iota(jnp.int32, (LANES, LANES), 0)
    cc = lax.broadcasted_iota(jnp.int32, (LANES, LANES), 1)
    eye = rr == cc
    tr = lax.broadcasted_iota(jnp.int32, (R, R), 0)
    tc = lax.broadcasted_iota(jnp.int32, (R, R), 1)
    same_chunk = (tr // C) == (tc // C)
    lo = _lane_lo(C)
    hi = jnp.logical_not(lo)
    lo_bf = jnp.where(lo, 1.0, 0.0).astype(BF16)
    hi_bf = jnp.where(hi, 1.0, 0.0).astype(BF16)
    lo_r = _lane_lo(R)

    def stack(x):
        return jnp.concatenate([jnp.where(lo, x, 0.0), jnp.where(hi, x, 0.0)], axis=0)

    def stack_bf(x):
        return jnp.concatenate([x * lo_bf, x * hi_bf], axis=0)

    units = []
    for e in range(2):
        rev = e == 1
        blk = (nb - 1 - c) if rev else c
        r0 = pl.multiple_of(blk * R, R)
        bv_ref = bvb_ref if rev else bvf_ref
        strict = (tt < ss) if rev else (tt > ss)
        incl = (tt <= ss) if rev else (tt >= ss)
        tri_bf = jnp.where(jnp.logical_and(same_chunk, (tr <= tc) if rev else (tr >= tc)), 1.0, 0.0).astype(BF16)
        last = 0 if rev else C - 1
        up, dn = (ss, tt) if rev else (tt, ss)
        level_masks = []
        m = 1
        while m < C:
            same_blk = ((tt ^ ss) >> (int(math.log2(m)) + 1)) == 0
            lvl_mask = jnp.logical_and(same_blk, jnp.logical_and((up & m) != 0, (dn & m) == 0))
            level_masks.append(jnp.where(lvl_mask, 1.0, 0.0).astype(BF16))
            m *= 2

        rs = r_ref[0, pl.ds(r0, R), :]
        ks = k_ref[0, pl.ds(r0, R), :]
        vs = v_ref[0, pl.ds(r0, R), :]
        wa = wa_ref[0, pl.ds(r0, R), :]
        sel = lo_r if e == 0 else jnp.logical_not(lo_r)
        wd = jnp.where(sel, jnp.tanh(wa[:, :LANES]), 0.0).astype(BF16)
        ad = jnp.where(sel, wa[:, LANES:], 0.0).astype(BF16)
        zw = w0_ref[e:e + 1, :] + _dot(wd, wup_ref[...])
        lw = -math.exp(-0.5) / (1.0 + jnp.exp(-zw))
        za = a0_ref[e:e + 1, :] + _dot(ad, aup_ref[...])
        a_sig = 1.0 / (1.0 + jnp.exp(-za))
        kk = ks * kk_ref[...]
        kk2 = kk * kk
        nrm2 = jnp.concatenate([_seg_sum(kk2[:, i * LANES:(i + 1) * LANES], lo_r) for i in range(n_pairs)], axis=1)
        kkn = kk * lax.rsqrt(jnp.maximum(nrm2, 1e-24))
        ka = ka_ref[...]
        ke = ks * ((1.0 - ka) + ka * a_sig)
        be = kkn * a_sig
        rkr = rs * ke * rk_ref[...]
        bonus = jnp.concatenate([_seg_sum(rkr[:, i * LANES:(i + 1) * LANES], lo_r) for i in range(n_pairs)], axis=1)
        bv_ref[0] = bonus * vs

        l_incl = _split3_dot(tri_bf, lw)
        l_tot = jnp.concatenate(
            [jnp.broadcast_to(l_incl[j * C + last:j * C + last + 1, :], (C, width)) for j in range(nch)], axis=0)
        w_inv = jnp.exp(-l_incl)
        d_end = jnp.exp(l_tot)
        w_end = d_end * w_inv
        at = -kkn * jnp.exp(l_incl - lw)
        rt = rs * jnp.exp(l_incl)
        bt = be * w_inv
        kt = ke * w_inv
        bh = be * w_end
        kh = ke * w_end
        for j in range(nch):
            rows = slice(j * C, (j + 1) * C)
            for pr in range(n_pairs):
                sl = slice(pr * LANES, (pr + 1) * LANES)
                units.append(dict(
                    e=e, j=j, pr=pr, strict=strict, incl=incl, levels=level_masks,
                    at=at[rows, sl], rt=rt[rows, sl], bt=bt[rows, sl], kt=kt[rows, sl],
                    bh=bh[rows, sl], kh=kh[rows, sl], v=vs[rows, sl], d_end=d_end[j * C:j * C + 1, sl]))

    for u in units:
        u["at_bf"] = u["at"].astype(BF16)
        u["v_st"] = stack_bf(u["v"].astype(BF16))
        lhs = jnp.concatenate([u["at_bf"], u["rt"].astype(BF16)], axis=0)
        rhs = jnp.concatenate([stack_bf(u["bt"].astype(BF16)), stack_bf(u["kt"].astype(BF16))], axis=0)
        g = _dot_nt(lhs, rhs)
        u["a_ab"] = jnp.where(u["strict"], g[:C, :LANES], 0.0).astype(BF16)
        u["a_ak"] = jnp.where(u["strict"], g[:C, LANES:], 0.0).astype(BF16)
        u["ly"] = jnp.concatenate([jnp.where(u["incl"], g[C:, :LANES], 0.0),
                                   jnp.where(u["incl"], g[C:, LANES:], 0.0)], axis=1).astype(BF16)
        u["t"] = eye_bf + u["a_ab"] * u["levels"][0]
    for lvl in range(1, len(units[0]["levels"])):
        for u in units:
            u["tmp"] = _dot(u["a_ab"] * u["levels"][lvl], stack_bf(u["t"])).astype(BF16)
        for u in units:
            u["t"] = u["t"] + _dot(u["t"], stack_bf(u["tmp"])).astype(BF16)
    for u in units:
        u["akv"] = _dot(u["a_ak"], u["v_st"]).astype(BF16)
    for u in units:
        x = jnp.concatenate([stack_bf(u["at_bf"]), stack_bf(u["akv"])], axis=1)
        pq = _dot(u["t"], x).astype(BF16)
        u["ry"] = jnp.concatenate(
            [jnp.concatenate([stack_bf(pq[:, :LANES]), stack_bf(pq[:, LANES:])], axis=1),
             jnp.concatenate([jnp.zeros((LANES, LANES), BF16), u["v_st"]], axis=1)], axis=0)
        ls = jnp.concatenate([stack(u["bh"]), stack(u["kh"])], axis=0)
        u["lyz"] = jnp.concatenate([u["ly"], ls.T.astype(BF16)], axis=0)
    for u in units:
        yz = _dot(u["lyz"], u["ry"])
        yy = yz[:C]
        zz = yz[C:]
        u["rpm"] = jnp.concatenate([u["rt"] + yy[:, :LANES], zz[:, :LANES]], axis=0).astype(BF16)
        u["y0"] = yy[:, LANES:]
        u["z"] = zz[:, LANES:]
        d_diag = jnp.where(eye, jnp.broadcast_to(u["d_end"], (LANES, LANES)), 0.0)
        u["d_col"] = jnp.sum(d_diag, axis=1, keepdims=True)

    by_key = {(u["e"], u["j"], u["pr"]): u for u in units}
    for e in range(2):
        y_ref = yb_ref if e == 1 else yf_ref
        order = list(range(nch))[::-1] if e == 1 else list(range(nch))
        ys = {}
        states = [state_ref[e, pr] for pr in range(n_pairs)]
        for j in order:
            for pr in range(n_pairs):
                u = by_key[(e, j, pr)]
                ym = _dot(u["rpm"], states[pr].astype(BF16))
                ys[(j, pr)] = ym[:C] + u["y0"]
                states[pr] = u["d_col"] * states[pr] + ym[C:] + u["z"]
        for pr in range(n_pairs):
            state_ref[e, pr] = states[pr]
        y_ref[0] = jnp.concatenate(
            [jnp.concatenate([ys[(j, pr)] for pr in range(n_pairs)], axis=1) for j in range(nch)], axis=0)


RWKV_CHUNKS_PER_STEP = 4


def _mixer_b_scan(p, w0, w_up, a0, a_up, k_k, k_a, r_k):
    b, s, _ = p.shape
    width = B_K - B_R
    nch = RWKV_CHUNKS_PER_STEP
    rows = nch * CHUNK
    nb = s // rows
    n_pairs = width // LANES

    def seq(off, w):
        return pl.BlockSpec((1, s, w), lambda bi, c, off=off, w=w: (bi, 0, off // w))

    def full(shape):
        return pl.BlockSpec(shape, lambda bi, c: tuple(0 for _ in shape))

    out_f = pl.BlockSpec((1, rows, width), lambda bi, c: (bi, c, 0))
    out_b = pl.BlockSpec((1, rows, width), lambda bi, c: (bi, nb - 1 - c, 0))
    sds = jax.ShapeDtypeStruct((b, s, width), F32)
    lora = 2 * HEAD
    return pl.pallas_call(
        functools.partial(_rwkv_kernel, nb=nb, nch=nch),
        grid=(b, nb),
        in_specs=[seq(B_R, width), seq(B_K, width), seq(B_V, width), seq(B_WA, 2 * LANES),
                  full((2, width)), full((lora, width)),
                  full((2, width)), full((lora, width)), full((1, width)), full((1, width)), full((1, width))],
        out_specs=[out_f, out_b, out_f, out_b],
        out_shape=[sds, sds, sds, sds],
        scratch_shapes=[pltpu.VMEM((2, n_pairs, LANES, LANES), F32)],
        compiler_params=_params("parallel", "arbitrary"),
        name="mixer_b",
    )(p, p, p, p, w0, w_up.reshape(lora, width).astype(BF16), a0, a_up.reshape(lora, width).astype(BF16),
      k_k.reshape(1, width), k_a.reshape(1, width), r_k.reshape(1, width))


def _merge_kernel(h_ref, hb_ref, oa_ref, yf_ref, yb_ref, bvf_ref, bvb_ref, gb0_ref, gb1_ref, oc_ref, od_ref,
                  om_ref, wg_ref, bg_ref, wb_ref, wo_ref, lng_ref, lnb_ref, bg_g_ref, bg_b_ref,
                  hn_ref, hnb_ref, *, alpha):
    d = h_ref.shape[1]
    tm = h_ref.shape[0]
    lo = _lane_lo(tm)
    y = yf_ref[...] + yb_ref[...]
    n_pairs = y.shape[1] // LANES
    gn = []
    for i in range(n_pairs):
        yp = y[:, i * LANES:(i + 1) * LANES]
        mu = _seg_sum(yp, lo) * (1.0 / HEAD)
        dy = yp - mu
        var = _seg_sum(dy * dy, lo) * (1.0 / HEAD)
        gn.append(dy * lax.rsqrt(var + B_GN_EPS))
    gn = jnp.concatenate(gn, axis=1) * bg_g_ref[...] + bg_b_ref[...]
    gb = jnp.concatenate([gb0_ref[...], gb1_ref[...]], axis=1)
    ob = (gn + bvf_ref[...] + bvb_ref[...]) * _silu(gb)

    hb = hb_ref[...]
    branches = (oa_ref[...], ob, oc_ref[...], od_ref[...], om_ref[...])
    acc = None
    row = 0
    for i, o in enumerate(branches):
        wdt = o.shape[1]
        gate = 1.0 / (1.0 + jnp.exp(-(_dot(hb, wg_ref[:, i * d:(i + 1) * d]) + bg_ref[:, i * d:(i + 1) * d])))
        proj = _dot(o.astype(BF16), wb_ref[row:row + wdt, :])
        term = gate * proj
        acc = term if acc is None else acc + term
        row += wdt
    out = _dot(acc.astype(BF16), wo_ref[...])
    z = alpha * h_ref[...] + out
    mu = jnp.mean(z, -1, keepdims=True)
    dz = z - mu
    var = jnp.mean(dz * dz, -1, keepdims=True)
    hn = dz * lax.rsqrt(var + 1e-5) * lng_ref[...] + lnb_ref[...]
    hn_ref[...] = hn
    hnb_ref[...] = hn.astype(BF16)


def _merge(h, hb, o_a, yf, yb, bvf, bvb, p2, o_c, o_d, o_m, layer, w_gate, b_gate, w_branch, w_out, ln_g, ln_b,
           bln_g, bln_b):
    n, d = h.shape
    tm = MERGE_ROWS
    alpha = (2 * DEPTH) ** 0.25
    wb = yf.shape[1]

    def rows(t):
        return pl.BlockSpec((tm, t.shape[1]), lambda i: (i, 0))

    def full(shape):
        return pl.BlockSpec(shape, lambda i: tuple(0 for _ in shape))

    def of_layer(w):
        return pl.BlockSpec((None,) + w.shape[1:], lambda i: (layer, 0, 0))

    half = wb // 2
    assert B_G % half == 0
    gb_specs = pl.BlockSpec((tm, half), lambda i: (i, B_G // half))
    gb_specs2 = pl.BlockSpec((tm, half), lambda i: (i, B_G // half + 1))
    return pl.pallas_call(
        functools.partial(_merge_kernel, alpha=alpha),
        grid=(n // tm,),
        in_specs=[rows(h), rows(hb), rows(o_a), rows(yf), rows(yb), rows(bvf), rows(bvb),
                  gb_specs, gb_specs2, rows(o_c), rows(o_d), rows(o_m),
                  of_layer(w_gate), full((1, b_gate.shape[0])), of_layer(w_branch), of_layer(w_out),
                  full((1, d)), full((1, d)), full((1, wb)), full((1, wb))],
        out_specs=[rows(h), rows(hb)],
        out_shape=[jax.ShapeDtypeStruct((n, d), F32), jax.ShapeDtypeStruct((n, d), BF16)],
        compiler_params=_params("parallel"),
        name="merge",
    )(h, hb, o_a, yf, yb, bvf, bvb, p2, p2, o_c, o_d, o_m,
      w_gate, b_gate.reshape(1, -1), w_branch, w_out, ln_g.reshape(1, d), ln_b.reshape(1, d),
      bln_g.reshape(1, wb), bln_b.reshape(1, wb))


def kernel(x, mem, ln_in_g, ln_in_b, rel_bias, w_in, shift_mu, rwkv_w0, rwkv_w_up, rwkv_a0, rwkv_a_up,
           rwkv_k_k, rwkv_k_a, rwkv_r_k, rwkv_ln_g, rwkv_ln_b, c_qnorm_g, c_knorm_g, d_lambda, d_subln_g,
           w_mem_kv, w_branch, w_gate, b_gate, w_out, ln_g, ln_b):
    b, s, d = x.shape
    n = b * s
    bias = _bias_tiles(rel_bias, s // TB)
    cos, sin = _rope_tables(s)
    mem_bf = mem.astype(BF16)
    h, hb = _ln_in(x.reshape(n, d), ln_in_g, ln_in_b)
    w_gate, w_branch, w_out = (w.astype(BF16) for w in (w_gate, w_branch, w_out))
    for l in range(DEPTH):
        p2 = _proj_in(hb, w_in, l, shift_mu[l], s)
        p = p2.reshape(b, s, IN_COLS)
        o_a = _mixer_a(p, bias)
        yf, yb, bvf, bvb = _mixer_b_scan(p, rwkv_w0[l], rwkv_w_up[l], rwkv_a0[l], rwkv_a_up[l],
                                         rwkv_k_k[l], rwkv_k_a[l], rwkv_r_k[l])
        o_c = _mixer_c(p, cos, sin, c_qnorm_g[l], c_knorm_g[l])
        o_d = _mixer_d(p, bias, d_lambda[l], d_subln_g[l], l)
        o_m = _mixer_m(p, mem_bf, w_mem_kv, l)
        flat = lambda t: t.reshape(n, t.shape[-1])
        h, hb = _merge(h, hb, flat(o_a), flat(yf), flat(yb), flat(bvf), flat(bvb), p2, flat(o_c), flat(o_d),
                       flat(o_m), l, w_gate, b_gate[l], w_branch, w_out, ln_g[l], ln_b[l],
                       rwkv_ln_g[l], rwkv_ln_b[l])
    return h.reshape(b, s, d)
```
